```python
import jax, jax.numpy as jnp
from jax import lax
import numpy as np

D_MODEL = 2048
BATCH = 2
SEQ = 4096
DEPTH = 2

EPS = 1e-6
N_EVEN = (DEPTH + 1) // 2
N_ODD = DEPTH // 2

D_MLSTM = D_MODEL // 2
MLSTM_HEADS = 4
MLSTM_DH = D_MLSTM // MLSTM_HEADS
MLSTM_CHUNK = 64
CONV_K = 4
D_POOL = D_MODEL // 2
POOL_WINDOWS = (2, 4, 8, 16)
POOL_GROUPS = len(POOL_WINDOWS)
POOL_DG = D_POOL // POOL_GROUPS
D_MIX = D_MLSTM + D_POOL
D_IN_EVEN = 4 * D_MLSTM + 2 * MLSTM_HEADS + D_POOL
D_FF = 5632

SB_HEADS = 16
SB_DH = D_MODEL // SB_HEADS
SB_BLOCK = 128
N_EXPERTS = 8
TOP_K = 2
D_FF_EXPERT = 7168

kernel_name = "hybrid_mlstm_pool_stickbreak_moe_adaln"

F32 = jnp.float32


def rmsnorm(x, g):
    xf = x.astype(F32)
    y = xf * lax.rsqrt(jnp.mean(xf * xf, axis=-1, keepdims=True) + EPS)
    return (y * g.astype(F32)).astype(x.dtype)


def modulate(h, shift, scale):
    return h * (1 + scale[:, None, :]) + shift[:, None, :]


def causal_depthwise_conv(x, w, b):
    C = x.shape[-1]
    y = lax.conv_general_dilated(
        x, w[:, None, :].astype(x.dtype), window_strides=(1,),
        padding=[(CONV_K - 1, 0)], dimension_numbers=("NWC", "WIO", "NWC"),
        feature_group_count=C)
    return y + b.astype(x.dtype)


def mlstm_chunkwise(q, k, v, i_raw, f_raw):
    B, H, S, dh = q.shape
    L = MLSTM_CHUNK
    NC = S // L
    dtype = v.dtype
    q = q.astype(F32) * (dh ** -0.5)
    k = k.astype(F32)
    v = v.astype(F32)
    logf = jax.nn.log_sigmoid(f_raw.astype(F32))
    i_raw = i_raw.astype(F32)

    def to_chunks(a):
        a = a.reshape(a.shape[:2] + (NC, L) + a.shape[3:])
        return jnp.moveaxis(a, 2, 0)

    qc, kc, vc, ic = to_chunks(q), to_chunks(k), to_chunks(v), to_chunks(i_raw)
    bc = jnp.cumsum(to_chunks(logf), axis=-1)
    causal = jnp.tril(jnp.ones((L, L), dtype=bool))

    def step(carry, xs):
        C, n, m = carry
        qb, kb, vb, ib, bb = xs
        D = bb[..., :, None] - bb[..., None, :] + ib[..., None, :]
        D = jnp.where(causal, D, -jnp.inf)
        inter = bb + m[..., None]
        m_t = jnp.maximum(inter, jnp.max(D, axis=-1))
        w_inter = jnp.exp(inter - m_t)
        W = jnp.exp(D - m_t[..., None]) * jnp.einsum("bhtd,bhsd->bhts", qb, kb)
        num = (w_inter[..., None] * jnp.einsum("bhtd,bhde->bhte", qb, C)
               + jnp.einsum("bhts,bhse->bhte", W, vb))
        den = w_inter * jnp.einsum("bhtd,bhd->bht", qb, n) + jnp.sum(W, axis=-1)
        h = num / jnp.maximum(jnp.abs(den), jnp.exp(-m_t))[..., None]
        g = bb[..., -1]
        wl = g[..., None] - bb + ib
        m_new = jnp.maximum(g + m, jnp.max(wl, axis=-1))
        decay = jnp.exp(g + m - m_new)
        wk = jnp.exp(wl - m_new[..., None])[..., None] * kb
        C_new = decay[..., None, None] * C + jnp.einsum("bhsd,bhse->bhde", wk, vb)
        n_new = decay[..., None] * n + jnp.sum(wk, axis=-2)
        return (C_new, n_new, m_new), h

    init = (jnp.zeros((B, H, dh, dh), F32), jnp.zeros((B, H, dh), F32), jnp.zeros((B, H), F32))
    _, hc = lax.scan(step, init, (qc, kc, vc, ic, bc))
    return jnp.moveaxis(hc, 0, 2).reshape(B, H, S, dh).astype(dtype)


def multiscale_pool(xp, pool_w, pool_scale):
    B, S, _ = xp.shape
    xg = xp.reshape(B, S, POOL_GROUPS, POOL_DG).astype(F32)
    cs = jnp.cumsum(xg, axis=1)
    pos = jnp.arange(S)
    outs = []
    for g, w in enumerate(POOL_WINDOWS):
        c_g = cs[:, :, g]
        lagged = jnp.pad(c_g, ((0, 0), (w, 0), (0, 0)))[:, :S]
        cnt = jnp.minimum(pos + 1, w).astype(F32)[None, :, None]
        outs.append((c_g - lagged) / cnt - xg[:, :, g])
    pooled = jnp.stack(outs, axis=2).astype(xp.dtype)
    mixed = jnp.einsum("bsgc,gcd->bsgd", pooled, pool_w)
    return mixed.reshape(B, S, D_POOL) * pool_scale


def mlstm_pool_mixer(h, w_in, b_gates, conv_w, conv_b, head_g, pool_w, pool_scale, w_out):
    B, S, _ = h.shape
    proj = h @ w_in
    qk, v, o, gates, xp = jnp.split(
        proj, [2 * D_MLSTM, 3 * D_MLSTM, 4 * D_MLSTM, 4 * D_MLSTM + 2 * MLSTM_HEADS], axis=-1)
    qk = jax.nn.silu(causal_depthwise_conv(qk, conv_w, conv_b))
    q, k = jnp.split(qk, 2, axis=-1)
    gates = gates.astype(F32) + b_gates.astype(F32)
    i_raw, f_raw = jnp.split(gates, 2, axis=-1)

    def heads(a):
        return a.reshape(B, S, MLSTM_HEADS, MLSTM_DH).transpose(0, 2, 1, 3)

    hm = mlstm_chunkwise(heads(q), heads(k), heads(v),
                         i_raw.transpose(0, 2, 1), f_raw.transpose(0, 2, 1))
    hm = rmsnorm(hm.transpose(0, 2, 1, 3), head_g.reshape(MLSTM_HEADS, MLSTM_DH))
    hm = hm.reshape(B, S, D_MLSTM) * jax.nn.sigmoid(o)
    hp = multiscale_pool(xp, pool_w, pool_scale)
    return jnp.concatenate([hm, hp], axis=-1) @ w_out


def stick_breaking_attention(q, k, v):
    B, S, H, dh = q.shape
    scale = dh ** -0.5
    outs = []
    for blk in range(S // SB_BLOCK):
        t0 = blk * SB_BLOCK
        t1 = t0 + SB_BLOCK
        qb, kb, vb = q[:, t0:t1], k[:, :t1], v[:, :t1]
        z = jnp.einsum("bthd,bshd->bhts", qb, kb).astype(F32) * scale
        strict = jnp.arange(t1)[None, :] < (t0 + jnp.arange(SB_BLOCK))[:, None]
        log_one_minus = jnp.where(strict, jax.nn.log_sigmoid(-z), 0.0)
        log_remain = lax.cumsum(log_one_minus, axis=3, reverse=True) - log_one_minus
        A = jnp.where(strict, jnp.exp(jax.nn.log_sigmoid(z) + log_remain), 0.0)
        outs.append(jnp.einsum("bhts,bshd->bthd", A.astype(v.dtype), vb))
    return jnp.concatenate(outs, axis=1)


def stick_breaking_mixer(h, w_qkv, w_o):
    B, S, _ = h.shape
    q, k, v = jnp.split(h @ w_qkv, 3, axis=-1)
    shp = (B, S, SB_HEADS, SB_DH)
    o = stick_breaking_attention(q.reshape(shp), k.reshape(shp), v.reshape(shp))
    return o.reshape(B, S, D_MODEL) @ w_o


def swiglu(h, w1, w3, w2):
    return (jax.nn.silu(h @ w1) * (h @ w3)) @ w2


def moe_swiglu(h, w_router, w1, w3, w2):
    B, S, D = h.shape
    t = h.reshape(B * S, D)
    logits = (t @ w_router).astype(F32)
    top_val, top_idx = lax.top_k(logits, TOP_K)
    gate = jax.nn.softmax(top_val, axis=-1)
    combine = jnp.sum(jax.nn.one_hot(top_idx, N_EXPERTS, dtype=F32) * gate[..., None], axis=1)
    combine = combine.astype(t.dtype)
    out = jnp.zeros_like(t)
    for e in range(N_EXPERTS):
        out = out + combine[:, e:e + 1] * swiglu(t, w1[e], w3[e], w2[e])
    return out.reshape(B, S, D)


def setup_inputs(seed: int = 0) -> dict:
    key = jax.random.key(seed)
    ks = jax.random.split(key, 26)

    def nrm(k, shape, s):
        return jax.random.normal(k, shape, F32) * s

    x = nrm(ks[0], (BATCH, SEQ, D_MODEL), 1.0)
    c = nrm(ks[1], (BATCH, D_MODEL), 1.0)
    ada_w = nrm(ks[2], (DEPTH, D_MODEL, 6 * D_MODEL), 0.5 * D_MODEL ** -0.5)
    ada_b = nrm(ks[3], (DEPTH, 6 * D_MODEL), 0.02)
    norm_mix_g = 1.0 + nrm(ks[4], (DEPTH, D_MODEL), 0.05)
    norm_ffn_g = 1.0 + nrm(ks[5], (DEPTH, D_MODEL), 0.05)
    norm_final_g = 1.0 + nrm(ks[6], (D_MODEL,), 0.05)
    ev_w_in = nrm(ks[7], (N_EVEN, D_MODEL, D_IN_EVEN), D_MODEL ** -0.5)
    i_bias = nrm(ks[8], (N_EVEN, MLSTM_HEADS), 0.1)
    f_bias = jnp.linspace(3.0, 6.0, MLSTM_HEADS, dtype=F32) + nrm(ks[9], (N_EVEN, MLSTM_HEADS), 0.1)
    ev_b_gates = jnp.concatenate([i_bias, f_bias], axis=-1)
    ev_conv_w = nrm(ks[10], (N_EVEN, CONV_K, 2 * D_MLSTM), CONV_K ** -0.5)
    ev_conv_b = nrm(ks[11], (N_EVEN, 2 * D_MLSTM), 0.02)
    ev_head_g = 1.0 + nrm(ks[12], (N_EVEN, D_MLSTM), 0.05)
    ev_pool_w = nrm(ks[13], (N_EVEN, POOL_GROUPS, POOL_DG, POOL_DG), POOL_DG ** -0.5)
    ev_pool_scale = 1.0 + nrm(ks[14], (N_EVEN, D_POOL), 0.1)
    ev_w_out = nrm(ks[15], (N_EVEN, D_MIX, D_MODEL), D_MIX ** -0.5)
    ev_ffn_w1 = nrm(ks[16], (N_EVEN, D_MODEL, D_FF), D_MODEL ** -0.5)
    ev_ffn_w3 = nrm(ks[17], (N_EVEN, D_MODEL, D_FF), D_MODEL ** -0.5)
    ev_ffn_w2 = nrm(ks[18], (N_EVEN, D_FF, D_MODEL), D_FF ** -0.5)
    od_w_qkv = nrm(ks[19], (N_ODD, D_MODEL, 3 * D_MODEL), D_MODEL ** -0.5)
    od_w_o = nrm(ks[20], (N_ODD, D_MODEL, D_MODEL), D_MODEL ** -0.5)
    od_router = nrm(ks[21], (N_ODD, D_MODEL, N_EXPERTS), D_MODEL ** -0.5)
    od_moe_w1 = nrm(ks[22], (N_ODD, N_EXPERTS, D_MODEL, D_FF_EXPERT), D_MODEL ** -0.5)
    od_moe_w3 = nrm(ks[23], (N_ODD, N_EXPERTS, D_MODEL, D_FF_EXPERT), D_MODEL ** -0.5)
    od_moe_w2 = nrm(ks[24], (N_ODD, N_EXPERTS, D_FF_EXPERT, D_MODEL), D_FF_EXPERT ** -0.5)
    return {
        "x": x, "c": c, "ada_w": ada_w, "ada_b": ada_b,
        "norm_mix_g": norm_mix_g, "norm_ffn_g": norm_ffn_g, "norm_final_g": norm_final_g,
        "ev_w_in": ev_w_in, "ev_b_gates": ev_b_gates, "ev_conv_w": ev_conv_w,
        "ev_conv_b": ev_conv_b, "ev_head_g": ev_head_g, "ev_pool_w": ev_pool_w,
        "ev_pool_scale": ev_pool_scale, "ev_w_out": ev_w_out,
        "ev_ffn_w1": ev_ffn_w1, "ev_ffn_w3": ev_ffn_w3, "ev_ffn_w2": ev_ffn_w2,
        "od_w_qkv": od_w_qkv, "od_w_o": od_w_o, "od_router": od_router,
        "od_moe_w1": od_moe_w1, "od_moe_w3": od_moe_w3, "od_moe_w2": od_moe_w2,
    }


def reference(x, c, ada_w, ada_b, norm_mix_g, norm_ffn_g, norm_final_g,
              ev_w_in, ev_b_gates, ev_conv_w, ev_conv_b, ev_head_g, ev_pool_w,
              ev_pool_scale, ev_w_out, ev_ffn_w1, ev_ffn_w3, ev_ffn_w2,
              od_w_qkv, od_w_o, od_router, od_moe_w1, od_moe_w3, od_moe_w2):
    c_act = jax.nn.silu(c)
    for l in range(DEPTH):
        mod = c_act @ ada_w[l] + ada_b[l]
        sh1, sc1, g1, sh2, sc2, g2 = jnp.split(mod, 6, axis=-1)
        h = modulate(rmsnorm(x, norm_mix_g[l]), sh1, sc1)
        if l % 2 == 0:
            e = l // 2
            mix = mlstm_pool_mixer(h, ev_w_in[e], ev_b_gates[e], ev_conv_w[e], ev_conv_b[e],
                                   ev_head_g[e], ev_pool_w[e], ev_pool_scale[e], ev_w_out[e])
        else:
            o = l // 2
            mix = stick_breaking_mixer(h, od_w_qkv[o], od_w_o[o])
        x = x + g1[:, None, :] * mix
        h = modulate(rmsnorm(x, norm_ffn_g[l]), sh2, sc2)
        if l % 2 == 0:
            e = l // 2
            ffn = swiglu(h, ev_ffn_w1[e], ev_ffn_w3[e], ev_ffn_w2[e])
        else:
            o = l // 2
            ffn = moe_swiglu(h, od_router[o], od_moe_w1[o], od_moe_w3[o], od_moe_w2[o])
        x = x + g2[:, None, :] * ffn
    return rmsnorm(x, norm_final_g)
```

```python
import functools

import jax
import jax.numpy as jnp
from jax import lax
from jax.experimental import pallas as pl
from jax.experimental.pallas import tpu as pltpu

F32 = jnp.float32
BF16 = jnp.bfloat16
I32 = jnp.int32

EPS = 1e-6
MLSTM_HEADS = 4
CONV_K = 4
POOL_WINDOWS = (2, 4, 8, 16)
SB_HEADS = 16
N_EXPERTS = 8
TOP_K = 2

LANES = 128
SUBLANES = 8
VMEM_LIMIT = 56 * 1024 * 1024
MLSTM_TILE = 256
POOL_HALO = 16
ATTN_TILE = 256
MOE_SUB = 256

HIGHEST = lax.Precision.HIGHEST


def _params(*sem):
    return pltpu.CompilerParams(dimension_semantics=sem, vmem_limit_bytes=VMEM_LIMIT)


def _tile(n, pref):
    t = min(n, pref)
    assert n % t == 0, (n, pref)
    return t


def _sigmoid(x):
    return 1.0 / (1.0 + jnp.exp(-x))


def _log_sigmoid(x):
    return jnp.minimum(x, 0.0) - jnp.log1p(jnp.exp(-jnp.abs(x)))


def _norm_mod(x, g, shift, scale):
    ms = jnp.mean(x * x, axis=-1, keepdims=True)
    y = x * lax.rsqrt(ms + EPS) * g
    return y * (1.0 + scale) + shift


def _adaln_kernel(cb_ref, w_ref, b_ref, o_ref):
    nb = cb_ref.shape[0]
    for j in range(w_ref.shape[1] // LANES):
        cols = pl.ds(j * LANES, LANES)
        w = w_ref[:, cols]
        for b in range(nb):
            o_ref[pl.ds(b, 1), cols] = jnp.sum(w * cb_ref[b], axis=0, keepdims=True) + b_ref[:, cols]


def _adaln(c, ada_w, ada_b):
    depth, d, n6 = ada_w.shape
    nb = c.shape[0]
    tn = _tile(n6, 1024)
    c_act = c * _sigmoid(c)
    cb = jnp.broadcast_to(c_act[:, :, None], (nb, d, LANES))
    return pl.pallas_call(
        _adaln_kernel,
        grid=(depth, n6 // tn),
        in_specs=[
            pl.BlockSpec((nb, d, LANES), lambda l, j: (0, 0, 0)),
            pl.BlockSpec((None, d, tn), lambda l, j: (l, 0, j)),
            pl.BlockSpec((None, 1, tn), lambda l, j: (l, 0, j)),
        ],
        out_specs=pl.BlockSpec((None, nb, tn), lambda l, j: (l, 0, j)),
        out_shape=jax.ShapeDtypeStruct((depth, nb, n6), F32),
        compiler_params=_params("parallel", "parallel"),
        name="adaln",
    )(cb, ada_w, ada_b.reshape(depth, 1, n6))


def _nm_matmul_kernel(x_ref, g_ref, sh_ref, sc_ref, w_ref, *rest, with_side):
    if with_side:
        ws_ref, bs_ref, o_ref, side_ref, h_ref = rest
    else:
        o_ref, h_ref = rest

    @pl.when(pl.program_id(1) == 0)
    def _():
        h = _norm_mod(x_ref[...], g_ref[...], sh_ref[...], sc_ref[...])
        h_ref[...] = h.astype(BF16)
        if with_side:
            side_ref[...] = jnp.dot(h, ws_ref[...], precision=HIGHEST,
                                    preferred_element_type=F32) + bs_ref[...]

    o_ref[...] = jnp.dot(h_ref[...], w_ref[...].astype(BF16),
                         preferred_element_type=F32).astype(o_ref.dtype)


def _nm_matmul(x, g, shift, scale, w, out_dtype, seq, side=None):
    n, d = x.shape
    nout = w.shape[1]
    tm = _tile(seq, 1024)
    tn = _tile(nout, 512)
    per_b = seq // tm
    nb = shift.shape[0]
    vec = lambda a: a.reshape(nb, 1, d)
    in_specs = [
        pl.BlockSpec((tm, d), lambda i, j: (i, 0)),
        pl.BlockSpec((1, d), lambda i, j: (0, 0)),
        pl.BlockSpec((None, 1, d), lambda i, j: (i // per_b, 0, 0)),
        pl.BlockSpec((None, 1, d), lambda i, j: (i // per_b, 0, 0)),
        pl.BlockSpec((d, tn), lambda i, j: (0, j)),
    ]
    args = [x, g.reshape(1, d), vec(shift), vec(scale), w]
    out_specs = pl.BlockSpec((tm, tn), lambda i, j: (i, j))
    out_shape = jax.ShapeDtypeStruct((n, nout), out_dtype)
    if side is not None:
        in_specs += [pl.BlockSpec((d, LANES), lambda i, j: (0, 0)),
                     pl.BlockSpec((1, LANES), lambda i, j: (0, 0))]
        args += list(side)
        out_specs = [out_specs, pl.BlockSpec((tm, LANES), lambda i, j: (i, 0))]
        out_shape = [out_shape, jax.ShapeDtypeStruct((n, LANES), F32)]
    return pl.pallas_call(
        functools.partial(_nm_matmul_kernel, with_side=side is not None),
        grid=(n // tm, nout // tn),
        in_specs=in_specs,
        out_specs=out_specs,
        out_shape=out_shape,
        scratch_shapes=[pltpu.VMEM((tm, d), BF16)],
        compiler_params=_params("parallel", "arbitrary"),
        name="norm_mod_matmul",
    )(*args)


def _mlstm_kernel(q_ref, qp_ref, k_ref, kp_ref, v_ref, o_ref, cwq_ref, cwk_ref, cbq_ref, cbk_ref,
                  hg_ref, gc_ref, gr_ref, out_ref, ext_ref, c_ref, n_ref, m_ref):
    head = pl.program_id(1)
    chunk = pl.program_id(2)
    nh = pl.num_programs(1)
    L, dh = q_ref.shape

    @pl.when(chunk == 0)
    def _():
        c_ref[...] = jnp.zeros_like(c_ref)
        n_ref[...] = jnp.zeros_like(n_ref)
        m_ref[...] = jnp.zeros_like(m_ref)

    def conv_silu(cur_ref, prev_ref, w_ref, b_ref):
        ext_ref[pl.ds(0, SUBLANES), :] = jnp.where(chunk == 0, 0.0, prev_ref[...])
        ext_ref[pl.ds(SUBLANES, L), :] = cur_ref[...]
        y = b_ref[...] + w_ref[pl.ds(0, 1), :] * ext_ref[pl.ds(SUBLANES - CONV_K + 1, L), :]
        for j in range(1, CONV_K):
            y = y + w_ref[pl.ds(j, 1), :] * ext_ref[pl.ds(SUBLANES - CONV_K + 1 + j, L), :]
        return y * _sigmoid(y)

    q = conv_silu(q_ref, qp_ref, cwq_ref, cbq_ref) * (dh ** -0.5)
    k = conv_silu(k_ref, kp_ref, cwk_ref, cbk_ref)
    qb = q.astype(BF16)
    kb = k.astype(BF16)
    vb = v_ref[...].astype(BF16)

    lane = lax.broadcasted_iota(I32, (L, LANES), 1)
    gc = gc_ref[...]
    i_col = jnp.sum(jnp.where(lane == head, gc, 0.0), axis=1, keepdims=True)
    f_col = jnp.sum(jnp.where(lane == nh + head, gc, 0.0), axis=1, keepdims=True)
    i_row = gr_ref[pl.ds(head, 1), :]
    f_row = gr_ref[pl.ds(nh + head, 1), :]
    logf_col = _log_sigmoid(f_col)
    logf_row = _log_sigmoid(f_row)

    row = lax.broadcasted_iota(I32, (L, L), 0)
    col = lax.broadcasted_iota(I32, (L, L), 1)
    causal = col <= row
    b_col = jnp.sum(jnp.where(causal, logf_row, 0.0), axis=1, keepdims=True)
    b_row = jnp.sum(jnp.where(row <= col, logf_col, 0.0), axis=0, keepdims=True)

    m_prev = m_ref[pl.ds(0, 1), pl.ds(0, 1)]
    dmat = jnp.where(causal, b_col - b_row + i_row, -jnp.inf)
    inter = b_col + m_prev
    m_t = jnp.maximum(inter, jnp.max(dmat, axis=1, keepdims=True))
    w_inter = jnp.exp(inter - m_t)
    scores = lax.dot_general(qb, kb, (((1,), (1,)), ((), ())), preferred_element_type=F32)
    wmat = jnp.exp(dmat - m_t) * scores
    c_old = c_ref[...]
    num = (w_inter * jnp.dot(qb, c_old.astype(BF16), preferred_element_type=F32)
           + jnp.dot(wmat.astype(BF16), vb, preferred_element_type=F32))
    den = w_inter * jnp.sum(q * n_ref[...], axis=1, keepdims=True) + jnp.sum(wmat, axis=1, keepdims=True)
    hval = num / jnp.maximum(jnp.abs(den), jnp.exp(-m_t))

    g_end = jnp.sum(logf_row, axis=1, keepdims=True)
    wl = g_end - b_col + i_col
    m_new = jnp.maximum(g_end + m_prev, jnp.max(wl, axis=0, keepdims=True))
    decay = jnp.exp(g_end + m_prev - m_new)
    wk = jnp.exp(wl - m_new) * k
    c_ref[...] = decay * c_old + lax.dot_general(wk.astype(BF16), vb, (((0,), (0,)), ((), ())),
                                                 preferred_element_type=F32)
    n_ref[...] = decay * n_ref[...] + jnp.sum(wk, axis=0, keepdims=True)
    m_ref[...] = jnp.broadcast_to(m_new, m_ref.shape)

    ms = jnp.mean(hval * hval, axis=-1, keepdims=True)
    hn = hval * lax.rsqrt(ms + EPS) * hg_ref[...]
    out_ref[...] = (hn * _sigmoid(o_ref[...])).astype(out_ref.dtype)


def _mlstm(proj, gates_col, gates_row, conv_w, conv_b, head_g, nb, seq):
    n = proj.shape[0]
    nh = MLSTM_HEADS
    dm = head_g.shape[0]
    dh = dm // nh
    L = _tile(seq, MLSTM_TILE)
    nc = seq // L
    rb = L // SUBLANES

    def cur(off):
        return pl.BlockSpec((L, dh), lambda b, h, c: (b * nc + c, off * nh + h))

    def prev(off):
        return pl.BlockSpec((SUBLANES, dh),
                            lambda b, h, c: (jnp.maximum((b * nc + c) * rb - 1, 0), off * nh + h))

    return pl.pallas_call(
        _mlstm_kernel,
        grid=(nb, nh, nc),
        in_specs=[
            cur(0), prev(0), cur(1), prev(1), cur(2), cur(3),
            pl.BlockSpec((CONV_K, dh), lambda b, h, c: (0, h)),
            pl.BlockSpec((CONV_K, dh), lambda b, h, c: (0, nh + h)),
            pl.BlockSpec((1, dh), lambda b, h, c: (0, h)),
            pl.BlockSpec((1, dh), lambda b, h, c: (0, nh + h)),
            pl.BlockSpec((1, dh), lambda b, h, c: (0, h)),
            pl.BlockSpec((L, LANES), lambda b, h, c: (b * nc + c, 0)),
            pl.BlockSpec((SUBLANES, L), lambda b, h, c: (0, b * nc + c)),
        ],
        out_specs=pl.BlockSpec((L, dh), lambda b, h, c: (b * nc + c, h)),
        out_shape=jax.ShapeDtypeStruct((n, dm), BF16),
        scratch_shapes=[
            pltpu.VMEM((L + SUBLANES, dh), F32),
            pltpu.VMEM((dh, dh), F32),
            pltpu.VMEM((1, dh), F32),
            pltpu.VMEM((SUBLANES, LANES), F32),
        ],
        compiler_params=_params("parallel", "parallel", "arbitrary"),
        name="mlstm",
    )(proj, proj, proj, proj, proj, proj, conv_w, conv_w, conv_b.reshape(1, -1), conv_b.reshape(1, -1),
      head_g.reshape(1, dm), gates_col, gates_row)


def _pool_kernel(x_ref, xp_ref, pw_ref, ps_ref, o_ref, buf_ref, *, tiles_per_seq):
    ts, dp = x_ref.shape
    dg = dp // len(POOL_WINDOWS)
    t_in_seq = pl.program_id(0) % tiles_per_seq
    buf_ref[pl.ds(0, POOL_HALO), :] = jnp.where(t_in_seq == 0, 0.0, xp_ref[...])
    buf_ref[pl.ds(POOL_HALO, ts), :] = x_ref[...]
    pos = t_in_seq * ts + lax.broadcasted_iota(I32, (ts, 1), 0)
    for g, w in enumerate(POOL_WINDOWS):
        cols = pl.ds(g * dg, dg)
        xg = buf_ref[pl.ds(POOL_HALO, ts), cols]
        s = xg
        for j in range(1, w):
            s = s + buf_ref[pl.ds(POOL_HALO - j, ts), cols]
        cnt = jnp.minimum(pos + 1, w).astype(F32)
        pooled = s / cnt - xg
        mixed = jnp.dot(pooled.astype(BF16), pw_ref[g].astype(BF16), preferred_element_type=F32)
        o_ref[:, cols] = (mixed * ps_ref[:, cols]).astype(o_ref.dtype)


def _pool(proj, col_block, pool_w, pool_scale, seq):
    n = proj.shape[0]
    dp = pool_scale.shape[0]
    ts = _tile(seq, 512)
    hb = ts // POOL_HALO
    return pl.pallas_call(
        functools.partial(_pool_kernel, tiles_per_seq=seq // ts),
        grid=(n // ts,),
        in_specs=[
            pl.BlockSpec((ts, dp), lambda i: (i, col_block)),
            pl.BlockSpec((POOL_HALO, dp), lambda i: (jnp.maximum(i * hb - 1, 0), col_block)),
            pl.BlockSpec(pool_w.shape, lambda i: (0, 0, 0)),
            pl.BlockSpec((1, dp), lambda i: (0, 0)),
        ],
        out_specs=pl.BlockSpec((ts, dp), lambda i: (i, 0)),
        out_shape=jax.ShapeDtypeStruct((n, dp), BF16),
        scratch_shapes=[pltpu.VMEM((ts + POOL_HALO, dp), F32)],
        compiler_params=_params("parallel"),
        name="pool",
    )(proj, proj, pool_w, pool_scale.reshape(1, dp))


def _mm_res_kernel(*refs, n_a):
    a_refs, w_refs = refs[:n_a], refs[n_a:2 * n_a]
    x_ref, gate_ref, o_ref = refs[2 * n_a:]
    acc = jnp.dot(a_refs[0][...], w_refs[0][...].astype(BF16), preferred_element_type=F32)
    for a_ref, w_ref in zip(a_refs[1:], w_refs[1:]):
        acc = acc + jnp.dot(a_ref[...], w_ref[...].astype(BF16), preferred_element_type=F32)
    o_ref[...] = x_ref[...] + gate_ref[...] * acc


def _mm_res(a_list, w, x, gate, seq):
    n, d = x.shape
    n_a = len(a_list)
    ka = a_list[0].shape[1]
    assert all(a.shape[1] == ka for a in a_list) and w.shape[0] == n_a * ka
    tm = _tile(seq, 1024)
    tn = _tile(d, 512)
    per_b = seq // tm
    nb = gate.shape[0]
    in_specs = [pl.BlockSpec((tm, ka), lambda i, j: (i, 0)) for _ in a_list]
    in_specs += [pl.BlockSpec((ka, tn), functools.partial(lambda i, j, r: (r, j), r=r)) for r in range(n_a)]
    in_specs += [pl.BlockSpec((tm, tn), lambda i, j: (i, j)),
                 pl.BlockSpec((None, 1, tn), lambda i, j: (i // per_b, 0, j))]
    return pl.pallas_call(
        functools.partial(_mm_res_kernel, n_a=n_a),
        grid=(n // tm, d // tn),
        in_specs=in_specs,
        out_specs=pl.BlockSpec((tm, tn), lambda i, j: (i, j)),
        out_shape=jax.ShapeDtypeStruct((n, d), F32),
        compiler_params=_params("parallel", "parallel"),
        name="matmul_residual",
    )(*a_list, *([w] * n_a), x, gate.reshape(nb, 1, d))


def _ffn_kernel(x_ref, g_ref, sh_ref, sc_ref, gate_ref, w1_ref, w3_ref, w2_ref, o_ref, h_ref):
    f = pl.program_id(1)

    @pl.when(f == 0)
    def _():
        h_ref[...] = _norm_mod(x_ref[...], g_ref[...], sh_ref[...], sc_ref[...]).astype(BF16)
        o_ref[...] = jnp.zeros_like(o_ref)

    h = h_ref[...]
    a = jnp.dot(h, w1_ref[...].astype(BF16), preferred_element_type=F32)
    b = jnp.dot(h, w3_ref[...].astype(BF16), preferred_element_type=F32)
    act = (a * _sigmoid(a) * b).astype(BF16)
    o_ref[...] += jnp.dot(act, w2_ref[...].astype(BF16), preferred_element_type=F32)

    @pl.when(f == pl.num_programs(1) - 1)
    def _():
        o_ref[...] = x_ref[...] + gate_ref[...] * o_ref[...]


def _ffn(x, g, shift, scale, gate, w1, w3, w2, seq):
    n, d = x.shape
    dff = w1.shape[1]
    tm = _tile(seq, 512)
    tf = _tile(dff, 256)
    per_b = seq // tm
    nb = gate.shape[0]
    vec = lambda a: a.reshape(nb, 1, d)
    bvec = pl.BlockSpec((None, 1, d), lambda i, f: (i // per_b, 0, 0))
    return pl.pallas_call(
        _ffn_kernel,
        grid=(n // tm, dff // tf),
        in_specs=[
            pl.BlockSpec((tm, d), lambda i, f: (i, 0)),
            pl.BlockSpec((1, d), lambda i, f: (0, 0)),
            bvec, bvec, bvec,
            pl.BlockSpec((d, tf), lambda i, f: (0, f)),
            pl.BlockSpec((d, tf), lambda i, f: (0, f)),
            pl.BlockSpec((tf, d), lambda i, f: (f, 0)),
        ],
        out_specs=pl.BlockSpec((tm, d), lambda i, f: (i, 0)),
        out_shape=jax.ShapeDtypeStruct((n, d), F32),
        scratch_shapes=[pltpu.VMEM((tm, d), BF16)],
        compiler_params=_params("parallel", "arbitrary"),
        name="swiglu_ffn",
    )(x, g.reshape(1, d), vec(shift), vec(scale), vec(gate), w1, w3, w2)


def _sb_attn_kernel(q_ref, k_ref, v_ref, o_ref):
    qi = pl.program_id(2)
    T, dh = q_ref.shape
    scale = dh ** -0.5
    q = q_ref[...]
    row = lax.broadcasted_iota(I32, (T, T), 0)
    col = lax.broadcasted_iota(I32, (T, T), 1)
    later = (row > col).astype(BF16)

    def tile(j, diagonal, carry):
        acc, rem = carry
        start = pl.multiple_of(j * T, T)
        kj = k_ref[pl.ds(start, T), :]
        vj = v_ref[pl.ds(start, T), :]
        z = lax.dot_general(q, kj, (((1,), (1,)), ((), ())), preferred_element_type=F32) * scale
        sp = jnp.maximum(z, 0.0) + jnp.log1p(jnp.exp(-jnp.abs(z)))
        lom = -sp
        if diagonal:
            strict = col < row
            lom = jnp.where(strict, lom, 0.0)
        hi = lom.astype(BF16)
        lo = (lom - hi.astype(F32)).astype(BF16)
        suffix = (jnp.dot(hi, later, preferred_element_type=F32)
                  + jnp.dot(lo, later, preferred_element_type=F32))
        a = jnp.exp(z - sp + suffix + rem)
        if diagonal:
            a = jnp.where(strict, a, 0.0)
        acc = acc + jnp.dot(a.astype(BF16), vj, preferred_element_type=F32)
        rem = rem + jnp.sum(lom, axis=1, keepdims=True)
        return acc, rem

    carry = tile(qi, True, (jnp.zeros((T, dh), F32), jnp.zeros((T, 1), F32)))
    acc, _ = lax.fori_loop(0, qi, lambda s, c: tile(qi - 1 - s, False, c), carry)
    o_ref[...] = acc.astype(o_ref.dtype)


def _sb_attention(qkv, nb, seq):
    n, d3 = qkv.shape
    d = d3 // 3
    nh = SB_HEADS
    dh = d // nh
    T = _tile(seq, ATTN_TILE)
    nq = seq // T
    return pl.pallas_call(
        _sb_attn_kernel,
        grid=(nb, nh, nq),
        in_specs=[
            pl.BlockSpec((T, dh), lambda b, h, i: (b * nq + i, h)),
            pl.BlockSpec((seq, dh), lambda b, h, i: (b, nh + h)),
            pl.BlockSpec((seq, dh), lambda b, h, i: (b, 2 * nh + h)),
        ],
        out_specs=pl.BlockSpec((T, dh), lambda b, h, i: (b * nq + i, h)),
        out_shape=jax.ShapeDtypeStruct((n, d), BF16),
        compiler_params=_params("parallel", "parallel", "parallel"),
        name="stick_breaking_attention",
    )(qkv, qkv, qkv)


def _router_kernel(x_ref, g_ref, sh_ref, sc_ref, wr_ref, h_ref, info_ref, gate_ref, cnt_ref, carry_ref):
    tm = x_ref.shape[0]

    @pl.when(pl.program_id(0) == 0)
    def _():
        carry_ref[...] = jnp.zeros_like(carry_ref)

    h = _norm_mod(x_ref[...], g_ref[...], sh_ref[...], sc_ref[...])
    h_ref[...] = h
    logits = jnp.dot(h, wr_ref[...], precision=HIGHEST, preferred_element_type=F32)
    lane = lax.broadcasted_iota(I32, (tm, LANES), 1)
    lane_f = lane.astype(F32)
    lg = jnp.where(lane < N_EXPERTS, logits, -jnp.inf)
    v0 = jnp.max(lg, axis=1, keepdims=True)
    i0 = jnp.min(jnp.where(lg == v0, lane_f, float(LANES)), axis=1, keepdims=True)
    lg1 = jnp.where(lane_f == i0, -jnp.inf, lg)
    v1 = jnp.max(lg1, axis=1, keepdims=True)
    i1 = jnp.min(jnp.where(lg1 == v1, lane_f, float(LANES)), axis=1, keepdims=True)
    ex = jnp.exp(v1 - v0)
    g0 = 1.0 / (1.0 + ex)
    g1 = ex / (1.0 + ex)
    sel0 = lane_f == i0
    sel1 = lane_f == i1
    onehot = jnp.where(sel0 | sel1, 1.0, 0.0)
    row = lax.broadcasted_iota(I32, (tm, tm), 0)
    col = lax.broadcasted_iota(I32, (tm, tm), 1)
    before = (col < row).astype(BF16)
    earlier = jnp.dot(before, onehot.astype(BF16), preferred_element_type=F32) + carry_ref[...]
    r0 = jnp.sum(jnp.where(sel0, earlier, 0.0), axis=1, keepdims=True)
    r1 = jnp.sum(jnp.where(sel1, earlier, 0.0), axis=1, keepdims=True)
    carry_ref[...] += jnp.sum(onehot, axis=0, keepdims=True)
    info = jnp.where(lane == 0, i0, jnp.where(lane == 1, i1, jnp.where(lane == 2, r0, jnp.where(lane == 3, r1, 0.0))))
    info_ref[...] = info.astype(I32)
    gate_ref[...] = jnp.where(lane == 0, g0, jnp.where(lane == 1, g1, 0.0))
    cnt_ref[...] = carry_ref[...].astype(I32)


def _router(x, g, shift, scale, w_router, seq):
    n, d = x.shape
    tm = _tile(seq, 512)
    per_b = seq // tm
    nb = shift.shape[0]
    vec = lambda a: a.reshape(nb, 1, d)
    bvec = pl.BlockSpec((None, 1, d), lambda i: (i // per_b, 0, 0))
    wr = jnp.pad(w_router, ((0, 0), (0, LANES - w_router.shape[1])))
    return pl.pallas_call(
        _router_kernel,
        grid=(n // tm,),
        in_specs=[
            pl.BlockSpec((tm, d), lambda i: (i, 0)),
            pl.BlockSpec((1, d), lambda i: (0, 0)),
            bvec, bvec,
            pl.BlockSpec((d, LANES), lambda i: (0, 0)),
        ],
        out_specs=[
            pl.BlockSpec((tm, d), lambda i: (i, 0)),
            pl.BlockSpec((tm, LANES), lambda i: (i, 0)),
            pl.BlockSpec((tm, LANES), lambda i: (i, 0)),
            pl.BlockSpec((1, LANES), lambda i: (0, 0)),
        ],
        out_shape=[
            jax.ShapeDtypeStruct((n, d), F32),
            jax.ShapeDtypeStruct((n, LANES), I32),
            jax.ShapeDtypeStruct((n, LANES), F32),
            jax.ShapeDtypeStruct((1, LANES), I32),
        ],
        scratch_shapes=[pltpu.VMEM((1, LANES), F32)],
        compiler_params=_params("arbitrary"),
        name="router",
    )(x, g.reshape(1, d), vec(shift), vec(scale), wr)


def _moe_kernel(rid_ref, vt_ref, vg_ref, vv_ref, off_ref, h_hbm, gs_ref, w1_ref, w3_ref, w2_ref, y_ref,
                hb_ref, acc_ref, wb1_ref, wb3_ref, wb2_ref, sem):
    v = pl.program_id(0)
    f = pl.program_id(1)
    tm = hb_ref.shape[0]
    tile = vt_ref[v]
    grp = vg_ref[v]
    valid = vv_ref[v] == 1
    new_tile = jnp.logical_or(v == 0, vt_ref[jnp.maximum(v - 1, 0)] != tile)

    def row_copy(r):
        src = rid_ref[tile * tm + r]
        return pltpu.make_async_copy(h_hbm.at[pl.ds(src, 1), :], acc_ref.at[pl.ds(r, 1), :], sem)

    @pl.when(jnp.logical_and(f == 0, new_tile))
    def _():
        def issue(r, c):
            row_copy(r).start()
            return c
        lax.fori_loop(0, tm, issue, 0)

        def drain(r, c):
            row_copy(r).wait()
            return c
        lax.fori_loop(0, tm, drain, 0)
        hb_ref[...] = acc_ref[...].astype(BF16)

    @pl.when(f == 0)
    def _():
        acc_ref[...] = jnp.zeros_like(acc_ref)

    lo = off_ref[grp] - tile * tm
    hi = off_ref[grp + 1] - tile * tm

    @pl.when(valid)
    def _():
        wb1_ref[...] = w1_ref[...].astype(BF16)
        wb3_ref[...] = w3_ref[...].astype(BF16)
        wb2_ref[...] = w2_ref[...].astype(BF16)
        sub = _tile(tm, MOE_SUB)
        for sb in range(tm // sub):
            @pl.when(jnp.logical_and(lo < (sb + 1) * sub, hi > sb * sub))
            def _():
                rows = pl.ds(sb * sub, sub)
                hs = hb_ref[rows, :]
                a = jnp.dot(hs, wb1_ref[...], preferred_element_type=F32)
                b = jnp.dot(hs, wb3_ref[...], preferred_element_type=F32)
                act = (a * _sigmoid(a) * b).astype(BF16)
                acc_ref[rows, :] += jnp.dot(act, wb2_ref[...], preferred_element_type=F32)

    @pl.when(jnp.logical_and(valid, f == pl.num_programs(1) - 1))
    def _():
        r = lax.broadcasted_iota(I32, (tm, 1), 0)
        mine = jnp.logical_and(r >= lo, r < hi)
        val = acc_ref[...] * gs_ref[...]

        @pl.when(new_tile)
        def _():
            y_ref[...] = jnp.where(mine, val, 0.0)

        @pl.when(jnp.logical_not(new_tile))
        def _():
            y_ref[...] = jnp.where(mine, val, y_ref[...])


def _moe(h, row_ids, gate_sorted, visits, offsets, w1, w3, w2, tm):
    _, d = h.shape
    r_total = row_ids.shape[0]
    ne, _, dff = w1.shape
    v_tile, v_group, v_valid = visits
    nv = v_tile.shape[0]
    assert nv == r_total // tm + ne - 1
    tf = _tile(dff, 256)
    nf = dff // tf

    def f_eff(v, f, vv):
        return jnp.where(vv[v] == 1, f, nf - 1)

    return pl.pallas_call(
        _moe_kernel,
        grid_spec=pltpu.PrefetchScalarGridSpec(
            num_scalar_prefetch=5,
            grid=(nv, nf),
            in_specs=[
                pl.BlockSpec(memory_space=pl.ANY),
                pl.BlockSpec((tm, 1), lambda v, f, rid, vt, vg, vv, off: (vt[v], 0)),
                pl.BlockSpec((None, d, tf), lambda v, f, rid, vt, vg, vv, off: (vg[v], 0, f_eff(v, f, vv))),
                pl.BlockSpec((None, d, tf), lambda v, f, rid, vt, vg, vv, off: (vg[v], 0, f_eff(v, f, vv))),
                pl.BlockSpec((None, tf, d), lambda v, f, rid, vt, vg, vv, off: (vg[v], f_eff(v, f, vv), 0)),
            ],
            out_specs=pl.BlockSpec((tm, d), lambda v, f, rid, vt, vg, vv, off: (vt[v], 0)),
            scratch_shapes=[
                pltpu.VMEM((tm, d), BF16),
                pltpu.VMEM((tm, d), F32),
                pltpu.VMEM((d, tf), BF16),
                pltpu.VMEM((d, tf), BF16),
                pltpu.VMEM((tf, d), BF16),
                pltpu.SemaphoreType.DMA(()),
            ],
        ),
        out_shape=jax.ShapeDtypeStruct((r_total, d), F32),
        compiler_params=_params("arbitrary", "arbitrary"),
        name="moe_experts",
    )(row_ids, v_tile, v_group, v_valid, offsets, h, gate_sorted, w1, w3, w2)


def _route_plan(info, gates, counts, tm):
    n = info.shape[0]
    ne = N_EXPERTS
    e0, e1, r0, r1 = info[:, 0], info[:, 1], info[:, 2], info[:, 3]
    cnt = counts[0, :ne]
    ends = jnp.cumsum(cnt)
    offsets = jnp.concatenate([jnp.zeros((1,), I32), ends]).astype(I32)
    pos0 = offsets[e0] + r0
    pos1 = offsets[e1] + r1
    tok = jnp.arange(n, dtype=I32)
    row_ids = jnp.zeros((TOP_K * n,), I32).at[pos0].set(tok).at[pos1].set(tok)
    gate_sorted = jnp.zeros((TOP_K * n,), F32).at[pos0].set(gates[:, 0]).at[pos1].set(gates[:, 1])
    nt = TOP_K * n // tm
    nv = nt + ne - 1
    first = offsets[:-1] // tm
    last = jnp.maximum(ends - 1, 0) // tm
    ntiles = jnp.where(cnt > 0, last - first + 1, 0)
    v_end = jnp.cumsum(ntiles)
    v_start = v_end - ntiles
    total = v_end[-1]
    vid = jnp.arange(nv, dtype=I32)
    valid = vid < total
    vc = jnp.minimum(vid, total - 1)
    grp = jnp.searchsorted(v_end, vc, side="right").astype(I32)
    tile = (first[grp] + vc - v_start[grp]).astype(I32)
    return (pos0, pos1, row_ids, gate_sorted.reshape(-1, 1),
            (tile, grp, valid.astype(I32)), offsets)


def _combine_kernel(p0_ref, p1_ref, y_hbm, x_ref, gate_ref, gf_ref, o_ref, b0_ref, b1_ref, sem):
    i = pl.program_id(0)
    tc = x_ref.shape[0]

    def copies(r):
        t = i * tc + r
        return (pltpu.make_async_copy(y_hbm.at[pl.ds(p0_ref[t], 1), :], b0_ref.at[pl.ds(r, 1), :], sem.at[0]),
                pltpu.make_async_copy(y_hbm.at[pl.ds(p1_ref[t], 1), :], b1_ref.at[pl.ds(r, 1), :], sem.at[1]))

    def issue(r, c):
        for cp in copies(r):
            cp.start()
        return c
    lax.fori_loop(0, tc, issue, 0)

    def drain(r, c):
        for cp in copies(r):
            cp.wait()
        return c
    lax.fori_loop(0, tc, drain, 0)

    xn = x_ref[...] + gate_ref[...] * (b0_ref[...] + b1_ref[...])
    ms = jnp.mean(xn * xn, axis=-1, keepdims=True)
    o_ref[...] = xn * lax.rsqrt(ms + EPS) * gf_ref[...]


def _combine(y, pos0, pos1, x, gate, g_final, seq):
    n, d = x.shape
    tc = _tile(seq, 256)
    per_b = seq // tc
    nb = gate.shape[0]
    return pl.pallas_call(
        _combine_kernel,
        grid_spec=pltpu.PrefetchScalarGridSpec(
            num_scalar_prefetch=2,
            grid=(n // tc,),
            in_specs=[
                pl.BlockSpec(memory_space=pl.ANY),
                pl.BlockSpec((tc, d), lambda i, p0, p1: (i, 0)),
                pl.BlockSpec((None, 1, d), lambda i, p0, p1: (i // per_b, 0, 0)),
                pl.BlockSpec((1, d), lambda i, p0, p1: (0, 0)),
            ],
            out_specs=pl.BlockSpec((tc, d), lambda i, p0, p1: (i, 0)),
            scratch_shapes=[
                pltpu.VMEM((tc, d), F32),
                pltpu.VMEM((tc, d), F32),
                pltpu.SemaphoreType.DMA((2,)),
            ],
        ),
        out_shape=jax.ShapeDtypeStruct((n, d), F32),
        compiler_params=_params("arbitrary"),
        name="moe_combine_final_norm",
    )(pos0, pos1, y, x, gate.reshape(nb, 1, d), g_final.reshape(1, d))


def _even_layer(x, mod, seq, g_mix, g_ffn, w_in, b_gates, conv_w, conv_b, head_g, pool_w, pool_scale,
                w_out, w1, w3, w2):
    n, d = x.shape
    nb = mod.shape[0]
    sh1, sc1, g1, sh2, sc2, g2 = jnp.split(mod, 6, axis=-1)
    dm = head_g.shape[0]
    dp = pool_scale.shape[0]
    ng = 2 * MLSTM_HEADS
    assert dm == dp and w_in.shape[1] == 4 * dm + ng + dp
    w_main = jnp.concatenate([w_in[:, :4 * dm], w_in[:, 4 * dm + ng:]], axis=1)
    w_gate = jnp.pad(w_in[:, 4 * dm:4 * dm + ng], ((0, 0), (0, LANES - ng)))
    b_gate = jnp.pad(b_gates, (0, LANES - ng)).reshape(1, LANES)
    proj, gates = _nm_matmul(x, g_mix, sh1, sc1, w_main, F32, seq, side=(w_gate, b_gate))
    gates_row = gates[:, :SUBLANES].T
    hm = _mlstm(proj, gates, gates_row, conv_w, conv_b, head_g, nb, seq)
    hp = _pool(proj, 4 * dm // dp, pool_w, pool_scale, seq)
    x = _mm_res([hm, hp], w_out, x, g1, seq)
    return _ffn(x, g_ffn, sh2, sc2, g2, w1, w3, w2, seq)


def _odd_layer(x, mod, seq, g_mix, g_ffn, g_final, w_qkv, w_o, w_router, w1, w3, w2):
    n, d = x.shape
    nb = mod.shape[0]
    sh1, sc1, g1, sh2, sc2, g2 = jnp.split(mod, 6, axis=-1)
    qkv = _nm_matmul(x, g_mix, sh1, sc1, w_qkv, BF16, seq)
    att = _sb_attention(qkv, nb, seq)
    x = _mm_res([att], w_o, x, g1, seq)
    h, info, gates, counts = _router(x, g_ffn, sh2, sc2, w_router, seq)
    tm = _tile(TOP_K * n, 1024)
    pos0, pos1, row_ids, gate_sorted, visits, offsets = _route_plan(info, gates, counts, tm)
    y = _moe(h, row_ids, gate_sorted, visits, offsets, w1, w3, w2, tm)
    return _combine(y, pos0, pos1, x, g2, g_final, seq)


def kernel(x, c, ada_w, ada_b, norm_mix_g, norm_ffn_g, norm_final_g, ev_w_in, ev_b_gates, ev_conv_w,
           ev_conv_b, ev_head_g, ev_pool_w, ev_pool_scale, ev_w_out, ev_ffn_w1, ev_ffn_w3, ev_ffn_w2,
           od_w_qkv, od_w_o, od_router, od_moe_w1, od_moe_w3, od_moe_w2):
    nb, seq, d = x.shape
    assert ada_w.shape[0] == 2, "one even and one odd layer"
    mod = _adaln(c, ada_w, ada_b)
    xf = x.reshape(nb * seq, d)
    xf = _even_layer(xf, mod[0], seq, norm_mix_g[0], norm_ffn_g[0], ev_w_in[0], ev_b_gates[0],
                     ev_conv_w[0], ev_conv_b[0], ev_head_g[0], ev_pool_w[0], ev_pool_scale[0],
                     ev_w_out[0], ev_ffn_w1[0], ev_ffn_w3[0], ev_ffn_w2[0])
    out = _odd_layer(xf, mod[1], seq, norm_mix_g[1], norm_ffn_g[1], norm_final_g, od_w_qkv[0], od_w_o[0],
                     od_router[0], od_moe_w1[0], od_moe_w3[0], od_moe_w2[0])
    return out.reshape(nb, seq, d)
```

```python
import functools

import jax
import jax.numpy as jnp
from jax import lax
from jax.experimental import pallas as pl
from jax.experimental.pallas import tpu as pltpu

F32 = jnp.float32
BF16 = jnp.bfloat16
I32 = jnp.int32

EPS = 1e-6
MLSTM_HEADS = 4
CONV_K = 4
POOL_WINDOWS = (2, 4, 8, 16)
SB_HEADS = 16
N_EXPERTS = 8
TOP_K = 2

LANES = 128
SUBLANES = 8
VMEM_LIMIT = 56 * 1024 * 1024
MLSTM_TILE = 256
POOL_HALO = 16
ATTN_TILE = 256
ATTN_HEADS_PER_STEP = 2
ATTN_LOG_CUTOFF = -120.0
MOE_SUB = 256
MOE_TILE_SUBS = 9
DMA_UNROLL = 8

HIGHEST = lax.Precision.HIGHEST


def _params(*sem):
    return pltpu.CompilerParams(dimension_semantics=sem, vmem_limit_bytes=VMEM_LIMIT)


def _tile(n, pref):
    t = min(n, pref)
    assert n % t == 0, (n, pref)
    return t


def _sigmoid(x):
    return 1.0 / (1.0 + jnp.exp(-x))


def _log_sigmoid(x):
    return jnp.minimum(x, 0.0) - jnp.log1p(jnp.exp(-jnp.abs(x)))


def _norm_mod(x, g, shift, scale):
    ms = jnp.mean(x * x, axis=-1, keepdims=True)
    y = x * lax.rsqrt(ms + EPS) * g
    return y * (1.0 + scale) + shift


def _adaln_kernel(cb_ref, w_ref, b_ref, o_ref):
    nb = cb_ref.shape[0]
    for j in range(w_ref.shape[1] // LANES):
        cols = pl.ds(j * LANES, LANES)
        w = w_ref[:, cols]
        for b in range(nb):
            o_ref[pl.ds(b, 1), cols] = jnp.sum(w * cb_ref[b], axis=0, keepdims=True) + b_ref[:, cols]


def _adaln(c, ada_w, ada_b):
    depth, d, n6 = ada_w.shape
    nb = c.shape[0]
    tn = _tile(n6, 1024)
    c_act = c * _sigmoid(c)
    cb = jnp.broadcast_to(c_act[:, :, None], (nb, d, LANES))
    return pl.pallas_call(
        _adaln_kernel,
        grid=(depth, n6 // tn),
        in_specs=[
            pl.BlockSpec((nb, d, LANES), lambda l, j: (0, 0, 0)),
            pl.BlockSpec((None, d, tn), lambda l, j: (l, 0, j)),
            pl.BlockSpec((None, 1, tn), lambda l, j: (l, 0, j)),
        ],
        out_specs=pl.BlockSpec((None, nb, tn), lambda l, j: (l, 0, j)),
        out_shape=jax.ShapeDtypeStruct((depth, nb, n6), F32),
        compiler_params=_params("parallel", "parallel"),
        name="adaln",
    )(cb, ada_w, ada_b.reshape(depth, 1, n6))


def _nm_matmul_kernel(x_ref, g_ref, sh_ref, sc_ref, w_ref, *rest, with_side):
    if with_side:
        ws_ref, bs_ref, o_ref, side_ref, h_ref = rest
    else:
        o_ref, h_ref = rest

    @pl.when(pl.program_id(1) == 0)
    def _():
        h = _norm_mod(x_ref[...], g_ref[...], sh_ref[...], sc_ref[...])
        h_ref[...] = h.astype(BF16)
        if with_side:
            side_ref[...] = jnp.dot(h, ws_ref[...], precision=HIGHEST,
                                    preferred_element_type=F32) + bs_ref[...]

    o_ref[...] = jnp.dot(h_ref[...], w_ref[...].astype(BF16),
                         preferred_element_type=F32).astype(o_ref.dtype)


def _nm_matmul(x, g, shift, scale, w, out_dtype, seq, side=None):
    n, d = x.shape
    nout = w.shape[1]
    tm = _tile(seq, 1024)
    tn = _tile(nout, 512)
    per_b = seq // tm
    nb = shift.shape[0]
    vec = lambda a: a.reshape(nb, 1, d)
    in_specs = [
        pl.BlockSpec((tm, d), lambda i, j: (i, 0)),
        pl.BlockSpec((1, d), lambda i, j: (0, 0)),
        pl.BlockSpec((None, 1, d), lambda i, j: (i // per_b, 0, 0)),
        pl.BlockSpec((None, 1, d), lambda i, j: (i // per_b, 0, 0)),
        pl.BlockSpec((d, tn), lambda i, j: (0, j)),
    ]
    args = [x, g.reshape(1, d), vec(shift), vec(scale), w]
    out_specs = pl.BlockSpec((tm, tn), lambda i, j: (i, j))
    out_shape = jax.ShapeDtypeStruct((n, nout), out_dtype)
    if side is not None:
        in_specs += [pl.BlockSpec((d, LANES), lambda i, j: (0, 0)),
                     pl.BlockSpec((1, LANES), lambda i, j: (0, 0))]
        args += list(side)
        out_specs = [out_specs, pl.BlockSpec((tm, LANES), lambda i, j: (i, 0))]
        out_shape = [out_shape, jax.ShapeDtypeStruct((n, LANES), F32)]
    return pl.pallas_call(
        functools.partial(_nm_matmul_kernel, with_side=side is not None),
        grid=(n // tm, nout // tn),
        in_specs=in_specs,
        out_specs=out_specs,
        out_shape=out_shape,
        scratch_shapes=[pltpu.VMEM((tm, d), BF16)],
        compiler_params=_params("parallel", "arbitrary"),
        name="norm_mod_matmul",
    )(*args)


def _mlstm_kernel(q_ref, qp_ref, k_ref, kp_ref, v_ref, o_ref, cwq_ref, cwk_ref, cbq_ref, cbk_ref,
                  hg_ref, gc_ref, gr_ref, out_ref, ext_ref, c_ref, n_ref, m_ref):
    head = pl.program_id(1)
    chunk = pl.program_id(2)
    nh = pl.num_programs(1)
    L, dh = q_ref.shape

    @pl.when(chunk == 0)
    def _():
        c_ref[...] = jnp.zeros_like(c_ref)
        n_ref[...] = jnp.zeros_like(n_ref)
        m_ref[...] = jnp.zeros_like(m_ref)

    def conv_silu(cur_ref, prev_ref, w_ref, b_ref):
        ext_ref[pl.ds(0, SUBLANES), :] = jnp.where(chunk == 0, 0.0, prev_ref[...])
        ext_ref[pl.ds(SUBLANES, L), :] = cur_ref[...]
        y = b_ref[...] + w_ref[pl.ds(0, 1), :] * ext_ref[pl.ds(SUBLANES - CONV_K + 1, L), :]
        for j in range(1, CONV_K):
            y = y + w_ref[pl.ds(j, 1), :] * ext_ref[pl.ds(SUBLANES - CONV_K + 1 + j, L), :]
        return y * _sigmoid(y)

    q = conv_silu(q_ref, qp_ref, cwq_ref, cbq_ref) * (dh ** -0.5)
    k = conv_silu(k_ref, kp_ref, cwk_ref, cbk_ref)
    qb = q.astype(BF16)
    kb = k.astype(BF16)
    vb = v_ref[...].astype(BF16)

    lane = lax.broadcasted_iota(I32, (L, LANES), 1)
    gc = gc_ref[...]
    i_col = jnp.sum(jnp.where(lane == head, gc, 0.0), axis=1, keepdims=True)
    f_col = jnp.sum(jnp.where(lane == nh + head, gc, 0.0), axis=1, keepdims=True)
    i_row = gr_ref[pl.ds(head, 1), :]
    f_row = gr_ref[pl.ds(nh + head, 1), :]
    logf_col = _log_sigmoid(f_col)
    logf_row = _log_sigmoid(f_row)

    row = lax.broadcasted_iota(I32, (L, L), 0)
    col = lax.broadcasted_iota(I32, (L, L), 1)
    causal = col <= row
    b_col = jnp.sum(jnp.where(causal, logf_row, 0.0), axis=1, keepdims=True)
    b_row = jnp.sum(jnp.where(row <= col, logf_col, 0.0), axis=0, keepdims=True)

    m_prev = m_ref[pl.ds(0, 1), pl.ds(0, 1)]
    dmat = jnp.where(causal, b_col - b_row + i_row, -jnp.inf)
    inter = b_col + m_prev
    m_t = jnp.maximum(inter, jnp.max(dmat, axis=1, keepdims=True))
    w_inter = jnp.exp(inter - m_t)
    scores = lax.dot_general(qb, kb, (((1,), (1,)), ((), ())), preferred_element_type=F32)
    wmat = jnp.exp(dmat - m_t) * scores
    c_old = c_ref[...]
    num = (w_inter * jnp.dot(qb, c_old.astype(BF16), preferred_element_type=F32)
           + jnp.dot(wmat.astype(BF16), vb, preferred_element_type=F32))
    den = w_inter * jnp.sum(q * n_ref[...], axis=1, keepdims=True) + jnp.sum(wmat, axis=1, keepdims=True)
    hval = num / jnp.maximum(jnp.abs(den), jnp.exp(-m_t))

    g_end = jnp.sum(logf_row, axis=1, keepdims=True)
    wl = g_end - b_col + i_col
    m_new = jnp.maximum(g_end + m_prev, jnp.max(wl, axis=0, keepdims=True))
    decay = jnp.exp(g_end + m_prev - m_new)
    wk = jnp.exp(wl - m_new) * k
    c_ref[...] = decay * c_old + lax.dot_general(wk.astype(BF16), vb, (((0,), (0,)), ((), ())),
                                                 preferred_element_type=F32)
    n_ref[...] = decay * n_ref[...] + jnp.sum(wk, axis=0, keepdims=True)
    m_ref[...] = jnp.broadcast_to(m_new, m_ref.shape)

    ms = jnp.mean(hval * hval, axis=-1, keepdims=True)
    hn = hval * lax.rsqrt(ms + EPS) * hg_ref[...]
    out_ref[...] = (hn * _sigmoid(o_ref[...])).astype(out_ref.dtype)


def _mlstm(proj, gates_col, gates_row, conv_w, conv_b, head_g, nb, seq):
    n = proj.shape[0]
    nh = MLSTM_HEADS
    dm = head_g.shape[0]
    dh = dm // nh
    L = _tile(seq, MLSTM_TILE)
    nc = seq // L
    rb = L // SUBLANES

    def cur(off):
        return pl.BlockSpec((L, dh), lambda b, h, c: (b * nc + c, off * nh + h))

    def prev(off):
        return pl.BlockSpec((SUBLANES, dh),
                            lambda b, h, c: (jnp.maximum((b * nc + c) * rb - 1, 0), off * nh + h))

    return pl.pallas_call(
        _mlstm_kernel,
        grid=(nb, nh, nc),
        in_specs=[
            cur(0), prev(0), cur(1), prev(1), cur(2), cur(3),
            pl.BlockSpec((CONV_K, dh), lambda b, h, c: (0, h)),
            pl.BlockSpec((CONV_K, dh), lambda b, h, c: (0, nh + h)),
            pl.BlockSpec((1, dh), lambda b, h, c: (0, h)),
            pl.BlockSpec((1, dh), lambda b, h, c: (0, nh + h)),
            pl.BlockSpec((1, dh), lambda b, h, c: (0, h)),
            pl.BlockSpec((L, LANES), lambda b, h, c: (b * nc + c, 0)),
            pl.BlockSpec((SUBLANES, L), lambda b, h, c: (0, b * nc + c)),
        ],
        out_specs=pl.BlockSpec((L, dh), lambda b, h, c: (b * nc + c, h)),
        out_shape=jax.ShapeDtypeStruct((n, dm), BF16),
        scratch_shapes=[
            pltpu.VMEM((L + SUBLANES, dh), F32),
            pltpu.VMEM((dh, dh), F32),
            pltpu.VMEM((1, dh), F32),
            pltpu.VMEM((SUBLANES, LANES), F32),
        ],
        compiler_params=_params("parallel", "parallel", "arbitrary"),
        name="mlstm",
    )(proj, proj, proj, proj, proj, proj, conv_w, conv_w, conv_b.reshape(1, -1), conv_b.reshape(1, -1),
      head_g.reshape(1, dm), gates_col, gates_row)


def _pool_kernel(x_ref, xp_ref, pw_ref, ps_ref, o_ref, buf_ref, *, tiles_per_seq):
    ts, dp = x_ref.shape
    dg = dp // len(POOL_WINDOWS)
    t_in_seq = pl.program_id(0) % tiles_per_seq
    buf_ref[pl.ds(0, POOL_HALO), :] = jnp.where(t_in_seq == 0, 0.0, xp_ref[...])
    buf_ref[pl.ds(POOL_HALO, ts), :] = x_ref[...]
    pos = t_in_seq * ts + lax.broadcasted_iota(I32, (ts, 1), 0)
    for g, w in enumerate(POOL_WINDOWS):
        cols = pl.ds(g * dg, dg)
        xg = buf_ref[pl.ds(POOL_HALO, ts), cols]
        s = xg
        for j in range(1, w):
            s = s + buf_ref[pl.ds(POOL_HALO - j, ts), cols]
        cnt = jnp.minimum(pos + 1, w).astype(F32)
        pooled = s / cnt - xg
        mixed = jnp.dot(pooled.astype(BF16), pw_ref[g].astype(BF16), preferred_element_type=F32)
        o_ref[:, cols] = (mixed * ps_ref[:, cols]).astype(o_ref.dtype)


def _pool(proj, col_block, pool_w, pool_scale, seq):
    n = proj.shape[0]
    dp = pool_scale.shape[0]
    ts = _tile(seq, 512)
    hb = ts // POOL_HALO
    return pl.pallas_call(
        functools.partial(_pool_kernel, tiles_per_seq=seq // ts),
        grid=(n // ts,),
        in_specs=[
            pl.BlockSpec((ts, dp), lambda i: (i, col_block)),
            pl.BlockSpec((POOL_HALO, dp), lambda i: (jnp.maximum(i * hb - 1, 0), col_block)),
            pl.BlockSpec(pool_w.shape, lambda i: (0, 0, 0)),
            pl.BlockSpec((1, dp), lambda i: (0, 0)),
        ],
        out_specs=pl.BlockSpec((ts, dp), lambda i: (i, 0)),
        out_shape=jax.ShapeDtypeStruct((n, dp), BF16),
        scratch_shapes=[pltpu.VMEM((ts + POOL_HALO, dp), F32)],
        compiler_params=_params("parallel"),
        name="pool",
    )(proj, proj, pool_w, pool_scale.reshape(1, dp))


def _mm_res_kernel(*refs, n_a):
    a_refs, w_refs = refs[:n_a], refs[n_a:2 * n_a]
    x_ref, gate_ref, o_ref = refs[2 * n_a:]
    acc = jnp.dot(a_refs[0][...], w_refs[0][...].astype(BF16), preferred_element_type=F32)
    for a_ref, w_ref in zip(a_refs[1:], w_refs[1:]):
        acc = acc + jnp.dot(a_ref[...], w_ref[...].astype(BF16), preferred_element_type=F32)
    o_ref[...] = x_ref[...] + gate_ref[...] * acc


def _mm_res(a_list, w, x, gate, seq):
    n, d = x.shape
    n_a = len(a_list)
    ka = a_list[0].shape[1]
    assert all(a.shape[1] == ka for a in a_list) and w.shape[0] == n_a * ka
    tm = _tile(seq, 1024)
    tn = _tile(d, 512)
    per_b = seq // tm
    nb = gate.shape[0]
    in_specs = [pl.BlockSpec((tm, ka), lambda i, j: (i, 0)) for _ in a_list]
    in_specs += [pl.BlockSpec((ka, tn), functools.partial(lambda i, j, r: (r, j), r=r)) for r in range(n_a)]
    in_specs += [pl.BlockSpec((tm, tn), lambda i, j: (i, j)),
                 pl.BlockSpec((None, 1, tn), lambda i, j: (i // per_b, 0, j))]
    return pl.pallas_call(
        functools.partial(_mm_res_kernel, n_a=n_a),
        grid=(n // tm, d // tn),
        in_specs=in_specs,
        out_specs=pl.BlockSpec((tm, tn), lambda i, j: (i, j)),
        out_shape=jax.ShapeDtypeStruct((n, d), F32),
        compiler_params=_params("parallel", "parallel"),
        name="matmul_residual",
    )(*a_list, *([w] * n_a), x, gate.reshape(nb, 1, d))


def _ffn_kernel(x_ref, g_ref, sh_ref, sc_ref, gate_ref, w1_ref, w3_ref, w2_ref, o_ref, h_ref):
    f = pl.program_id(1)

    @pl.when(f == 0)
    def _():
        h_ref[...] = _norm_mod(x_ref[...], g_ref[...], sh_ref[...], sc_ref[...]).astype(BF16)
        o_ref[...] = jnp.zeros_like(o_ref)

    h = h_ref[...]
    a = jnp.dot(h, w1_ref[...].astype(BF16), preferred_element_type=F32)
    b = jnp.dot(h, w3_ref[...].astype(BF16), preferred_element_type=F32)
    act = (a * _sigmoid(a) * b).astype(BF16)
    o_ref[...] += jnp.dot(act, w2_ref[...].astype(BF16), preferred_element_type=F32)

    @pl.when(f == pl.num_programs(1) - 1)
    def _():
        o_ref[...] = x_ref[...] + gate_ref[...] * o_ref[...]


def _ffn(x, g, shift, scale, gate, w1, w3, w2, seq):
    n, d = x.shape
    dff = w1.shape[1]
    tm = _tile(seq, 1024)
    tf = _tile(dff, 256)
    per_b = seq // tm
    nb = gate.shape[0]
    vec = lambda a: a.reshape(nb, 1, d)
    bvec = pl.BlockSpec((None, 1, d), lambda i, f: (i // per_b, 0, 0))
    return pl.pallas_call(
        _ffn_kernel,
        grid=(n // tm, dff // tf),
        in_specs=[
            pl.BlockSpec((tm, d), lambda i, f: (i, 0), pipeline_mode=pl.Buffered(1)),
            pl.BlockSpec((1, d), lambda i, f: (0, 0)),
            bvec, bvec, bvec,
            pl.BlockSpec((d, tf), lambda i, f: (0, f)),
            pl.BlockSpec((d, tf), lambda i, f: (0, f)),
            pl.BlockSpec((tf, d), lambda i, f: (f, 0)),
        ],
        out_specs=pl.BlockSpec((tm, d), lambda i, f: (i, 0)),
        out_shape=jax.ShapeDtypeStruct((n, d), F32),
        scratch_shapes=[pltpu.VMEM((tm, d), BF16)],
        compiler_params=_params("parallel", "arbitrary"),
        name="swiglu_ffn",
    )(x, g.reshape(1, d), vec(shift), vec(scale), vec(gate), w1, w3, w2)


def _sb_attn_kernel(q_ref, k_ref, v_ref, o_ref):
    qi = pl.program_id(2)
    T = q_ref.shape[0]
    dh = q_ref.shape[1] // ATTN_HEADS_PER_STEP
    scale = dh ** -0.5
    row = lax.broadcasted_iota(I32, (T, T), 0)
    col = lax.broadcasted_iota(I32, (T, T), 1)
    later = (row > col).astype(BF16)

    def tile(j, diagonal, accs, rems):
        start = pl.multiple_of(j * T, T)
        new_accs, new_rems = [], []
        for g in range(ATTN_HEADS_PER_STEP):
            cols = pl.ds(g * dh, dh)
            q = q_ref[:, cols]
            kj = k_ref[pl.ds(start, T), cols]
            vj = v_ref[pl.ds(start, T), cols]
            z = lax.dot_general(q, kj, (((1,), (1,)), ((), ())), preferred_element_type=F32) * scale
            sp = jnp.maximum(z, 0.0) + jnp.log1p(jnp.exp(-jnp.abs(z)))
            lom = -sp
            if diagonal:
                strict = col < row
                lom = jnp.where(strict, lom, 0.0)
            hi = lom.astype(BF16)
            lo = (lom - hi.astype(F32)).astype(BF16)
            suffix = (jnp.dot(hi, later, preferred_element_type=F32)
                      + jnp.dot(lo, later, preferred_element_type=F32))
            a = jnp.exp(z - sp + suffix + rems[g])
            if diagonal:
                a = jnp.where(strict, a, 0.0)
            new_accs.append(accs[g] + jnp.dot(a.astype(BF16), vj, preferred_element_type=F32))
            new_rems.append(rems[g] + jnp.sum(lom, axis=1, keepdims=True))
        return tuple(new_accs), tuple(new_rems)

    def alive(rems):
        top = rems[0]
        for r in rems[1:]:
            top = jnp.maximum(top, r)
        return jnp.max(top) > ATTN_LOG_CUTOFF

    zeros = lambda w: tuple(jnp.zeros((T, w), F32) for _ in range(ATTN_HEADS_PER_STEP))
    accs, rems = tile(qi, True, zeros(dh), zeros(1))

    def cond(c):
        s, live, _, _ = c
        return jnp.logical_and(s < qi, live)

    def body(c):
        s, _, accs, rems = c
        accs, rems = tile(qi - 1 - s, False, accs, rems)
        return s + 1, alive(rems), accs, rems

    _, _, accs, _ = lax.while_loop(cond, body, (jnp.int32(0), alive(rems), accs, rems))
    for g in range(ATTN_HEADS_PER_STEP):
        o_ref[:, pl.ds(g * dh, dh)] = accs[g].astype(o_ref.dtype)


def _sb_attention(qkv, nb, seq):
    n, d3 = qkv.shape
    d = d3 // 3
    nh = SB_HEADS
    dh = d // nh
    T = _tile(seq, ATTN_TILE)
    nq = seq // T
    hg = ATTN_HEADS_PER_STEP
    ng = nh // hg
    return pl.pallas_call(
        _sb_attn_kernel,
        grid=(nb, ng, nq),
        in_specs=[
            pl.BlockSpec((T, hg * dh), lambda b, h, i: (b * nq + i, h)),
            pl.BlockSpec((seq, hg * dh), lambda b, h, i: (b, ng + h)),
            pl.BlockSpec((seq, hg * dh), lambda b, h, i: (b, 2 * ng + h)),
        ],
        out_specs=pl.BlockSpec((T, hg * dh), lambda b, h, i: (b * nq + i, h)),
        out_shape=jax.ShapeDtypeStruct((n, d), BF16),
        compiler_params=_params("parallel", "parallel", "parallel"),
        name="stick_breaking_attention",
    )(qkv, qkv, qkv)


def _router_kernel(x_ref, g_ref, sh_ref, sc_ref, wr_ref, h_ref, info_ref, gate_ref, cnt_ref, carry_ref):
    tm = x_ref.shape[0]

    @pl.when(pl.program_id(0) == 0)
    def _():
        carry_ref[...] = jnp.zeros_like(carry_ref)

    h = _norm_mod(x_ref[...], g_ref[...], sh_ref[...], sc_ref[...])
    h_ref[...] = h
    logits = jnp.dot(h, wr_ref[...], precision=HIGHEST, preferred_element_type=F32)
    lane = lax.broadcasted_iota(I32, (tm, LANES), 1)
    lane_f = lane.astype(F32)
    lg = jnp.where(lane < N_EXPERTS, logits, -jnp.inf)
    v0 = jnp.max(lg, axis=1, keepdims=True)
    i0 = jnp.min(jnp.where(lg == v0, lane_f, float(LANES)), axis=1, keepdims=True)
    lg1 = jnp.where(lane_f == i0, -jnp.inf, lg)
    v1 = jnp.max(lg1, axis=1, keepdims=True)
    i1 = jnp.min(jnp.where(lg1 == v1, lane_f, float(LANES)), axis=1, keepdims=True)
    ex = jnp.exp(v1 - v0)
    g0 = 1.0 / (1.0 + ex)
    g1 = ex / (1.0 + ex)
    sel0 = lane_f == i0
    sel1 = lane_f == i1
    onehot = jnp.where(sel0 | sel1, 1.0, 0.0)
    row = lax.broadcasted_iota(I32, (tm, tm), 0)
    col = lax.broadcasted_iota(I32, (tm, tm), 1)
    before = (col < row).astype(BF16)
    earlier = jnp.dot(before, onehot.astype(BF16), preferred_element_type=F32) + carry_ref[...]
    r0 = jnp.sum(jnp.where(sel0, earlier, 0.0), axis=1, keepdims=True)
    r1 = jnp.sum(jnp.where(sel1, earlier, 0.0), axis=1, keepdims=True)
    carry_ref[...] += jnp.sum(onehot, axis=0, keepdims=True)
    info = jnp.where(lane == 0, i0, jnp.where(lane == 1, i1, jnp.where(lane == 2, r0, jnp.where(lane == 3, r1, 0.0))))
    info_ref[...] = info.astype(I32)
    gate_ref[...] = jnp.where(lane == 0, g0, jnp.where(lane == 1, g1, 0.0))
    cnt_ref[...] = carry_ref[...].astype(I32)


def _router(x, g, shift, scale, w_router, seq):
    n, d = x.shape
    tm = _tile(seq, 512)
    per_b = seq // tm
    nb = shift.shape[0]
    vec = lambda a: a.reshape(nb, 1, d)
    bvec = pl.BlockSpec((None, 1, d), lambda i: (i // per_b, 0, 0))
    wr = jnp.pad(w_router, ((0, 0), (0, LANES - w_router.shape[1])))
    return pl.pallas_call(
        _router_kernel,
        grid=(n // tm,),
        in_specs=[
            pl.BlockSpec((tm, d), lambda i: (i, 0)),
            pl.BlockSpec((1, d), lambda i: (0, 0)),
            bvec, bvec,
            pl.BlockSpec((d, LANES), lambda i: (0, 0)),
        ],
        out_specs=[
            pl.BlockSpec((tm, d), lambda i: (i, 0)),
            pl.BlockSpec((tm, LANES), lambda i: (i, 0)),
            pl.BlockSpec((tm, LANES), lambda i: (i, 0)),
            pl.BlockSpec((1, LANES), lambda i: (0, 0)),
        ],
        out_shape=[
            jax.ShapeDtypeStruct((n, d), F32),
            jax.ShapeDtypeStruct((n, LANES), I32),
            jax.ShapeDtypeStruct((n, LANES), F32),
            jax.ShapeDtypeStruct((1, LANES), I32),
        ],
        scratch_shapes=[pltpu.VMEM((1, LANES), F32)],
        compiler_params=_params("arbitrary"),
        name="router",
    )(x, g.reshape(1, d), vec(shift), vec(scale), wr)


def _route_plan(info, counts, tm):
    n = info.shape[0]
    ne = N_EXPERTS
    e0, e1, r0, r1 = info[:, 0], info[:, 1], info[:, 2], info[:, 3]
    cnt = counts[0, :ne]
    padded = (cnt + MOE_SUB - 1) // MOE_SUB * MOE_SUB
    ends = jnp.cumsum(padded)
    starts = ends - padded
    pos0 = (starts[e0] + r0).astype(I32)
    pos1 = (starts[e1] + r1).astype(I32)
    r_pad = TOP_K * n + ne * MOE_SUB
    nv = -(-r_pad // tm) + ne
    ntiles = (padded + tm - 1) // tm
    v_end = jnp.cumsum(ntiles)
    v_first = v_end - ntiles
    total = v_end[-1]
    vid = jnp.arange(nv, dtype=I32)
    vc = jnp.minimum(vid, total - 1)
    grp = jnp.sum((v_end[None, :] <= vc[:, None]).astype(I32), axis=1)
    k = vc - v_first[grp]
    v_start = (starts[grp] + k * tm).astype(I32)
    v_nsub = jnp.where(vid < total, jnp.minimum(tm, padded[grp] - k * tm) // MOE_SUB, 0).astype(I32)
    return pos0, pos1, (grp.astype(I32), v_start, v_nsub), r_pad


def _dispatch_kernel(p0_ref, p1_ref, h_ref, init_hbm, hs_hbm, sem):
    del init_hbm
    i = pl.program_id(0)
    td = h_ref.shape[0]

    def copies(r):
        t = i * td + r
        src = h_ref.at[pl.ds(r, 1), :]
        return (pltpu.make_async_copy(src, hs_hbm.at[pl.ds(p0_ref[t], 1), :], sem.at[0]),
                pltpu.make_async_copy(src, hs_hbm.at[pl.ds(p1_ref[t], 1), :], sem.at[1]))

    def issue(r, c):
        for cp in copies(r):
            cp.start()
        return c
    lax.fori_loop(0, td, issue, 0, unroll=DMA_UNROLL)

    def drain(r, c):
        for cp in copies(r):
            cp.wait()
        return c
    lax.fori_loop(0, td, drain, 0, unroll=DMA_UNROLL)


def _dispatch(h, pos0, pos1, r_pad, seq):
    n, d = h.shape
    td = _tile(seq, 512)
    return pl.pallas_call(
        _dispatch_kernel,
        grid_spec=pltpu.PrefetchScalarGridSpec(
            num_scalar_prefetch=2,
            grid=(n // td,),
            in_specs=[
                pl.BlockSpec((td, d), lambda i, p0, p1: (i, 0)),
                pl.BlockSpec(memory_space=pl.ANY),
            ],
            out_specs=pl.BlockSpec(memory_space=pl.ANY),
            scratch_shapes=[pltpu.SemaphoreType.DMA((2,))],
        ),
        out_shape=jax.ShapeDtypeStruct((r_pad, d), F32),
        input_output_aliases={3: 0},
        compiler_params=_params("arbitrary"),
        name="moe_dispatch",
    )(pos0, pos1, h, jnp.zeros((r_pad, d), F32))


def _moe_kernel(vg_ref, vs_ref, vn_ref, hs_hbm, w1_ref, w3_ref, w2_ref, y_hbm,
                stage_ref, hb_ref, acc_ref, wb1_ref, wb3_ref, wb2_ref, sem_in, sem_out):
    v = pl.program_id(0)
    f = pl.program_id(1)
    sub = stage_ref.shape[1]
    start = vs_ref[v]
    nsub = vn_ref[v]

    def rows(sb):
        return pl.ds(pl.multiple_of(sb * sub, sub), sub)

    def hbm_rows(row0, sb):
        return pl.ds(pl.multiple_of(row0 + sb * sub, sub), sub)

    def in_copy(sb, slot):
        return pltpu.make_async_copy(hs_hbm.at[hbm_rows(start, sb), :], stage_ref.at[slot], sem_in.at[slot])

    def out_copy(row0, sb):
        return pltpu.make_async_copy(acc_ref.at[rows(sb), :], y_hbm.at[hbm_rows(row0, sb), :], sem_out)

    def each_block(count, fn):
        def body(sb, c):
            fn(sb)
            return c
        lax.fori_loop(0, count, body, 0)

    @pl.when(f == 0)
    def _():
        @pl.when(nsub > 0)
        def _():
            in_copy(0, 0).start()

        def load(sb):
            slot = sb % 2

            @pl.when(sb + 1 < nsub)
            def _():
                in_copy(sb + 1, 1 - slot).start()
            in_copy(sb, slot).wait()
            hb_ref[rows(sb), :] = stage_ref[slot].astype(BF16)
        each_block(nsub, load)

        prev = jnp.maximum(v - 1, 0)
        n_prev = jnp.where(v > 0, vn_ref[prev], 0)
        each_block(n_prev, lambda sb: out_copy(vs_ref[prev], sb).wait())

        def clear(sb):
            acc_ref[rows(sb), :] = jnp.zeros((sub, acc_ref.shape[1]), F32)
        each_block(nsub, clear)

    @pl.when(nsub > 0)
    def _():
        wb1_ref[...] = w1_ref[...].astype(BF16)
        wb3_ref[...] = w3_ref[...].astype(BF16)
        wb2_ref[...] = w2_ref[...].astype(BF16)

        def block(sb):
            hs = hb_ref[rows(sb), :]
            a = jnp.dot(hs, wb1_ref[...], preferred_element_type=F32)
            b = jnp.dot(hs, wb3_ref[...], preferred_element_type=F32)
            act = (a * _sigmoid(a) * b).astype(BF16)
            acc_ref[rows(sb), :] += jnp.dot(act, wb2_ref[...], preferred_element_type=F32)
        each_block(nsub, block)

    @pl.when(f == pl.num_programs(1) - 1)
    def _():
        each_block(nsub, lambda sb: out_copy(start, sb).start())

        @pl.when(v == pl.num_programs(0) - 1)
        def _():
            each_block(nsub, lambda sb: out_copy(start, sb).wait())


def _moe(hs, visits, w1, w3, w2, tm):
    r_pad, d = hs.shape
    _, _, dff = w1.shape
    v_group, v_start, v_nsub = visits
    nv = v_group.shape[0]
    tf = _tile(dff, 256)
    nf = dff // tf

    def f_eff(v, f, vn):
        return jnp.where(vn[v] > 0, f, nf - 1)

    return pl.pallas_call(
        _moe_kernel,
        grid_spec=pltpu.PrefetchScalarGridSpec(
            num_scalar_prefetch=3,
            grid=(nv, nf),
            in_specs=[
                pl.BlockSpec(memory_space=pl.ANY),
                pl.BlockSpec((None, d, tf), lambda v, f, vg, vs, vn: (vg[v], 0, f_eff(v, f, vn))),
                pl.BlockSpec((None, d, tf), lambda v, f, vg, vs, vn: (vg[v], 0, f_eff(v, f, vn))),
                pl.BlockSpec((None, tf, d), lambda v, f, vg, vs, vn: (vg[v], f_eff(v, f, vn), 0)),
            ],
            out_specs=pl.BlockSpec(memory_space=pl.ANY),
            scratch_shapes=[
                pltpu.VMEM((2, MOE_SUB, d), F32),
                pltpu.VMEM((tm, d), BF16),
                pltpu.VMEM((tm, d), F32),
                pltpu.VMEM((d, tf), BF16),
                pltpu.VMEM((d, tf), BF16),
                pltpu.VMEM((tf, d), BF16),
                pltpu.SemaphoreType.DMA((2,)),
                pltpu.SemaphoreType.DMA(()),
            ],
        ),
        out_shape=jax.ShapeDtypeStruct((r_pad, d), F32),
        input_output_aliases={3: 0},
        compiler_params=_params("arbitrary", "arbitrary"),
        name="moe_experts",
    )(v_group, v_start, v_nsub, hs, w1, w3, w2)


def _combine_kernel(p0_ref, p1_ref, y_hbm, x_ref, gate_ref, rg_ref, gf_ref, o_ref, b0_ref, b1_ref, sem):
    i = pl.program_id(0)
    tc = x_ref.shape[0]

    def copies(r):
        t = i * tc + r
        return (pltpu.make_async_copy(y_hbm.at[pl.ds(p0_ref[t], 1), :], b0_ref.at[pl.ds(r, 1), :], sem.at[0]),
                pltpu.make_async_copy(y_hbm.at[pl.ds(p1_ref[t], 1), :], b1_ref.at[pl.ds(r, 1), :], sem.at[1]))

    def issue(r, c):
        for cp in copies(r):
            cp.start()
        return c
    lax.fori_loop(0, tc, issue, 0, unroll=DMA_UNROLL)

    def drain(r, c):
        for cp in copies(r):
            cp.wait()
        return c
    lax.fori_loop(0, tc, drain, 0, unroll=DMA_UNROLL)

    rg = rg_ref[...]
    moe = rg[:, 0:1] * b0_ref[...] + rg[:, 1:2] * b1_ref[...]
    xn = x_ref[...] + gate_ref[...] * moe
    ms = jnp.mean(xn * xn, axis=-1, keepdims=True)
    o_ref[...] = xn * lax.rsqrt(ms + EPS) * gf_ref[...]


def _combine(y, pos0, pos1, route_gates, x, gate, g_final, seq):
    n, d = x.shape
    tc = _tile(seq, 256)
    per_b = seq // tc
    nb = gate.shape[0]
    return pl.pallas_call(
        _combine_kernel,
        grid_spec=pltpu.PrefetchScalarGridSpec(
            num_scalar_prefetch=2,
            grid=(n // tc,),
            in_specs=[
                pl.BlockSpec(memory_space=pl.ANY),
                pl.BlockSpec((tc, d), lambda i, p0, p1: (i, 0)),
                pl.BlockSpec((None, 1, d), lambda i, p0, p1: (i // per_b, 0, 0)),
                pl.BlockSpec((tc, LANES), lambda i, p0, p1: (i, 0)),
                pl.BlockSpec((1, d), lambda i, p0, p1: (0, 0)),
            ],
            out_specs=pl.BlockSpec((tc, d), lambda i, p0, p1: (i, 0)),
            scratch_shapes=[
                pltpu.VMEM((tc, d), F32),
                pltpu.VMEM((tc, d), F32),
                pltpu.SemaphoreType.DMA((2,)),
            ],
        ),
        out_shape=jax.ShapeDtypeStruct((n, d), F32),
        compiler_params=_params("arbitrary"),
        name="moe_combine_final_norm",
    )(pos0, pos1, y, x, gate.reshape(nb, 1, d), route_gates, g_final.reshape(1, d))


def _even_layer(x, mod, seq, g_mix, g_ffn, w_in, b_gates, conv_w, conv_b, head_g, pool_w, pool_scale,
                w_out, w1, w3, w2):
    n, d = x.shape
    nb = mod.shape[0]
    sh1, sc1, g1, sh2, sc2, g2 = jnp.split(mod, 6, axis=-1)
    dm = head_g.shape[0]
    dp = pool_scale.shape[0]
    ng = 2 * MLSTM_HEADS
    assert dm == dp and w_in.shape[1] == 4 * dm + ng + dp
    w_main = jnp.concatenate([w_in[:, :4 * dm], w_in[:, 4 * dm + ng:]], axis=1)
    w_gate = jnp.pad(w_in[:, 4 * dm:4 * dm + ng], ((0, 0), (0, LANES - ng)))
    b_gate = jnp.pad(b_gates, (0, LANES - ng)).reshape(1, LANES)
    proj, gates = _nm_matmul(x, g_mix, sh1, sc1, w_main, F32, seq, side=(w_gate, b_gate))
    gates_row = gates[:, :SUBLANES].T
    hm = _mlstm(proj, gates, gates_row, conv_w, conv_b, head_g, nb, seq)
    hp = _pool(proj, 4 * dm // dp, pool_w, pool_scale, seq)
    x = _mm_res([hm, hp], w_out, x, g1, seq)
    return _ffn(x, g_ffn, sh2, sc2, g2, w1, w3, w2, seq)


def _odd_layer(x, mod, seq, g_mix, g_ffn, g_final, w_qkv, w_o, w_router, w1, w3, w2):
    n, d = x.shape
    nb = mod.shape[0]
    sh1, sc1, g1, sh2, sc2, g2 = jnp.split(mod, 6, axis=-1)
    qkv = _nm_matmul(x, g_mix, sh1, sc1, w_qkv, BF16, seq)
    att = _sb_attention(qkv, nb, seq)
    x = _mm_res([att], w_o, x, g1, seq)
    h, info, gates, counts = _router(x, g_ffn, sh2, sc2, w_router, seq)
    tm = MOE_TILE_SUBS * MOE_SUB
    pos0, pos1, visits, r_pad = _route_plan(info, counts, tm)
    hs = _dispatch(h, pos0, pos1, r_pad, seq)
    y = _moe(hs, visits, w1, w3, w2, tm)
    return _combine(y, pos0, pos1, gates, x, g2, g_final, seq)


def kernel(x, c, ada_w, ada_b, norm_mix_g, norm_ffn_g, norm_final_g, ev_w_in, ev_b_gates, ev_conv_w,
           ev_conv_b, ev_head_g, ev_pool_w, ev_pool_scale, ev_w_out, ev_ffn_w1, ev_ffn_w3, ev_ffn_w2,
           od_w_qkv, od_w_o, od_router, od_moe_w1, od_moe_w3, od_moe_w2):
    nb, seq, d = x.shape
    assert ada_w.shape[0] == 2, "one even and one odd layer"
    mod = _adaln(c, ada_w, ada_b)
    xf = x.reshape(nb * seq, d)
    xf = _even_layer(xf, mod[0], seq, norm_mix_g[0], norm_ffn_g[0], ev_w_in[0], ev_b_gates[0],
                     ev_conv_w[0], ev_conv_b[0], ev_head_g[0], ev_pool_w[0], ev_pool_scale[0],
                     ev_w_out[0], ev_ffn_w1[0], ev_ffn_w3[0], ev_ffn_w2[0])
    out = _odd_layer(xf, mod[1], seq, norm_mix_g[1], norm_ffn_g[1], norm_final_g, od_w_qkv[0], od_w_o[0],
                     od_router[0], od_moe_w1[0], od_moe_w3[0], od_moe_w2[0])
    return out.reshape(nb, seq, d)
```

```python
import functools

import jax
import jax.numpy as jnp
from jax import lax
from jax.experimental import pallas as pl
from jax.experimental.pallas import tpu as pltpu

F32 = jnp.float32
BF16 = jnp.bfloat16
I32 = jnp.int32

EPS = 1e-6
MLSTM_HEADS = 4
CONV_K = 4
POOL_WINDOWS = (2, 4, 8, 16)
SB_HEADS = 16
N_EXPERTS = 8
TOP_K = 2

LANES = 128
SUBLANES = 8
VMEM_LIMIT = 56 * 1024 * 1024
MLSTM_TILE = 256
POOL_HALO = 16
ATTN_TILE = 256
ATTN_HEADS_PER_STEP = 2
ATTN_LOG_CUTOFF = -120.0
MOE_SUB = 256
MOE_TILE_SUBS = 9
MOE_LONG = 3
DMA_UNROLL = 8

HIGHEST = lax.Precision.HIGHEST


def _params(*sem):
    return pltpu.CompilerParams(dimension_semantics=sem, vmem_limit_bytes=VMEM_LIMIT)


def _tile(n, pref):
    t = min(n, pref)
    assert n % t == 0, (n, pref)
    return t


def _sigmoid(x):
    return 1.0 / (1.0 + jnp.exp(-x))


def _log_sigmoid(x):
    return jnp.minimum(x, 0.0) - jnp.log1p(jnp.exp(-jnp.abs(x)))


def _norm_mod(x, g, shift, scale):
    ms = jnp.mean(x * x, axis=-1, keepdims=True)
    y = x * lax.rsqrt(ms + EPS) * g
    return y * (1.0 + scale) + shift


def _adaln_kernel(cb_ref, w_ref, b_ref, o_ref):
    nb = cb_ref.shape[0]
    for j in range(w_ref.shape[1] // LANES):
        cols = pl.ds(j * LANES, LANES)
        w = w_ref[:, cols]
        for b in range(nb):
            o_ref[pl.ds(b, 1), cols] = jnp.sum(w * cb_ref[b], axis=0, keepdims=True) + b_ref[:, cols]


def _adaln(c, ada_w, ada_b):
    depth, d, n6 = ada_w.shape
    nb = c.shape[0]
    tn = _tile(n6, 1024)
    c_act = c * _sigmoid(c)
    cb = jnp.broadcast_to(c_act[:, :, None], (nb, d, LANES))
    return pl.pallas_call(
        _adaln_kernel,
        grid=(depth, n6 // tn),
        in_specs=[
            pl.BlockSpec((nb, d, LANES), lambda l, j: (0, 0, 0)),
            pl.BlockSpec((None, d, tn), lambda l, j: (l, 0, j)),
            pl.BlockSpec((None, 1, tn), lambda l, j: (l, 0, j)),
        ],
        out_specs=pl.BlockSpec((None, nb, tn), lambda l, j: (l, 0, j)),
        out_shape=jax.ShapeDtypeStruct((depth, nb, n6), F32),
        compiler_params=_params("parallel", "parallel"),
        name="adaln",
    )(cb, ada_w, ada_b.reshape(depth, 1, n6))


def _nm_matmul_kernel(x_ref, g_ref, sh_ref, sc_ref, w_ref, *rest, with_side, n_head):
    if with_side:
        wt_ref, ws_ref, bs_ref, o_ref, side_ref, h_ref = rest
    else:
        o_ref, h_ref = rest
    j = pl.program_id(1)

    @pl.when(j == 0)
    def _():
        h = _norm_mod(x_ref[...], g_ref[...], sh_ref[...], sc_ref[...])
        h_ref[...] = h.astype(BF16)
        if with_side:
            side_ref[...] = jnp.dot(h, ws_ref[...], precision=HIGHEST,
                                    preferred_element_type=F32) + bs_ref[...]

    def project(weights_ref):
        o_ref[...] = jnp.dot(h_ref[...], weights_ref[...].astype(BF16),
                             preferred_element_type=F32).astype(o_ref.dtype)

    if with_side:
        pl.when(j < n_head)(lambda: project(w_ref))
        pl.when(j >= n_head)(lambda: project(wt_ref))
    else:
        project(w_ref)


def _nm_matmul(x, g, shift, scale, w, out_dtype, seq, head_cols=None, side=None):
    n, d = x.shape
    tm = _tile(seq, 1024)
    per_b = seq // tm
    nb = shift.shape[0]
    vec = lambda a: a.reshape(nb, 1, d)
    if side is None:
        nout = w.shape[1]
        tn = _tile(nout, 512)
        n_head = nout // tn
    else:
        nout = head_cols + side[0].shape[1]
        tn = _tile(head_cols, 512)
        assert side[0].shape[1] % tn == 0
        n_head = head_cols // tn
    in_specs = [
        pl.BlockSpec((tm, d), lambda i, j: (i, 0)),
        pl.BlockSpec((1, d), lambda i, j: (0, 0)),
        pl.BlockSpec((None, 1, d), lambda i, j: (i // per_b, 0, 0)),
        pl.BlockSpec((None, 1, d), lambda i, j: (i // per_b, 0, 0)),
        pl.BlockSpec((d, tn), lambda i, j: (0, jnp.minimum(j, n_head - 1))),
    ]
    args = [x, g.reshape(1, d), vec(shift), vec(scale), w]
    out_specs = pl.BlockSpec((tm, tn), lambda i, j: (i, j))
    out_shape = jax.ShapeDtypeStruct((n, nout), out_dtype)
    if side is not None:
        in_specs += [pl.BlockSpec((d, tn), lambda i, j: (0, jnp.maximum(j - n_head, 0))),
                     pl.BlockSpec((d, LANES), lambda i, j: (0, 0)),
                     pl.BlockSpec((1, LANES), lambda i, j: (0, 0))]
        args += list(side)
        out_specs = [out_specs, pl.BlockSpec((tm, LANES), lambda i, j: (i, 0))]
        out_shape = [out_shape, jax.ShapeDtypeStruct((n, LANES), F32)]
    return pl.pallas_call(
        functools.partial(_nm_matmul_kernel, with_side=side is not None, n_head=n_head),
        grid=(n // tm, nout // tn),
        in_specs=in_specs,
        out_specs=out_specs,
        out_shape=out_shape,
        scratch_shapes=[pltpu.VMEM((tm, d), BF16)],
        compiler_params=_params("parallel", "arbitrary"),
        name="norm_mod_matmul",
    )(*args)


def _mlstm_kernel(q_ref, qp_ref, k_ref, kp_ref, v_ref, o_ref, cw_ref, cb_ref, hg_ref, gc_ref, gr_ref,
                  out_ref, ext_ref, c_ref, n_ref, m_ref):
    chunk = pl.program_id(1)
    nh = c_ref.shape[0]
    L, dm = q_ref.shape
    dh = dm // nh

    @pl.when(chunk == 0)
    def _():
        c_ref[...] = jnp.zeros_like(c_ref)
        n_ref[...] = jnp.zeros_like(n_ref)
        m_ref[...] = jnp.zeros_like(m_ref)

    def conv_silu(cur_ref, prev_ref, off):
        ext_ref[pl.ds(0, SUBLANES), :] = jnp.where(chunk == 0, 0.0, prev_ref[...])
        ext_ref[pl.ds(SUBLANES, L), :] = cur_ref[...]
        y = cb_ref[:, pl.ds(off, dm)]
        for j in range(CONV_K):
            y = y + cw_ref[pl.ds(j, 1), pl.ds(off, dm)] * ext_ref[pl.ds(SUBLANES - CONV_K + 1 + j, L), :]
        return y * _sigmoid(y)

    q_all = conv_silu(q_ref, qp_ref, 0) * (dh ** -0.5)
    k_all = conv_silu(k_ref, kp_ref, dm)
    gc = gc_ref[...]
    row = lax.broadcasted_iota(I32, (L, L), 0)
    col = lax.broadcasted_iota(I32, (L, L), 1)
    causal = col <= row

    for head in range(nh):
        cols = pl.ds(head * dh, dh)
        q = q_all[:, head * dh:(head + 1) * dh]
        k = k_all[:, head * dh:(head + 1) * dh]
        qb = q.astype(BF16)
        kb = k.astype(BF16)
        vb = v_ref[:, cols].astype(BF16)

        i_col = gc[:, head:head + 1]
        f_col = gc[:, nh + head:nh + head + 1]
        i_row = gr_ref[pl.ds(head, 1), :]
        f_row = gr_ref[pl.ds(nh + head, 1), :]
        logf_col = _log_sigmoid(f_col)
        logf_row = _log_sigmoid(f_row)
        b_col = jnp.sum(jnp.where(causal, logf_row, 0.0), axis=1, keepdims=True)
        b_row = jnp.sum(jnp.where(row <= col, logf_col, 0.0), axis=0, keepdims=True)

        m_prev = m_ref[pl.ds(head, 1), pl.ds(0, 1)]
        n_prev = n_ref[pl.ds(head, 1), :]
        dmat = jnp.where(causal, b_col - b_row + i_row, -jnp.inf)
        inter = b_col + m_prev
        m_t = jnp.maximum(inter, jnp.max(dmat, axis=1, keepdims=True))
        w_inter = jnp.exp(inter - m_t)
        scores = lax.dot_general(qb, kb, (((1,), (1,)), ((), ())), preferred_element_type=F32)
        wmat = jnp.exp(dmat - m_t) * scores
        c_old = c_ref[head]
        num = (w_inter * jnp.dot(qb, c_old.astype(BF16), preferred_element_type=F32)
               + jnp.dot(wmat.astype(BF16), vb, preferred_element_type=F32))
        den = w_inter * jnp.sum(q * n_prev, axis=1, keepdims=True) + jnp.sum(wmat, axis=1, keepdims=True)
        hval = num / jnp.maximum(jnp.abs(den), jnp.exp(-m_t))

        g_end = jnp.sum(logf_row, axis=1, keepdims=True)
        wl = g_end - b_col + i_col
        m_new = jnp.maximum(g_end + m_prev, jnp.max(wl, axis=0, keepdims=True))
        decay = jnp.exp(g_end + m_prev - m_new)
        wk = jnp.exp(wl - m_new) * k
        c_ref[head] = decay * c_old + lax.dot_general(wk.astype(BF16), vb, (((0,), (0,)), ((), ())),
                                                      preferred_element_type=F32)
        n_ref[pl.ds(head, 1), :] = decay * n_prev + jnp.sum(wk, axis=0, keepdims=True)
        m_ref[pl.ds(head, 1), :] = jnp.broadcast_to(m_new, (1, m_ref.shape[1]))

        ms = jnp.mean(hval * hval, axis=-1, keepdims=True)
        hn = hval * lax.rsqrt(ms + EPS) * hg_ref[:, cols]
        out_ref[:, cols] = (hn * _sigmoid(o_ref[:, cols])).astype(out_ref.dtype)


def _mlstm(proj, gates_col, gates_row, conv_w, conv_b, head_g, nb, seq):
    n = proj.shape[0]
    nh = MLSTM_HEADS
    dm = head_g.shape[0]
    dh = dm // nh
    L = _tile(seq, MLSTM_TILE)
    nc = seq // L
    rb = L // SUBLANES

    assert 2 * nh <= SUBLANES

    def cur(off):
        return pl.BlockSpec((L, dm), lambda b, c: (b * nc + c, off))

    def prev(off):
        return pl.BlockSpec((SUBLANES, dm), lambda b, c: (jnp.maximum((b * nc + c) * rb - 1, 0), off))

    return pl.pallas_call(
        _mlstm_kernel,
        grid=(nb, nc),
        in_specs=[
            cur(0), prev(0), cur(1), prev(1), cur(2), cur(3),
            pl.BlockSpec((CONV_K, 2 * dm), lambda b, c: (0, 0)),
            pl.BlockSpec((1, 2 * dm), lambda b, c: (0, 0)),
            pl.BlockSpec((1, dm), lambda b, c: (0, 0)),
            pl.BlockSpec((L, LANES), lambda b, c: (b * nc + c, 0)),
            pl.BlockSpec((SUBLANES, L), lambda b, c: (0, b * nc + c)),
        ],
        out_specs=pl.BlockSpec((L, dm), lambda b, c: (b * nc + c, 0)),
        out_shape=jax.ShapeDtypeStruct((n, dm), BF16),
        scratch_shapes=[
            pltpu.VMEM((L + SUBLANES, dm), F32),
            pltpu.VMEM((nh, dh, dh), F32),
            pltpu.VMEM((SUBLANES, dh), F32),
            pltpu.VMEM((SUBLANES, LANES), F32),
        ],
        compiler_params=_params("parallel", "arbitrary"),
        name="mlstm",
    )(proj, proj, proj, proj, proj, proj, conv_w, conv_b.reshape(1, -1), head_g.reshape(1, dm),
      gates_col, gates_row)


def _pool_kernel(x_ref, xp_ref, pw_ref, ps_ref, o_ref, buf_ref, *, tiles_per_seq):
    ts, dp = x_ref.shape
    dg = dp // len(POOL_WINDOWS)
    t_in_seq = pl.program_id(0) % tiles_per_seq
    buf_ref[pl.ds(0, POOL_HALO), :] = jnp.where(t_in_seq == 0, 0.0, xp_ref[...])
    buf_ref[pl.ds(POOL_HALO, ts), :] = x_ref[...]
    pos = t_in_seq * ts + lax.broadcasted_iota(I32, (ts, 1), 0)
    for g, w in enumerate(POOL_WINDOWS):
        cols = pl.ds(g * dg, dg)
        xg = buf_ref[pl.ds(POOL_HALO, ts), cols]
        s = xg
        for j in range(1, w):
            s = s + buf_ref[pl.ds(POOL_HALO - j, ts), cols]
        cnt = jnp.minimum(pos + 1, w).astype(F32)
        pooled = s / cnt - xg
        mixed = jnp.dot(pooled.astype(BF16), pw_ref[g].astype(BF16), preferred_element_type=F32)
        o_ref[:, cols] = (mixed * ps_ref[:, cols]).astype(o_ref.dtype)


def _pool(proj, col_block, pool_w, pool_scale, seq):
    n = proj.shape[0]
    dp = pool_scale.shape[0]
    ts = _tile(seq, 512)
    hb = ts // POOL_HALO
    return pl.pallas_call(
        functools.partial(_pool_kernel, tiles_per_seq=seq // ts),
        grid=(n // ts,),
        in_specs=[
            pl.BlockSpec((ts, dp), lambda i: (i, col_block)),
            pl.BlockSpec((POOL_HALO, dp), lambda i: (jnp.maximum(i * hb - 1, 0), col_block)),
            pl.BlockSpec(pool_w.shape, lambda i: (0, 0, 0)),
            pl.BlockSpec((1, dp), lambda i: (0, 0)),
        ],
        out_specs=pl.BlockSpec((ts, dp), lambda i: (i, 0)),
        out_shape=jax.ShapeDtypeStruct((n, dp), BF16),
        scratch_shapes=[pltpu.VMEM((ts + POOL_HALO, dp), F32)],
        compiler_params=_params("parallel"),
        name="pool",
    )(proj, proj, pool_w, pool_scale.reshape(1, dp))


def _mm_res_kernel(*refs, n_a):
    a_refs, w_refs = refs[:n_a], refs[n_a:2 * n_a]
    x_ref, gate_ref, o_ref = refs[2 * n_a:]
    acc = jnp.dot(a_refs[0][...], w_refs[0][...].astype(BF16), preferred_element_type=F32)
    for a_ref, w_ref in zip(a_refs[1:], w_refs[1:]):
        acc = acc + jnp.dot(a_ref[...], w_ref[...].astype(BF16), preferred_element_type=F32)
    o_ref[...] = x_ref[...] + gate_ref[...] * acc


def _mm_res(a_list, w, x, gate, seq):
    n, d = x.shape
    n_a = len(a_list)
    ka = a_list[0].shape[1]
    assert all(a.shape[1] == ka for a in a_list) and w.shape[0] == n_a * ka
    tm = _tile(seq, 1024)
    tn = _tile(d, 512)
    per_b = seq // tm
    nb = gate.shape[0]
    in_specs = [pl.BlockSpec((tm, ka), lambda i, j: (i, 0)) for _ in a_list]
    in_specs += [pl.BlockSpec((ka, tn), functools.partial(lambda i, j, r: (r, j), r=r)) for r in range(n_a)]
    in_specs += [pl.BlockSpec((tm, tn), lambda i, j: (i, j)),
                 pl.BlockSpec((None, 1, tn), lambda i, j: (i // per_b, 0, j))]
    return pl.pallas_call(
        functools.partial(_mm_res_kernel, n_a=n_a),
        grid=(n // tm, d // tn),
        in_specs=in_specs,
        out_specs=pl.BlockSpec((tm, tn), lambda i, j: (i, j)),
        out_shape=jax.ShapeDtypeStruct((n, d), F32),
        compiler_params=_params("parallel", "parallel"),
        name="matmul_residual",
    )(*a_list, *([w] * n_a), x, gate.reshape(nb, 1, d))


def _ffn_kernel(x_ref, g_ref, sh_ref, sc_ref, gate_ref, w1_ref, w3_ref, w2_ref, o_ref, h_ref):
    f = pl.program_id(1)

    @pl.when(f == 0)
    def _():
        h_ref[...] = _norm_mod(x_ref[...], g_ref[...], sh_ref[...], sc_ref[...]).astype(BF16)
        o_ref[...] = jnp.zeros_like(o_ref)

    h = h_ref[...]
    a = jnp.dot(h, w1_ref[...].astype(BF16), preferred_element_type=F32)
    b = jnp.dot(h, w3_ref[...].astype(BF16), preferred_element_type=F32)
    act = (a * _sigmoid(a) * b).astype(BF16)
    o_ref[...] += jnp.dot(act, w2_ref[...].astype(BF16), preferred_element_type=F32)

    @pl.when(f == pl.num_programs(1) - 1)
    def _():
        o_ref[...] = x_ref[...] + gate_ref[...] * o_ref[...]


def _ffn(x, g, shift, scale, gate, w1, w3, w2, seq):
    n, d = x.shape
    dff = w1.shape[1]
    tm = _tile(seq, 1024)
    tf = _tile(dff, 256)
    per_b = seq // tm
    nb = gate.shape[0]
    vec = lambda a: a.reshape(nb, 1, d)
    bvec = pl.BlockSpec((None, 1, d), lambda i, f: (i // per_b, 0, 0))
    return pl.pallas_call(
        _ffn_kernel,
        grid=(n // tm, dff // tf),
        in_specs=[
            pl.BlockSpec((tm, d), lambda i, f: (i, 0), pipeline_mode=pl.Buffered(1)),
            pl.BlockSpec((1, d), lambda i, f: (0, 0)),
            bvec, bvec, bvec,
            pl.BlockSpec((d, tf), lambda i, f: (0, f)),
            pl.BlockSpec((d, tf), lambda i, f: (0, f)),
            pl.BlockSpec((tf, d), lambda i, f: (f, 0)),
        ],
        out_specs=pl.BlockSpec((tm, d), lambda i, f: (i, 0)),
        out_shape=jax.ShapeDtypeStruct((n, d), F32),
        scratch_shapes=[pltpu.VMEM((tm, d), BF16)],
        compiler_params=_params("parallel", "arbitrary"),
        name="swiglu_ffn",
    )(x, g.reshape(1, d), vec(shift), vec(scale), vec(gate), w1, w3, w2)


def _sb_attn_kernel(q_ref, k_ref, v_ref, o_ref):
    qi = pl.program_id(2)
    T = q_ref.shape[0]
    dh = q_ref.shape[1] // ATTN_HEADS_PER_STEP
    scale = dh ** -0.5
    row = lax.broadcasted_iota(I32, (T, T), 0)
    col = lax.broadcasted_iota(I32, (T, T), 1)
    later = (row > col).astype(BF16)

    def prepare(j, g, diagonal):
        start = pl.multiple_of(j * T, T)
        cols = pl.ds(g * dh, dh)
        kj = k_ref[pl.ds(start, T), cols]
        z = lax.dot_general(q_ref[:, cols], kj, (((1,), (1,)), ((), ())), preferred_element_type=F32) * scale
        sp = jnp.maximum(z, 0.0) + jnp.log(1.0 + jnp.exp(-jnp.abs(z)))
        lom = -sp
        if diagonal:
            strict = col < row
            lom = jnp.where(strict, lom, 0.0)
        hi = lom.astype(BF16)
        lo = (lom - hi.astype(F32)).astype(BF16)
        suffix = (jnp.dot(hi, later, preferred_element_type=F32)
                  + jnp.dot(lo, later, preferred_element_type=F32))
        base = z - sp + suffix
        if diagonal:
            base = jnp.where(strict, base, -jnp.inf)
        return base, jnp.sum(lom, axis=1, keepdims=True), v_ref[pl.ds(start, T), cols]

    def walk(tiles, accs, rems):
        new_accs, new_rems = [], []
        for g in range(ATTN_HEADS_PER_STEP):
            parts = [prepare(j, g, diagonal) for j, diagonal in tiles]
            acc, rem = accs[g], rems[g]
            for base, total, vj in parts:
                a = jnp.exp(base + rem)
                acc = acc + jnp.dot(a.astype(BF16), vj, preferred_element_type=F32)
                rem = rem + total
            new_accs.append(acc)
            new_rems.append(rem)
        return tuple(new_accs), tuple(new_rems)

    def alive(rems):
        top = rems[0]
        for r in rems[1:]:
            top = jnp.maximum(top, r)
        return jnp.max(top) > ATTN_LOG_CUTOFF

    def finish(accs):
        for g in range(ATTN_HEADS_PER_STEP):
            o_ref[:, pl.ds(g * dh, dh)] = accs[g].astype(o_ref.dtype)

    zeros = lambda w: tuple(jnp.zeros((T, w), F32) for _ in range(ATTN_HEADS_PER_STEP))

    @pl.when(qi == 0)
    def _():
        accs, _ = walk([(qi, True)], zeros(dh), zeros(1))
        finish(accs)

    @pl.when(qi > 0)
    def _():
        accs, rems = walk([(qi, True), (qi - 1, False)], zeros(dh), zeros(1))

        def cond(c):
            s, live, _, _ = c
            return jnp.logical_and(s < qi, live)

        def body(c):
            s, _, accs, rems = c
            accs, rems = walk([(qi - 1 - s, False)], accs, rems)
            return s + 1, alive(rems), accs, rems

        _, _, accs, _ = lax.while_loop(cond, body, (jnp.int32(1), alive(rems), accs, rems))
        finish(accs)


def _sb_attention(qkv, nb, seq):
    n, d3 = qkv.shape
    d = d3 // 3
    nh = SB_HEADS
    dh = d // nh
    T = _tile(seq, ATTN_TILE)
    nq = seq // T
    hg = ATTN_HEADS_PER_STEP
    ng = nh // hg
    return pl.pallas_call(
        _sb_attn_kernel,
        grid=(nb, ng, nq),
        in_specs=[
            pl.BlockSpec((T, hg * dh), lambda b, h, i: (b * nq + i, h)),
            pl.BlockSpec((seq, hg * dh), lambda b, h, i: (b, ng + h)),
            pl.BlockSpec((seq, hg * dh), lambda b, h, i: (b, 2 * ng + h)),
        ],
        out_specs=pl.BlockSpec((T, hg * dh), lambda b, h, i: (b * nq + i, h)),
        out_shape=jax.ShapeDtypeStruct((n, d), BF16),
        compiler_params=_params("parallel", "parallel", "parallel"),
        name="stick_breaking_attention",
    )(qkv, qkv, qkv)


def _router_kernel(x_ref, g_ref, sh_ref, sc_ref, wr_ref, h_ref, info_ref, gate_ref, cnt_ref, carry_ref):
    tm = x_ref.shape[0]

    @pl.when(pl.program_id(0) == 0)
    def _():
        carry_ref[...] = jnp.zeros_like(carry_ref)

    h = _norm_mod(x_ref[...], g_ref[...], sh_ref[...], sc_ref[...])
    h_ref[...] = h
    logits = jnp.dot(h, wr_ref[...], precision=HIGHEST, preferred_element_type=F32)
    lane = lax.broadcasted_iota(I32, (tm, LANES), 1)
    lane_f = lane.astype(F32)
    lg = jnp.where(lane < N_EXPERTS, logits, -jnp.inf)
    v0 = jnp.max(lg, axis=1, keepdims=True)
    i0 = jnp.min(jnp.where(lg == v0, lane_f, float(LANES)), axis=1, keepdims=True)
    lg1 = jnp.where(lane_f == i0, -jnp.inf, lg)
    v1 = jnp.max(lg1, axis=1, keepdims=True)
    i1 = jnp.min(jnp.where(lg1 == v1, lane_f, float(LANES)), axis=1, keepdims=True)
    ex = jnp.exp(v1 - v0)
    g0 = 1.0 / (1.0 + ex)
    g1 = ex / (1.0 + ex)
    sel0 = lane_f == i0
    sel1 = lane_f == i1
    onehot = jnp.where(sel0 | sel1, 1.0, 0.0)
    row = lax.broadcasted_iota(I32, (tm, tm), 0)
    col = lax.broadcasted_iota(I32, (tm, tm), 1)
    before = (col < row).astype(BF16)
    earlier = jnp.dot(before, onehot.astype(BF16), preferred_element_type=F32) + carry_ref[...]
    r0 = jnp.sum(jnp.where(sel0, earlier, 0.0), axis=1, keepdims=True)
    r1 = jnp.sum(jnp.where(sel1, earlier, 0.0), axis=1, keepdims=True)
    carry_ref[...] += jnp.sum(onehot, axis=0, keepdims=True)
    info = jnp.where(lane == 0, i0, jnp.where(lane == 1, i1, jnp.where(lane == 2, r0, jnp.where(lane == 3, r1, 0.0))))
    info_ref[...] = info.astype(I32)
    gate_ref[...] = jnp.where(lane == 0, g0, jnp.where(lane == 1, g1, 0.0))
    cnt_ref[...] = carry_ref[...].astype(I32)


def _router(x, g, shift, scale, w_router, seq):
    n, d = x.shape
    tm = _tile(seq, 512)
    per_b = seq // tm
    nb = shift.shape[0]
    vec = lambda a: a.reshape(nb, 1, d)
    bvec = pl.BlockSpec((None, 1, d), lambda i: (i // per_b, 0, 0))
    wr = jnp.pad(w_router, ((0, 0), (0, LANES - w_router.shape[1])))
    return pl.pallas_call(
        _router_kernel,
        grid=(n // tm,),
        in_specs=[
            pl.BlockSpec((tm, d), lambda i: (i, 0)),
            pl.BlockSpec((1, d), lambda i: (0, 0)),
            bvec, bvec,
            pl.BlockSpec((d, LANES), lambda i: (0, 0)),
        ],
        out_specs=[
            pl.BlockSpec((tm, d), lambda i: (i, 0)),
            pl.BlockSpec((tm, LANES), lambda i: (i, 0)),
            pl.BlockSpec((tm, LANES), lambda i: (i, 0)),
            pl.BlockSpec((1, LANES), lambda i: (0, 0)),
        ],
        out_shape=[
            jax.ShapeDtypeStruct((n, d), F32),
            jax.ShapeDtypeStruct((n, LANES), I32),
            jax.ShapeDtypeStruct((n, LANES), F32),
            jax.ShapeDtypeStruct((1, LANES), I32),
        ],
        scratch_shapes=[pltpu.VMEM((1, LANES), F32)],
        compiler_params=_params("arbitrary"),
        name="router",
    )(x, g.reshape(1, d), vec(shift), vec(scale), wr)


def _route_plan(info, counts, tm):
    n = info.shape[0]
    ne = N_EXPERTS
    e0, e1, r0, r1 = info[:, 0], info[:, 1], info[:, 2], info[:, 3]
    cnt = counts[0, :ne]
    padded = (cnt + MOE_SUB - 1) // MOE_SUB * MOE_SUB
    ends = jnp.cumsum(padded)
    starts = ends - padded
    pos0 = (starts[e0] + r0).astype(I32)
    pos1 = (starts[e1] + r1).astype(I32)
    r_pad = TOP_K * n + ne * MOE_SUB
    nv = -(-r_pad // tm) + ne
    ntiles = (padded + tm - 1) // tm
    v_end = jnp.cumsum(ntiles)
    v_first = v_end - ntiles
    total = v_end[-1]
    vid = jnp.arange(nv, dtype=I32)
    vc = jnp.minimum(vid, total - 1)
    grp = jnp.sum((v_end[None, :] <= vc[:, None]).astype(I32), axis=1)
    k = vc - v_first[grp]
    v_start = (starts[grp] + k * tm).astype(I32)
    v_nsub = jnp.where(vid < total, jnp.minimum(tm, padded[grp] - k * tm) // MOE_SUB, 0).astype(I32)
    return pos0, pos1, (grp.astype(I32), v_start, v_nsub), r_pad


def _dispatch_kernel(p0_ref, p1_ref, h_ref, init_hbm, hs_hbm, sem):
    del init_hbm
    i = pl.program_id(0)
    td = h_ref.shape[0]

    def copies(r):
        t = i * td + r
        src = h_ref.at[pl.ds(r, 1), :]
        return (pltpu.make_async_copy(src, hs_hbm.at[pl.ds(p0_ref[t], 1), :], sem.at[0]),
                pltpu.make_async_copy(src, hs_hbm.at[pl.ds(p1_ref[t], 1), :], sem.at[1]))

    def issue(r, c):
        for cp in copies(r):
            cp.start()
        return c
    lax.fori_loop(0, td, issue, 0, unroll=DMA_UNROLL)

    def drain(r, c):
        for cp in copies(r):
            cp.wait()
        return c
    lax.fori_loop(0, td, drain, 0, unroll=DMA_UNROLL)


def _dispatch(h, pos0, pos1, r_pad, seq):
    n, d = h.shape
    td = _tile(seq, 512)
    return pl.pallas_call(
        _dispatch_kernel,
        grid_spec=pltpu.PrefetchScalarGridSpec(
            num_scalar_prefetch=2,
            grid=(n // td,),
            in_specs=[
                pl.BlockSpec((td, d), lambda i, p0, p1: (i, 0)),
                pl.BlockSpec(memory_space=pl.ANY),
            ],
            out_specs=pl.BlockSpec(memory_space=pl.ANY),
            scratch_shapes=[pltpu.SemaphoreType.DMA((2,))],
        ),
        out_shape=jax.ShapeDtypeStruct((r_pad, d), F32),
        input_output_aliases={3: 0},
        compiler_params=_params("arbitrary"),
        name="moe_dispatch",
    )(pos0, pos1, h, jnp.zeros((r_pad, d), F32))


def _moe_kernel(vg_ref, vs_ref, vn_ref, hs_hbm, w1_ref, w3_ref, w2_ref, y_hbm,
                stage_ref, hb_ref, acc_ref, wb1_ref, wb3_ref, wb2_ref, sem_in, sem_out):
    v = pl.program_id(0)
    f = pl.program_id(1)
    sub = stage_ref.shape[1]
    start = vs_ref[v]
    nsub = vn_ref[v]

    def rows(sb):
        return pl.ds(pl.multiple_of(sb * sub, sub), sub)

    def hbm_rows(row0, sb):
        return pl.ds(pl.multiple_of(row0 + sb * sub, sub), sub)

    def in_copy(sb, slot):
        return pltpu.make_async_copy(hs_hbm.at[hbm_rows(start, sb), :], stage_ref.at[slot], sem_in.at[slot])

    def out_copy(row0, sb):
        return pltpu.make_async_copy(acc_ref.at[rows(sb), :], y_hbm.at[hbm_rows(row0, sb), :], sem_out)

    def each_block(count, fn):
        def body(sb, c):
            fn(sb)
            return c
        lax.fori_loop(0, count, body, 0)

    @pl.when(f == 0)
    def _():
        @pl.when(nsub > 0)
        def _():
            in_copy(0, 0).start()

        def load(sb):
            slot = sb % 2

            @pl.when(sb + 1 < nsub)
            def _():
                in_copy(sb + 1, 1 - slot).start()
            in_copy(sb, slot).wait()
            hb_ref[rows(sb), :] = stage_ref[slot].astype(BF16)
        each_block(nsub, load)

        prev = jnp.maximum(v - 1, 0)
        n_prev = jnp.where(v > 0, vn_ref[prev], 0)
        each_block(n_prev, lambda sb: out_copy(vs_ref[prev], sb).wait())

        def clear(sb):
            acc_ref[rows(sb), :] = jnp.zeros((sub, acc_ref.shape[1]), F32)
        each_block(nsub, clear)

    @pl.when(nsub > 0)
    def _():
        wb1_ref[...] = w1_ref[...].astype(BF16)
        wb3_ref[...] = w3_ref[...].astype(BF16)
        wb2_ref[...] = w2_ref[...].astype(BF16)

        def blocks(sb, count):
            r = pl.ds(pl.multiple_of(sb * sub, sub), count * sub)
            hs = hb_ref[r, :]
            a = jnp.dot(hs, wb1_ref[...], preferred_element_type=F32)
            b = jnp.dot(hs, wb3_ref[...], preferred_element_type=F32)
            act = (a * _sigmoid(a) * b).astype(BF16)
            acc_ref[r, :] += jnp.dot(act, wb2_ref[...], preferred_element_type=F32)

        n_long = nsub // MOE_LONG
        each_block(n_long, lambda i: blocks(i * MOE_LONG, MOE_LONG))
        each_block(nsub - n_long * MOE_LONG, lambda i: blocks(n_long * MOE_LONG + i, 1))

    @pl.when(f == pl.num_programs(1) - 1)
    def _():
        each_block(nsub, lambda sb: out_copy(start, sb).start())

        @pl.when(v == pl.num_programs(0) - 1)
        def _():
            each_block(nsub, lambda sb: out_copy(start, sb).wait())


def _moe(hs, visits, w1, w3, w2, tm):
    r_pad, d = hs.shape
    _, _, dff = w1.shape
    v_group, v_start, v_nsub = visits
    nv = v_group.shape[0]
    tf = _tile(dff, 256)
    nf = dff // tf

    def f_eff(v, f, vn):
        return jnp.where(vn[v] > 0, f, nf - 1)

    return pl.pallas_call(
        _moe_kernel,
        grid_spec=pltpu.PrefetchScalarGridSpec(
            num_scalar_prefetch=3,
            grid=(nv, nf),
            in_specs=[
                pl.BlockSpec(memory_space=pl.ANY),
                pl.BlockSpec((None, d, tf), lambda v, f, vg, vs, vn: (vg[v], 0, f_eff(v, f, vn))),
                pl.BlockSpec((None, d, tf), lambda v, f, vg, vs, vn: (vg[v], 0, f_eff(v, f, vn))),
                pl.BlockSpec((None, tf, d), lambda v, f, vg, vs, vn: (vg[v], f_eff(v, f, vn), 0)),
            ],
            out_specs=pl.BlockSpec(memory_space=pl.ANY),
            scratch_shapes=[
                pltpu.VMEM((2, MOE_SUB, d), F32),
                pltpu.VMEM((tm, d), BF16),
                pltpu.VMEM((tm, d), F32),
                pltpu.VMEM((d, tf), BF16),
                pltpu.VMEM((d, tf), BF16),
                pltpu.VMEM((tf, d), BF16),
                pltpu.SemaphoreType.DMA((2,)),
                pltpu.SemaphoreType.DMA(()),
            ],
        ),
        out_shape=jax.ShapeDtypeStruct((r_pad, d), F32),
        input_output_aliases={3: 0},
        compiler_params=_params("arbitrary", "arbitrary"),
        name="moe_experts",
    )(v_group, v_start, v_nsub, hs, w1, w3, w2)


def _combine_kernel(p0_ref, p1_ref, y_hbm, x_ref, gate_ref, rg_ref, gf_ref, o_ref, b0_ref, b1_ref, sem):
    i = pl.program_id(0)
    tc = x_ref.shape[0]

    def copies(r):
        t = i * tc + r
        return (pltpu.make_async_copy(y_hbm.at[pl.ds(p0_ref[t], 1), :], b0_ref.at[pl.ds(r, 1), :], sem.at[0]),
                pltpu.make_async_copy(y_hbm.at[pl.ds(p1_ref[t], 1), :], b1_ref.at[pl.ds(r, 1), :], sem.at[1]))

    def issue(r, c):
        for cp in copies(r):
            cp.start()
        return c
    lax.fori_loop(0, tc, issue, 0, unroll=DMA_UNROLL)

    def drain(r, c):
        for cp in copies(r):
            cp.wait()
        return c
    lax.fori_loop(0, tc, drain, 0, unroll=DMA_UNROLL)

    rg = rg_ref[...]
    moe = rg[:, 0:1] * b0_ref[...] + rg[:, 1:2] * b1_ref[...]
    xn = x_ref[...] + gate_ref[...] * moe
    ms = jnp.mean(xn * xn, axis=-1, keepdims=True)
    o_ref[...] = xn * lax.rsqrt(ms + EPS) * gf_ref[...]


def _combine(y, pos0, pos1, route_gates, x, gate, g_final, seq):
    n, d = x.shape
    tc = _tile(seq, 256)
    per_b = seq // tc
    nb = gate.shape[0]
    return pl.pallas_call(
        _combine_kernel,
        grid_spec=pltpu.PrefetchScalarGridSpec(
            num_scalar_prefetch=2,
            grid=(n // tc,),
            in_specs=[
                pl.BlockSpec(memory_space=pl.ANY),
                pl.BlockSpec((tc, d), lambda i, p0, p1: (i, 0)),
                pl.BlockSpec((None, 1, d), lambda i, p0, p1: (i // per_b, 0, 0)),
                pl.BlockSpec((tc, LANES), lambda i, p0, p1: (i, 0)),
                pl.BlockSpec((1, d), lambda i, p0, p1: (0, 0)),
            ],
            out_specs=pl.BlockSpec((tc, d), lambda i, p0, p1: (i, 0)),
            scratch_shapes=[
                pltpu.VMEM((tc, d), F32),
                pltpu.VMEM((tc, d), F32),
                pltpu.SemaphoreType.DMA((2,)),
            ],
        ),
        out_shape=jax.ShapeDtypeStruct((n, d), F32),
        compiler_params=_params("arbitrary"),
        name="moe_combine_final_norm",
    )(pos0, pos1, y, x, gate.reshape(nb, 1, d), route_gates, g_final.reshape(1, d))


def _even_layer(x, mod, seq, g_mix, g_ffn, w_in, b_gates, conv_w, conv_b, head_g, pool_w, pool_scale,
                w_out, w1, w3, w2):
    n, d = x.shape
    nb = mod.shape[0]
    sh1, sc1, g1, sh2, sc2, g2 = jnp.split(mod, 6, axis=-1)
    dm = head_g.shape[0]
    dp = pool_scale.shape[0]
    ng = 2 * MLSTM_HEADS
    assert dm == dp and w_in.shape[1] == 4 * dm + ng + dp
    w_pool = w_in[:, 4 * dm + ng:]
    w_gate = jnp.pad(w_in[:, 4 * dm:4 * dm + ng], ((0, 0), (0, LANES - ng)))
    b_gate = jnp.pad(b_gates, (0, LANES - ng)).reshape(1, LANES)
    proj, gates = _nm_matmul(x, g_mix, sh1, sc1, w_in, F32, seq, head_cols=4 * dm,
                             side=(w_pool, w_gate, b_gate))
    gates_row = gates[:, :SUBLANES].T
    hm = _mlstm(proj, gates, gates_row, conv_w, conv_b, head_g, nb, seq)
    hp = _pool(proj, 4 * dm // dp, pool_w, pool_scale, seq)
    x = _mm_res([hm, hp], w_out, x, g1, seq)
    return _ffn(x, g_ffn, sh2, sc2, g2, w1, w3, w2, seq)


def _odd_layer(x, mod, seq, g_mix, g_ffn, g_final, w_qkv, w_o, w_router, w1, w3, w2):
    n, d = x.shape
    nb = mod.shape[0]
    sh1, sc1, g1, sh2, sc2, g2 = jnp.split(mod, 6, axis=-1)
    qkv = _nm_matmul(x, g_mix, sh1, sc1, w_qkv, BF16, seq)
    att = _sb_attention(qkv, nb, seq)
    x = _mm_res([att], w_o, x, g1, seq)
    h, info, gates, counts = _router(x, g_ffn, sh2, sc2, w_router, seq)
    tm = MOE_TILE_SUBS * MOE_SUB
    pos0, pos1, visits, r_pad = _route_plan(info, counts, tm)
    hs = _dispatch(h, pos0, pos1, r_pad, seq)
    y = _moe(hs, visits, w1, w3, w2, tm)
    return _combine(y, pos0, pos1, gates, x, g2, g_final, seq)


def kernel(x, c, ada_w, ada_b, norm_mix_g, norm_ffn_g, norm_final_g, ev_w_in, ev_b_gates, ev_conv_w,
           ev_conv_b, ev_head_g, ev_pool_w, ev_pool_scale, ev_w_out, ev_ffn_w1, ev_ffn_w3, ev_ffn_w2,
           od_w_qkv, od_w_o, od_router, od_moe_w1, od_moe_w3, od_moe_w2):
    nb, seq, d = x.shape
    assert ada_w.shape[0] == 2, "one even and one odd layer"
    mod = _adaln(c, ada_w, ada_b)
    xf = x.reshape(nb * seq, d)
    xf = _even_layer(xf, mod[0], seq, norm_mix_g[0], norm_ffn_g[0], ev_w_in[0], ev_b_gates[0],
                     ev_conv_w[0], ev_conv_b[0], ev_head_g[0], ev_pool_w[0], ev_pool_scale[0],
                     ev_w_out[0], ev_ffn_w1[0], ev_ffn_w3[0], ev_ffn_w2[0])
    out = _odd_layer(xf, mod[1], seq, norm_mix_g[1], norm_ffn_g[1], norm_final_g, od_w_qkv[0], od_w_o[0],
                     od_router[0], od_moe_w1[0], od_moe_w3[0], od_moe_w2[0])
    return out.reshape(nb, seq, d)
```

```python
import functools

import jax
import jax.numpy as jnp
from jax import lax
from jax.experimental import pallas as pl
from jax.experimental.pallas import tpu as pltpu

F32 = jnp.float32
BF16 = jnp.bfloat16
I32 = jnp.int32

EPS = 1e-6
MLSTM_HEADS = 4
CONV_K = 4
POOL_WINDOWS = (2, 4, 8, 16)
SB_HEADS = 16
N_EXPERTS = 8
TOP_K = 2

LANES = 128
SUBLANES = 8
VMEM_LIMIT = 56 * 1024 * 1024
PROJ_ROWS = 2048
NORM_ROWS = 512
MLSTM_TILE = 256
POOL_HALO = 16
ATTN_TILE = 256
ATTN_HEADS_PER_STEP = 2
ATTN_LOG_CUTOFF = -120.0
MOE_SUB = 256
MOE_TILE_SUBS = 9
MOE_LONG = 3
DMA_UNROLL = 8

HIGHEST = lax.Precision.HIGHEST


def _params(*sem):
    return pltpu.CompilerParams(dimension_semantics=sem, vmem_limit_bytes=VMEM_LIMIT)


def _tile(n, pref):
    t = min(n, pref)
    assert n % t == 0, (n, pref)
    return t


def _sigmoid(x):
    return 1.0 / (1.0 + jnp.exp(-x))


def _log_sigmoid(x):
    return jnp.minimum(x, 0.0) - jnp.log1p(jnp.exp(-jnp.abs(x)))


def _norm_mod(x, g, shift, scale):
    ms = jnp.mean(x * x, axis=-1, keepdims=True)
    y = x * lax.rsqrt(ms + EPS) * g
    return y * (1.0 + scale) + shift


def _adaln_kernel(cb_ref, w_ref, b_ref, o_ref):
    nb = cb_ref.shape[0]
    for j in range(w_ref.shape[1] // LANES):
        cols = pl.ds(j * LANES, LANES)
        w = w_ref[:, cols]
        for b in range(nb):
            o_ref[pl.ds(b, 1), cols] = jnp.sum(w * cb_ref[b], axis=0, keepdims=True) + b_ref[:, cols]


def _adaln(c, ada_w, ada_b):
    depth, d, n6 = ada_w.shape
    nb = c.shape[0]
    tn = _tile(n6, 1024)
    c_act = c * _sigmoid(c)
    cb = jnp.broadcast_to(c_act[:, :, None], (nb, d, LANES))
    return pl.pallas_call(
        _adaln_kernel,
        grid=(depth, n6 // tn),
        in_specs=[
            pl.BlockSpec((nb, d, LANES), lambda l, j: (0, 0, 0)),
            pl.BlockSpec((None, d, tn), lambda l, j: (l, 0, j)),
            pl.BlockSpec((None, 1, tn), lambda l, j: (l, 0, j)),
        ],
        out_specs=pl.BlockSpec((None, nb, tn), lambda l, j: (l, 0, j)),
        out_shape=jax.ShapeDtypeStruct((depth, nb, n6), F32),
        compiler_params=_params("parallel", "parallel"),
        name="adaln",
    )(cb, ada_w, ada_b.reshape(depth, 1, n6))


def _nm_matmul_kernel(x_ref, g_ref, sh_ref, sc_ref, w_ref, *rest, with_side, n_head):
    if with_side:
        wt_ref, ws_ref, bs_ref, o_ref, side_ref, h_ref = rest
    else:
        o_ref, h_ref = rest
    j = pl.program_id(1)

    @pl.when(j == 0)
    def _():
        tm = x_ref.shape[0]
        chunk = _tile(tm, NORM_ROWS)
        for c in range(tm // chunk):
            rows = pl.ds(c * chunk, chunk)
            h = _norm_mod(x_ref[rows, :], g_ref[...], sh_ref[...], sc_ref[...])
            h_ref[rows, :] = h.astype(BF16)
            if with_side:
                side_ref[rows, :] = jnp.dot(h, ws_ref[...], precision=HIGHEST,
                                            preferred_element_type=F32) + bs_ref[...]

    def project(weights_ref):
        o_ref[...] = jnp.dot(h_ref[...], weights_ref[...].astype(BF16),
                             preferred_element_type=F32).astype(o_ref.dtype)

    if with_side:
        pl.when(j < n_head)(lambda: project(w_ref))
        pl.when(j >= n_head)(lambda: project(wt_ref))
    else:
        project(w_ref)


def _nm_matmul(x, g, shift, scale, w, out_dtype, seq, head_cols=None, side=None):
    n, d = x.shape
    tm = _tile(seq, PROJ_ROWS)
    per_b = seq // tm
    nb = shift.shape[0]
    vec = lambda a: a.reshape(nb, 1, d)
    if side is None:
        nout = w.shape[1]
        tn = _tile(nout, 512)
        n_head = nout // tn
    else:
        nout = head_cols + side[0].shape[1]
        tn = _tile(head_cols, 256)
        assert side[0].shape[1] % tn == 0
        n_head = head_cols // tn
    in_specs = [
        pl.BlockSpec((tm, d), lambda i, j: (i, 0), pipeline_mode=pl.Buffered(1)),
        pl.BlockSpec((1, d), lambda i, j: (0, 0)),
        pl.BlockSpec((None, 1, d), lambda i, j: (i // per_b, 0, 0)),
        pl.BlockSpec((None, 1, d), lambda i, j: (i // per_b, 0, 0)),
        pl.BlockSpec((d, tn), lambda i, j: (0, jnp.minimum(j, n_head - 1))),
    ]
    args = [x, g.reshape(1, d), vec(shift), vec(scale), w]
    out_specs = pl.BlockSpec((tm, tn), lambda i, j: (i, j))
    out_shape = jax.ShapeDtypeStruct((n, nout), out_dtype)
    if side is not None:
        in_specs += [pl.BlockSpec((d, tn), lambda i, j: (0, jnp.maximum(j - n_head, 0))),
                     pl.BlockSpec((d, LANES), lambda i, j: (0, 0)),
                     pl.BlockSpec((1, LANES), lambda i, j: (0, 0))]
        args += list(side)
        out_specs = [out_specs, pl.BlockSpec((tm, LANES), lambda i, j: (i, 0))]
        out_shape = [out_shape, jax.ShapeDtypeStruct((n, LANES), F32)]
    return pl.pallas_call(
        functools.partial(_nm_matmul_kernel, with_side=side is not None, n_head=n_head),
        grid=(n // tm, nout // tn),
        in_specs=in_specs,
        out_specs=out_specs,
        out_shape=out_shape,
        scratch_shapes=[pltpu.VMEM((tm, d), BF16)],
        compiler_params=_params("parallel", "arbitrary"),
        name="norm_mod_matmul",
    )(*args)


def _mlstm_kernel(q_ref, qp_ref, k_ref, kp_ref, v_ref, o_ref, cw_ref, cb_ref, hg_ref, gc_ref, gr_ref,
                  out_ref, ext_ref, c_ref, n_ref, m_ref):
    chunk = pl.program_id(1)
    nh = c_ref.shape[0]
    L, dm = q_ref.shape
    dh = dm // nh

    @pl.when(chunk == 0)
    def _():
        c_ref[...] = jnp.zeros_like(c_ref)
        n_ref[...] = jnp.zeros_like(n_ref)
        m_ref[...] = jnp.zeros_like(m_ref)

    def conv_silu(cur_ref, prev_ref, off):
        ext_ref[pl.ds(0, SUBLANES), :] = jnp.where(chunk == 0, 0.0, prev_ref[...])
        ext_ref[pl.ds(SUBLANES, L), :] = cur_ref[...]
        y = cb_ref[:, pl.ds(off, dm)]
        for j in range(CONV_K):
            y = y + cw_ref[pl.ds(j, 1), pl.ds(off, dm)] * ext_ref[pl.ds(SUBLANES - CONV_K + 1 + j, L), :]
        return y * _sigmoid(y)

    q_all = conv_silu(q_ref, qp_ref, 0) * (dh ** -0.5)
    k_all = conv_silu(k_ref, kp_ref, dm)
    gc = gc_ref[...]
    row = lax.broadcasted_iota(I32, (L, L), 0)
    col = lax.broadcasted_iota(I32, (L, L), 1)
    causal = col <= row

    for head in range(nh):
        cols = pl.ds(head * dh, dh)
        q = q_all[:, head * dh:(head + 1) * dh]
        k = k_all[:, head * dh:(head + 1) * dh]
        qb = q.astype(BF16)
        kb = k.astype(BF16)
        vb = v_ref[:, cols].astype(BF16)

        i_col = gc[:, head:head + 1]
        f_col = gc[:, nh + head:nh + head + 1]
        i_row = gr_ref[pl.ds(head, 1), :]
        f_row = gr_ref[pl.ds(nh + head, 1), :]
        logf_col = _log_sigmoid(f_col)
        logf_row = _log_sigmoid(f_row)
        b_col = jnp.sum(jnp.where(causal, logf_row, 0.0), axis=1, keepdims=True)
        b_row = jnp.sum(jnp.where(row <= col, logf_col, 0.0), axis=0, keepdims=True)

        m_prev = m_ref[pl.ds(head, 1), pl.ds(0, 1)]
        n_prev = n_ref[pl.ds(head, 1), :]
        dmat = jnp.where(causal, b_col - b_row + i_row, -jnp.inf)
        inter = b_col + m_prev
        m_t = jnp.maximum(inter, jnp.max(dmat, axis=1, keepdims=True))
        w_inter = jnp.exp(inter - m_t)
        scores = lax.dot_general(qb, kb, (((1,), (1,)), ((), ())), preferred_element_type=F32)
        wmat = jnp.exp(dmat - m_t) * scores
        c_old = c_ref[head]
        num = (w_inter * jnp.dot(qb, c_old.astype(BF16), preferred_element_type=F32)
               + jnp.dot(wmat.astype(BF16), vb, preferred_element_type=F32))
        den = w_inter * jnp.sum(q * n_prev, axis=1, keepdims=True) + jnp.sum(wmat, axis=1, keepdims=True)
        hval = num / jnp.maximum(jnp.abs(den), jnp.exp(-m_t))

        g_end = jnp.sum(logf_row, axis=1, keepdims=True)
        wl = g_end - b_col + i_col
        m_new = jnp.maximum(g_end + m_prev, jnp.max(wl, axis=0, keepdims=True))
        decay = jnp.exp(g_end + m_prev - m_new)
        wk = jnp.exp(wl - m_new) * k
        c_ref[head] = decay * c_old + lax.dot_general(wk.astype(BF16), vb, (((0,), (0,)), ((), ())),
                                                      preferred_element_type=F32)
        n_ref[pl.ds(head, 1), :] = decay * n_prev + jnp.sum(wk, axis=0, keepdims=True)
        m_ref[pl.ds(head, 1), :] = jnp.broadcast_to(m_new, (1, m_ref.shape[1]))

        ms = jnp.mean(hval * hval, axis=-1, keepdims=True)
        hn = hval * lax.rsqrt(ms + EPS) * hg_ref[:, cols]
        out_ref[:, cols] = (hn * _sigmoid(o_ref[:, cols])).astype(out_ref.dtype)


def _mlstm(proj, gates_col, gates_row, conv_w, conv_b, head_g, nb, seq):
    n = proj.shape[0]
    nh = MLSTM_HEADS
    dm = head_g.shape[0]
    dh = dm // nh
    L = _tile(seq, MLSTM_TILE)
    nc = seq // L
    rb = L // SUBLANES

    assert 2 * nh <= SUBLANES

    def cur(off):
        return pl.BlockSpec((L, dm), lambda b, c: (b * nc + c, off))

    def prev(off):
        return pl.BlockSpec((SUBLANES, dm), lambda b, c: (jnp.maximum((b * nc + c) * rb - 1, 0), off))

    return pl.pallas_call(
        _mlstm_kernel,
        grid=(nb, nc),
        in_specs=[
            cur(0), prev(0), cur(1), prev(1), cur(2), cur(3),
            pl.BlockSpec((CONV_K, 2 * dm), lambda b, c: (0, 0)),
            pl.BlockSpec((1, 2 * dm), lambda b, c: (0, 0)),
            pl.BlockSpec((1, dm), lambda b, c: (0, 0)),
            pl.BlockSpec((L, LANES), lambda b, c: (b * nc + c, 0)),
            pl.BlockSpec((SUBLANES, L), lambda b, c: (0, b * nc + c)),
        ],
        out_specs=pl.BlockSpec((L, dm), lambda b, c: (b * nc + c, 0)),
        out_shape=jax.ShapeDtypeStruct((n, dm), BF16),
        scratch_shapes=[
            pltpu.VMEM((L + SUBLANES, dm), F32),
            pltpu.VMEM((nh, dh, dh), F32),
            pltpu.VMEM((SUBLANES, dh), F32),
            pltpu.VMEM((SUBLANES, LANES), F32),
        ],
        compiler_params=_params("parallel", "arbitrary"),
        name="mlstm",
    )(proj, proj, proj, proj, proj, proj, conv_w, conv_b.reshape(1, -1), head_g.reshape(1, dm),
      gates_col, gates_row)


def _pool_kernel(x_ref, xp_ref, pw_ref, ps_ref, o_ref, buf_ref, *, tiles_per_seq):
    ts, dp = x_ref.shape
    dg = dp // len(POOL_WINDOWS)
    t_in_seq = pl.program_id(0) % tiles_per_seq
    buf_ref[pl.ds(0, POOL_HALO), :] = jnp.where(t_in_seq == 0, 0.0, xp_ref[...])
    buf_ref[pl.ds(POOL_HALO, ts), :] = x_ref[...]
    pos = t_in_seq * ts + lax.broadcasted_iota(I32, (ts, 1), 0)
    for g, w in enumerate(POOL_WINDOWS):
        cols = pl.ds(g * dg, dg)
        xg = buf_ref[pl.ds(POOL_HALO, ts), cols]
        s = xg
        for j in range(1, w):
            s = s + buf_ref[pl.ds(POOL_HALO - j, ts), cols]
        cnt = jnp.minimum(pos + 1, w).astype(F32)
        pooled = s / cnt - xg
        mixed = jnp.dot(pooled.astype(BF16), pw_ref[g].astype(BF16), preferred_element_type=F32)
        o_ref[:, cols] = (mixed * ps_ref[:, cols]).astype(o_ref.dtype)


def _pool(proj, col_block, pool_w, pool_scale, seq):
    n = proj.shape[0]
    dp = pool_scale.shape[0]
    ts = _tile(seq, 512)
    hb = ts // POOL_HALO
    return pl.pallas_call(
        functools.partial(_pool_kernel, tiles_per_seq=seq // ts),
        grid=(n // ts,),
        in_specs=[
            pl.BlockSpec((ts, dp), lambda i: (i, col_block)),
            pl.BlockSpec((POOL_HALO, dp), lambda i: (jnp.maximum(i * hb - 1, 0), col_block)),
            pl.BlockSpec(pool_w.shape, lambda i: (0, 0, 0)),
            pl.BlockSpec((1, dp), lambda i: (0, 0)),
        ],
        out_specs=pl.BlockSpec((ts, dp), lambda i: (i, 0)),
        out_shape=jax.ShapeDtypeStruct((n, dp), BF16),
        scratch_shapes=[pltpu.VMEM((ts + POOL_HALO, dp), F32)],
        compiler_params=_params("parallel"),
        name="pool",
    )(proj, proj, pool_w, pool_scale.reshape(1, dp))


def _mm_res_kernel(*refs, n_a):
    a_refs, w_refs = refs[:n_a], refs[n_a:2 * n_a]
    x_ref, gate_ref, o_ref = refs[2 * n_a:]
    acc = jnp.dot(a_refs[0][...], w_refs[0][...].astype(BF16), preferred_element_type=F32)
    for a_ref, w_ref in zip(a_refs[1:], w_refs[1:]):
        acc = acc + jnp.dot(a_ref[...], w_ref[...].astype(BF16), preferred_element_type=F32)
    o_ref[...] = x_ref[...] + gate_ref[...] * acc


def _mm_res(a_list, w, x, gate, seq):
    n, d = x.shape
    n_a = len(a_list)
    ka = a_list[0].shape[1]
    assert all(a.shape[1] == ka for a in a_list) and w.shape[0] == n_a * ka
    tm = _tile(seq, PROJ_ROWS)
    tn = _tile(d, 512)
    per_b = seq // tm
    nb = gate.shape[0]
    in_specs = [pl.BlockSpec((tm, ka), lambda i, j: (i, 0)) for _ in a_list]
    in_specs += [pl.BlockSpec((ka, tn), functools.partial(lambda i, j, r: (r, j), r=r)) for r in range(n_a)]
    in_specs += [pl.BlockSpec((tm, tn), lambda i, j: (i, j)),
                 pl.BlockSpec((None, 1, tn), lambda i, j: (i // per_b, 0, j))]
    return pl.pallas_call(
        functools.partial(_mm_res_kernel, n_a=n_a),
        grid=(n // tm, d // tn),
        in_specs=in_specs,
        out_specs=pl.BlockSpec((tm, tn), lambda i, j: (i, j)),
        out_shape=jax.ShapeDtypeStruct((n, d), F32),
        compiler_params=_params("parallel", "parallel"),
        name="matmul_residual",
    )(*a_list, *([w] * n_a), x, gate.reshape(nb, 1, d))


def _ffn_kernel(x_ref, g_ref, sh_ref, sc_ref, gate_ref, w1_ref, w3_ref, w2_ref, o_ref, h_ref):
    f = pl.program_id(1)

    @pl.when(f == 0)
    def _():
        h_ref[...] = _norm_mod(x_ref[...], g_ref[...], sh_ref[...], sc_ref[...]).astype(BF16)
        o_ref[...] = jnp.zeros_like(o_ref)

    h = h_ref[...]
    a = jnp.dot(h, w1_ref[...].astype(BF16), preferred_element_type=F32)
    b = jnp.dot(h, w3_ref[...].astype(BF16), preferred_element_type=F32)
    act = (a * _sigmoid(a) * b).astype(BF16)
    o_ref[...] += jnp.dot(act, w2_ref[...].astype(BF16), preferred_element_type=F32)

    @pl.when(f == pl.num_programs(1) - 1)
    def _():
        o_ref[...] = x_ref[...] + gate_ref[...] * o_ref[...]


def _ffn(x, g, shift, scale, gate, w1, w3, w2, seq):
    n, d = x.shape
    dff = w1.shape[1]
    tm = _tile(seq, 1024)
    tf = _tile(dff, 256)
    per_b = seq // tm
    nb = gate.shape[0]
    vec = lambda a: a.reshape(nb, 1, d)
    bvec = pl.BlockSpec((None, 1, d), lambda i, f: (i // per_b, 0, 0))
    return pl.pallas_call(
        _ffn_kernel,
        grid=(n // tm, dff // tf),
        in_specs=[
            pl.BlockSpec((tm, d), lambda i, f: (i, 0), pipeline_mode=pl.Buffered(1)),
            pl.BlockSpec((1, d), lambda i, f: (0, 0)),
            bvec, bvec, bvec,
            pl.BlockSpec((d, tf), lambda i, f: (0, f)),
            pl.BlockSpec((d, tf), lambda i, f: (0, f)),
            pl.BlockSpec((tf, d), lambda i, f: (f, 0)),
        ],
        out_specs=pl.BlockSpec((tm, d), lambda i, f: (i, 0)),
        out_shape=jax.ShapeDtypeStruct((n, d), F32),
        scratch_shapes=[pltpu.VMEM((tm, d), BF16)],
        compiler_params=_params("parallel", "arbitrary"),
        name="swiglu_ffn",
    )(x, g.reshape(1, d), vec(shift), vec(scale), vec(gate), w1, w3, w2)


def _sb_attn_kernel(q_ref, k_ref, v_ref, o_ref):
    qi = pl.program_id(2)
    T = q_ref.shape[0]
    dh = q_ref.shape[1] // ATTN_HEADS_PER_STEP
    scale = dh ** -0.5
    row = lax.broadcasted_iota(I32, (T, T), 0)
    col = lax.broadcasted_iota(I32, (T, T), 1)
    later = (row > col).astype(BF16)

    def prepare(j, g, diagonal):
        start = pl.multiple_of(j * T, T)
        cols = pl.ds(g * dh, dh)
        kj = k_ref[pl.ds(start, T), cols]
        z = lax.dot_general(q_ref[:, cols], kj, (((1,), (1,)), ((), ())), preferred_element_type=F32) * scale
        sp = jnp.maximum(z, 0.0) + jnp.log(1.0 + jnp.exp(-jnp.abs(z)))
        lom = -sp
        if diagonal:
            strict = col < row
            lom = jnp.where(strict, lom, 0.0)
        hi = lom.astype(BF16)
        lo = (lom - hi.astype(F32)).astype(BF16)
        suffix = (jnp.dot(hi, later, preferred_element_type=F32)
                  + jnp.dot(lo, later, preferred_element_type=F32))
        base = z - sp + suffix
        if diagonal:
            base = jnp.where(strict, base, -jnp.inf)
        return base, jnp.sum(lom, axis=1, keepdims=True), v_ref[pl.ds(start, T), cols]

    def walk(tiles, accs, rems):
        new_accs, new_rems = [], []
        for g in range(ATTN_HEADS_PER_STEP):
            parts = [prepare(j, g, diagonal) for j, diagonal in tiles]
            acc, rem = accs[g], rems[g]
            for base, total, vj in parts:
                a = jnp.exp(base + rem)
                acc = acc + jnp.dot(a.astype(BF16), vj, preferred_element_type=F32)
                rem = rem + total
            new_accs.append(acc)
            new_rems.append(rem)
        return tuple(new_accs), tuple(new_rems)

    def alive(rems):
        top = rems[0]
        for r in rems[1:]:
            top = jnp.maximum(top, r)
        return jnp.max(top) > ATTN_LOG_CUTOFF

    def finish(accs):
        for g in range(ATTN_HEADS_PER_STEP):
            o_ref[:, pl.ds(g * dh, dh)] = accs[g].astype(o_ref.dtype)

    zeros = lambda w: tuple(jnp.zeros((T, w), F32) for _ in range(ATTN_HEADS_PER_STEP))

    @pl.when(qi == 0)
    def _():
        accs, _ = walk([(qi, True)], zeros(dh), zeros(1))
        finish(accs)

    @pl.when(qi > 0)
    def _():
        accs, rems = walk([(qi, True), (qi - 1, False)], zeros(dh), zeros(1))

        def cond(c):
            s, live, _, _ = c
            return jnp.logical_and(s < qi, live)

        def body(c):
            s, _, accs, rems = c
            accs, rems = walk([(qi - 1 - s, False)], accs, rems)
            return s + 1, alive(rems), accs, rems

        _, _, accs, _ = lax.while_loop(cond, body, (jnp.int32(1), alive(rems), accs, rems))
        finish(accs)


def _sb_attention(qkv, nb, seq):
    n, d3 = qkv.shape
    d = d3 // 3
    nh = SB_HEADS
    dh = d // nh
    T = _tile(seq, ATTN_TILE)
    nq = seq // T
    hg = ATTN_HEADS_PER_STEP
    ng = nh // hg
    return pl.pallas_call(
        _sb_attn_kernel,
        grid=(nb, ng, nq),
        in_specs=[
            pl.BlockSpec((T, hg * dh), lambda b, h, i: (b * nq + i, h)),
            pl.BlockSpec((seq, hg * dh), lambda b, h, i: (b, ng + h)),
            pl.BlockSpec((seq, hg * dh), lambda b, h, i: (b, 2 * ng + h)),
        ],
        out_specs=pl.BlockSpec((T, hg * dh), lambda b, h, i: (b * nq + i, h)),
        out_shape=jax.ShapeDtypeStruct((n, d), BF16),
        compiler_params=_params("parallel", "parallel", "parallel"),
        name="stick_breaking_attention",
    )(qkv, qkv, qkv)


def _router_kernel(x_ref, g_ref, sh_ref, sc_ref, wr_ref, h_ref, info_ref, gate_ref, cnt_ref, carry_ref):
    tm = x_ref.shape[0]

    @pl.when(pl.program_id(0) == 0)
    def _():
        carry_ref[...] = jnp.zeros_like(carry_ref)

    h = _norm_mod(x_ref[...], g_ref[...], sh_ref[...], sc_ref[...])
    h_ref[...] = h
    logits = jnp.dot(h, wr_ref[...], precision=HIGHEST, preferred_element_type=F32)
    lane = lax.broadcasted_iota(I32, (tm, LANES), 1)
    lane_f = lane.astype(F32)
    lg = jnp.where(lane < N_EXPERTS, logits, -jnp.inf)
    v0 = jnp.max(lg, axis=1, keepdims=True)
    i0 = jnp.min(jnp.where(lg == v0, lane_f, float(LANES)), axis=1, keepdims=True)
    lg1 = jnp.where(lane_f == i0, -jnp.inf, lg)
    v1 = jnp.max(lg1, axis=1, keepdims=True)
    i1 = jnp.min(jnp.where(lg1 == v1, lane_f, float(LANES)), axis=1, keepdims=True)
    ex = jnp.exp(v1 - v0)
    g0 = 1.0 / (1.0 + ex)
    g1 = ex / (1.0 + ex)
    sel0 = lane_f == i0
    sel1 = lane_f == i1
    onehot = jnp.where(sel0 | sel1, 1.0, 0.0)
    row = lax.broadcasted_iota(I32, (tm, tm), 0)
    col = lax.broadcasted_iota(I32, (tm, tm), 1)
    before = (col < row).astype(BF16)
    earlier = jnp.dot(before, onehot.astype(BF16), preferred_element_type=F32) + carry_ref[...]
    r0 = jnp.sum(jnp.where(sel0, earlier, 0.0), axis=1, keepdims=True)
    r1 = jnp.sum(jnp.where(sel1, earlier, 0.0), axis=1, keepdims=True)
    carry_ref[...] += jnp.sum(onehot, axis=0, keepdims=True)
    info = jnp.where(lane == 0, i0, jnp.where(lane == 1, i1, jnp.where(lane == 2, r0, jnp.where(lane == 3, r1, 0.0))))
    info_ref[...] = info.astype(I32)
    gate_ref[...] = jnp.where(lane == 0, g0, jnp.where(lane == 1, g1, 0.0))
    cnt_ref[...] = carry_ref[...].astype(I32)


def _router(x, g, shift, scale, w_router, seq):
    n, d = x.shape
    tm = _tile(seq, 512)
    per_b = seq // tm
    nb = shift.shape[0]
    vec = lambda a: a.reshape(nb, 1, d)
    bvec = pl.BlockSpec((None, 1, d), lambda i: (i // per_b, 0, 0))
    wr = jnp.pad(w_router, ((0, 0), (0, LANES - w_router.shape[1])))
    return pl.pallas_call(
        _router_kernel,
        grid=(n // tm,),
        in_specs=[
            pl.BlockSpec((tm, d), lambda i: (i, 0)),
            pl.BlockSpec((1, d), lambda i: (0, 0)),
            bvec, bvec,
            pl.BlockSpec((d, LANES), lambda i: (0, 0)),
        ],
        out_specs=[
            pl.BlockSpec((tm, d), lambda i: (i, 0)),
            pl.BlockSpec((tm, LANES), lambda i: (i, 0)),
            pl.BlockSpec((tm, LANES), lambda i: (i, 0)),
            pl.BlockSpec((1, LANES), lambda i: (0, 0)),
        ],
        out_shape=[
            jax.ShapeDtypeStruct((n, d), F32),
            jax.ShapeDtypeStruct((n, LANES), I32),
            jax.ShapeDtypeStruct((n, LANES), F32),
            jax.ShapeDtypeStruct((1, LANES), I32),
        ],
        scratch_shapes=[pltpu.VMEM((1, LANES), F32)],
        compiler_params=_params("arbitrary"),
        name="router",
    )(x, g.reshape(1, d), vec(shift), vec(scale), wr)


def _route_plan(info, counts, tm):
    n = info.shape[0]
    ne = N_EXPERTS
    e0, e1, r0, r1 = info[:, 0], info[:, 1], info[:, 2], info[:, 3]
    cnt = counts[0, :ne]
    padded = (cnt + MOE_SUB - 1) // MOE_SUB * MOE_SUB
    ends = jnp.cumsum(padded)
    starts = ends - padded
    pos0 = (starts[e0] + r0).astype(I32)
    pos1 = (starts[e1] + r1).astype(I32)
    r_pad = TOP_K * n + ne * MOE_SUB
    nv = -(-r_pad // tm) + ne
    ntiles = (padded + tm - 1) // tm
    v_end = jnp.cumsum(ntiles)
    v_first = v_end - ntiles
    total = v_end[-1]
    vid = jnp.arange(nv, dtype=I32)
    vc = jnp.minimum(vid, total - 1)
    grp = jnp.sum((v_end[None, :] <= vc[:, None]).astype(I32), axis=1)
    k = vc - v_first[grp]
    v_start = (starts[grp] + k * tm).astype(I32)
    v_nsub = jnp.where(vid < total, jnp.minimum(tm, padded[grp] - k * tm) // MOE_SUB, 0).astype(I32)
    tail = ends[-1] + MOE_SUB * jnp.arange(ne, dtype=I32)
    z_start = jnp.concatenate([jnp.maximum(ends - MOE_SUB, 0), tail]).astype(I32)
    z_on = jnp.concatenate([padded > 0, tail < r_pad]).astype(I32)
    return pos0, pos1, (grp.astype(I32), v_start, v_nsub), (z_start, z_on), r_pad


def _dispatch_kernel(p0_ref, p1_ref, zs_ref, zn_ref, h_ref, hs_hbm, zero_ref, sem, zsem):
    i = pl.program_id(0)
    td = h_ref.shape[0]
    sub = zero_ref.shape[0]

    def zero_copy(b):
        dst = hs_hbm.at[pl.ds(pl.multiple_of(zs_ref[b], sub), sub), :]
        return pltpu.make_async_copy(zero_ref, dst, zsem)

    @pl.when(i == 0)
    def _():
        zero_ref[...] = jnp.zeros_like(zero_ref)
        for b in range(zs_ref.shape[0]):
            pl.when(zn_ref[b] == 1)(lambda: zero_copy(b).start())
        for b in range(zs_ref.shape[0]):
            pl.when(zn_ref[b] == 1)(lambda: zero_copy(b).wait())

    def issue(r, c):
        t = i * td + r
        src = h_ref.at[pl.ds(r, 1), :]
        pltpu.make_async_copy(src, hs_hbm.at[pl.ds(p0_ref[t], 1), :], sem.at[0]).start()
        pltpu.make_async_copy(src, hs_hbm.at[pl.ds(p1_ref[t], 1), :], sem.at[1]).start()
        return c
    lax.fori_loop(0, td, issue, 0, unroll=DMA_UNROLL)
    for stream in range(TOP_K):
        pltpu.make_async_copy(h_ref, hs_hbm.at[pl.ds(0, td), :], sem.at[stream]).wait()


def _dispatch(h, pos0, pos1, zero_blocks, r_pad, seq):
    n, d = h.shape
    td = _tile(seq, 512)
    z_start, z_on = zero_blocks
    return pl.pallas_call(
        _dispatch_kernel,
        grid_spec=pltpu.PrefetchScalarGridSpec(
            num_scalar_prefetch=4,
            grid=(n // td,),
            in_specs=[pl.BlockSpec((td, d), lambda i, p0, p1, zs, zn: (i, 0))],
            out_specs=pl.BlockSpec(memory_space=pl.ANY),
            scratch_shapes=[
                pltpu.VMEM((MOE_SUB, d), F32),
                pltpu.SemaphoreType.DMA((TOP_K,)),
                pltpu.SemaphoreType.DMA(()),
            ],
        ),
        out_shape=jax.ShapeDtypeStruct((r_pad, d), F32),
        compiler_params=_params("arbitrary"),
        name="moe_dispatch",
    )(pos0, pos1, z_start, z_on, h)


def _moe_kernel(vg_ref, vs_ref, vn_ref, hs_hbm, w1_ref, w3_ref, w2_ref, y_hbm,
                stage_ref, hb_ref, acc_ref, wb1_ref, wb3_ref, wb2_ref, sem_in, sem_out):
    v = pl.program_id(0)
    f = pl.program_id(1)
    sub = stage_ref.shape[1]
    start = vs_ref[v]
    nsub = vn_ref[v]

    def rows(sb):
        return pl.ds(pl.multiple_of(sb * sub, sub), sub)

    def hbm_rows(row0, sb):
        return pl.ds(pl.multiple_of(row0 + sb * sub, sub), sub)

    def in_copy(sb, slot):
        return pltpu.make_async_copy(hs_hbm.at[hbm_rows(start, sb), :], stage_ref.at[slot], sem_in.at[slot])

    def out_copy(row0, sb):
        return pltpu.make_async_copy(acc_ref.at[rows(sb), :], y_hbm.at[hbm_rows(row0, sb), :], sem_out)

    def each_block(count, fn):
        def body(sb, c):
            fn(sb)
            return c
        lax.fori_loop(0, count, body, 0)

    @pl.when(f == 0)
    def _():
        @pl.when(nsub > 0)
        def _():
            in_copy(0, 0).start()

        def load(sb):
            slot = sb % 2

            @pl.when(sb + 1 < nsub)
            def _():
                in_copy(sb + 1, 1 - slot).start()
            in_copy(sb, slot).wait()
            hb_ref[rows(sb), :] = stage_ref[slot].astype(BF16)
        each_block(nsub, load)

        prev = jnp.maximum(v - 1, 0)
        n_prev = jnp.where(v > 0, vn_ref[prev], 0)
        each_block(n_prev, lambda sb: out_copy(vs_ref[prev], sb).wait())

        def clear(sb):
            acc_ref[rows(sb), :] = jnp.zeros((sub, acc_ref.shape[1]), F32)
        each_block(nsub, clear)

    @pl.when(nsub > 0)
    def _():
        wb1_ref[...] = w1_ref[...].astype(BF16)
        wb3_ref[...] = w3_ref[...].astype(BF16)
        wb2_ref[...] = w2_ref[...].astype(BF16)

        def blocks(sb, count):
            r = pl.ds(pl.multiple_of(sb * sub, sub), count * sub)
            hs = hb_ref[r, :]
            a = jnp.dot(hs, wb1_ref[...], preferred_element_type=F32)
            b = jnp.dot(hs, wb3_ref[...], preferred_element_type=F32)
            act = (a * _sigmoid(a) * b).astype(BF16)
            acc_ref[r, :] += jnp.dot(act, wb2_ref[...], preferred_element_type=F32)

        n_long = nsub // MOE_LONG
        each_block(n_long, lambda i: blocks(i * MOE_LONG, MOE_LONG))
        each_block(nsub - n_long * MOE_LONG, lambda i: blocks(n_long * MOE_LONG + i, 1))

    @pl.when(f == pl.num_programs(1) - 1)
    def _():
        each_block(nsub, lambda sb: out_copy(start, sb).start())

        @pl.when(v == pl.num_programs(0) - 1)
        def _():
            each_block(nsub, lambda sb: out_copy(start, sb).wait())


def _moe(hs, visits, w1, w3, w2, tm):
    r_pad, d = hs.shape
    _, _, dff = w1.shape
    v_group, v_start, v_nsub = visits
    nv = v_group.shape[0]
    tf = _tile(dff, 256)
    nf = dff // tf

    def f_eff(v, f, vn):
        return jnp.where(vn[v] > 0, f, nf - 1)

    return pl.pallas_call(
        _moe_kernel,
        grid_spec=pltpu.PrefetchScalarGridSpec(
            num_scalar_prefetch=3,
            grid=(nv, nf),
            in_specs=[
                pl.BlockSpec(memory_space=pl.ANY),
                pl.BlockSpec((None, d, tf), lambda v, f, vg, vs, vn: (vg[v], 0, f_eff(v, f, vn))),
                pl.BlockSpec((None, d, tf), lambda v, f, vg, vs, vn: (vg[v], 0, f_eff(v, f, vn))),
                pl.BlockSpec((None, tf, d), lambda v, f, vg, vs, vn: (vg[v], f_eff(v, f, vn), 0)),
            ],
            out_specs=pl.BlockSpec(memory_space=pl.ANY),
            scratch_shapes=[
                pltpu.VMEM((2, MOE_SUB, d), F32),
                pltpu.VMEM((tm, d), BF16),
                pltpu.VMEM((tm, d), F32),
                pltpu.VMEM((d, tf), BF16),
                pltpu.VMEM((d, tf), BF16),
                pltpu.VMEM((tf, d), BF16),
                pltpu.SemaphoreType.DMA((2,)),
                pltpu.SemaphoreType.DMA(()),
            ],
        ),
        out_shape=jax.ShapeDtypeStruct((r_pad, d), F32),
        input_output_aliases={3: 0},
        compiler_params=_params("arbitrary", "arbitrary"),
        name="moe_experts",
    )(v_group, v_start, v_nsub, hs, w1, w3, w2)


def _combine_kernel(p0_ref, p1_ref, y_hbm, x_ref, gate_ref, rg_ref, gf_ref, o_ref, buf_ref, sem):
    i = pl.program_id(0)
    tc = x_ref.shape[0]
    slot = i % 2

    def gather(tile, into):
        def issue(r, c):
            t = tile * tc + r
            for k, p_ref in enumerate((p0_ref, p1_ref)):
                pltpu.make_async_copy(y_hbm.at[pl.ds(p_ref[t], 1), :],
                                      buf_ref.at[into, k, pl.ds(r, 1), :], sem.at[into, k]).start()
            return c
        lax.fori_loop(0, tc, issue, 0, unroll=DMA_UNROLL)

    @pl.when(i == 0)
    def _():
        gather(0, 0)

    @pl.when(i + 1 < pl.num_programs(0))
    def _():
        gather(i + 1, 1 - slot)

    for k in range(TOP_K):
        pltpu.make_async_copy(y_hbm.at[pl.ds(0, tc), :], buf_ref.at[slot, k], sem.at[slot, k]).wait()

    rg = rg_ref[...]
    moe = rg[:, 0:1] * buf_ref[slot, 0] + rg[:, 1:2] * buf_ref[slot, 1]
    xn = x_ref[...] + gate_ref[...] * moe
    ms = jnp.mean(xn * xn, axis=-1, keepdims=True)
    o_ref[...] = xn * lax.rsqrt(ms + EPS) * gf_ref[...]


def _combine(y, pos0, pos1, route_gates, x, gate, g_final, seq):
    n, d = x.shape
    tc = _tile(seq, 256)
    per_b = seq // tc
    nb = gate.shape[0]
    return pl.pallas_call(
        _combine_kernel,
        grid_spec=pltpu.PrefetchScalarGridSpec(
            num_scalar_prefetch=2,
            grid=(n // tc,),
            in_specs=[
                pl.BlockSpec(memory_space=pl.ANY),
                pl.BlockSpec((tc, d), lambda i, p0, p1: (i, 0)),
                pl.BlockSpec((None, 1, d), lambda i, p0, p1: (i // per_b, 0, 0)),
                pl.BlockSpec((tc, LANES), lambda i, p0, p1: (i, 0)),
                pl.BlockSpec((1, d), lambda i, p0, p1: (0, 0)),
            ],
            out_specs=pl.BlockSpec((tc, d), lambda i, p0, p1: (i, 0)),
            scratch_shapes=[
                pltpu.VMEM((2, TOP_K, tc, d), F32),
                pltpu.SemaphoreType.DMA((2, TOP_K)),
            ],
        ),
        out_shape=jax.ShapeDtypeStruct((n, d), F32),
        compiler_params=_params("arbitrary"),
        name="moe_combine_final_norm",
    )(pos0, pos1, y, x, gate.reshape(nb, 1, d), route_gates, g_final.reshape(1, d))


def _even_layer(x, mod, seq, g_mix, g_ffn, w_in, b_gates, conv_w, conv_b, head_g, pool_w, pool_scale,
                w_out, w1, w3, w2):
    n, d = x.shape
    nb = mod.shape[0]
    sh1, sc1, g1, sh2, sc2, g2 = jnp.split(mod, 6, axis=-1)
    dm = head_g.shape[0]
    dp = pool_scale.shape[0]
    ng = 2 * MLSTM_HEADS
    assert dm == dp and w_in.shape[1] == 4 * dm + ng + dp
    w_pool = w_in[:, 4 * dm + ng:]
    w_gate = jnp.pad(w_in[:, 4 * dm:4 * dm + ng], ((0, 0), (0, LANES - ng)))
    b_gate = jnp.pad(b_gates, (0, LANES - ng)).reshape(1, LANES)
    proj, gates = _nm_matmul(x, g_mix, sh1, sc1, w_in, F32, seq, head_cols=4 * dm,
                             side=(w_pool, w_gate, b_gate))
    gates_row = gates[:, :SUBLANES].T
    hm = _mlstm(proj, gates, gates_row, conv_w, conv_b, head_g, nb, seq)
    hp = _pool(proj, 4 * dm // dp, pool_w, pool_scale, seq)
    x = _mm_res([hm, hp], w_out, x, g1, seq)
    return _ffn(x, g_ffn, sh2, sc2, g2, w1, w3, w2, seq)


def _odd_layer(x, mod, seq, g_mix, g_ffn, g_final, w_qkv, w_o, w_router, w1, w3, w2):
    n, d = x.shape
    nb = mod.shape[0]
    sh1, sc1, g1, sh2, sc2, g2 = jnp.split(mod, 6, axis=-1)
    qkv = _nm_matmul(x, g_mix, sh1, sc1, w_qkv, BF16, seq)
    att = _sb_attention(qkv, nb, seq)
    x = _mm_res([att], w_o, x, g1, seq)
    h, info, gates, counts = _router(x, g_ffn, sh2, sc2, w_router, seq)
    tm = MOE_TILE_SUBS * MOE_SUB
    pos0, pos1, visits, zero_blocks, r_pad = _route_plan(info, counts, tm)
    hs = _dispatch(h, pos0, pos1, zero_blocks, r_pad, seq)
    y = _moe(hs, visits, w1, w3, w2, tm)
    return _combine(y, pos0, pos1, gates, x, g2, g_final, seq)


def kernel(x, c, ada_w, ada_b, norm_mix_g, norm_ffn_g, norm_final_g, ev_w_in, ev_b_gates, ev_conv_w,
           ev_conv_b, ev_head_g, ev_pool_w, ev_pool_scale, ev_w_out, ev_ffn_w1, ev_ffn_w3, ev_ffn_w2,
           od_w_qkv, od_w_o, od_router, od_moe_w1, od_moe_w3, od_moe_w2):
    nb, seq, d = x.shape
    assert ada_w.shape[0] == 2, "one even and one odd layer"
    mod = _adaln(c, ada_w, ada_b)
    xf = x.reshape(nb * seq, d)
    xf = _even_layer(xf, mod[0], seq, norm_mix_g[0], norm_ffn_g[0], ev_w_in[0], ev_b_gates[0],
                     ev_conv_w[0], ev_conv_b[0], ev_head_g[0], ev_pool_w[0], ev_pool_scale[0],
                     ev_w_out[0], ev_ffn_w1[0], ev_ffn_w3[0], ev_ffn_w2[0])
    out = _odd_layer(xf, mod[1], seq, norm_mix_g[1], norm_ffn_g[1], norm_final_g, od_w_qkv[0], od_w_o[0],
                     od_router[0], od_moe_w1[0], od_moe_w3[0], od_moe_w2[0])
    return out.reshape(nb, seq, d)
```

```python
import functools

import jax
import jax.numpy as jnp
from jax import lax
from jax.experimental import pallas as pl
from jax.experimental.pallas import tpu as pltpu

F32 = jnp.float32
BF16 = jnp.bfloat16
I32 = jnp.int32

EPS = 1e-6
MLSTM_HEADS = 4
CONV_K = 4
POOL_WINDOWS = (2, 4, 8, 16)
SB_HEADS = 16
N_EXPERTS = 8
TOP_K = 2

LANES = 128
SUBLANES = 8
VMEM_LIMIT = 56 * 1024 * 1024
PROJ_ROWS = 2048
NORM_ROWS = 512
MLSTM_TILE = 256
POOL_HALO = 16
ATTN_TILE = 256
ATTN_HEADS_PER_STEP = 4
ATTN_LOG_CUTOFF = -120.0
MOE_SUB = 256
MOE_TILE_SUBS = 9
MOE_LONG = 3
DMA_UNROLL = 8

HIGHEST = lax.Precision.HIGHEST


def _params(*sem):
    return pltpu.CompilerParams(dimension_semantics=sem, vmem_limit_bytes=VMEM_LIMIT)


def _tile(n, pref):
    t = min(n, pref)
    assert n % t == 0, (n, pref)
    return t


def _sigmoid(x):
    return 1.0 / (1.0 + jnp.exp(-x))


def _log_sigmoid(x):
    return jnp.minimum(x, 0.0) - jnp.log1p(jnp.exp(-jnp.abs(x)))


def _norm_mod(x, g, shift, scale):
    ms = jnp.mean(x * x, axis=-1, keepdims=True)
    y = x * lax.rsqrt(ms + EPS) * g
    return y * (1.0 + scale) + shift


def _adaln_kernel(cb_ref, w_ref, b_ref, o_ref):
    nb = cb_ref.shape[0]
    for j in range(w_ref.shape[1] // LANES):
        cols = pl.ds(j * LANES, LANES)
        w = w_ref[:, cols]
        for b in range(nb):
            o_ref[pl.ds(b, 1), cols] = jnp.sum(w * cb_ref[b], axis=0, keepdims=True) + b_ref[:, cols]


def _adaln(c, ada_w, ada_b):
    depth, d, n6 = ada_w.shape
    nb = c.shape[0]
    tn = _tile(n6, 1024)
    c_act = c * _sigmoid(c)
    cb = jnp.broadcast_to(c_act[:, :, None], (nb, d, LANES))
    return pl.pallas_call(
        _adaln_kernel,
        grid=(depth, n6 // tn),
        in_specs=[
            pl.BlockSpec((nb, d, LANES), lambda l, j: (0, 0, 0)),
            pl.BlockSpec((None, d, tn), lambda l, j: (l, 0, j)),
            pl.BlockSpec((None, 1, tn), lambda l, j: (l, 0, j)),
        ],
        out_specs=pl.BlockSpec((None, nb, tn), lambda l, j: (l, 0, j)),
        out_shape=jax.ShapeDtypeStruct((depth, nb, n6), F32),
        compiler_params=_params("parallel", "parallel"),
        name="adaln",
    )(cb, ada_w, ada_b.reshape(depth, 1, n6))


def _nm_matmul_kernel(x_ref, g_ref, sh_ref, sc_ref, w_ref, *rest, with_side, n_head):
    if with_side:
        wt_ref, ws_ref, bs_ref, o_ref, side_ref, h_ref = rest
    else:
        o_ref, h_ref = rest
    j = pl.program_id(1)

    @pl.when(j == 0)
    def _():
        tm = x_ref.shape[0]
        chunk = _tile(tm, NORM_ROWS)
        for c in range(tm // chunk):
            rows = pl.ds(c * chunk, chunk)
            h = _norm_mod(x_ref[rows, :], g_ref[...], sh_ref[...], sc_ref[...])
            h_ref[rows, :] = h.astype(BF16)
            if with_side:
                side_ref[rows, :] = jnp.dot(h, ws_ref[...], precision=HIGHEST,
                                            preferred_element_type=F32) + bs_ref[...]

    def project(weights_ref):
        o_ref[...] = jnp.dot(h_ref[...], weights_ref[...].astype(BF16),
                             preferred_element_type=F32).astype(o_ref.dtype)

    if with_side:
        pl.when(j < n_head)(lambda: project(w_ref))
        pl.when(j >= n_head)(lambda: project(wt_ref))
    else:
        project(w_ref)


def _nm_matmul(x, g, shift, scale, w, out_dtype, seq, head_cols=None, side=None):
    n, d = x.shape
    tm = _tile(seq, PROJ_ROWS)
    per_b = seq // tm
    nb = shift.shape[0]
    vec = lambda a: a.reshape(nb, 1, d)
    if side is None:
        nout = w.shape[1]
        tn = _tile(nout, 512)
        n_head = nout // tn
    else:
        nout = head_cols + side[0].shape[1]
        tn = _tile(head_cols, 256)
        assert side[0].shape[1] % tn == 0
        n_head = head_cols // tn
    in_specs = [
        pl.BlockSpec((tm, d), lambda i, j: (i, 0), pipeline_mode=pl.Buffered(1)),
        pl.BlockSpec((1, d), lambda i, j: (0, 0)),
        pl.BlockSpec((None, 1, d), lambda i, j: (i // per_b, 0, 0)),
        pl.BlockSpec((None, 1, d), lambda i, j: (i // per_b, 0, 0)),
        pl.BlockSpec((d, tn), lambda i, j: (0, jnp.minimum(j, n_head - 1))),
    ]
    args = [x, g.reshape(1, d), vec(shift), vec(scale), w]
    out_specs = pl.BlockSpec((tm, tn), lambda i, j: (i, j))
    out_shape = jax.ShapeDtypeStruct((n, nout), out_dtype)
    if side is not None:
        in_specs += [pl.BlockSpec((d, tn), lambda i, j: (0, jnp.maximum(j - n_head, 0))),
                     pl.BlockSpec((d, LANES), lambda i, j: (0, 0)),
                     pl.BlockSpec((1, LANES), lambda i, j: (0, 0))]
        args += list(side)
        out_specs = [out_specs, pl.BlockSpec((tm, LANES), lambda i, j: (i, 0))]
        out_shape = [out_shape, jax.ShapeDtypeStruct((n, LANES), F32)]
    return pl.pallas_call(
        functools.partial(_nm_matmul_kernel, with_side=side is not None, n_head=n_head),
        grid=(n // tm, nout // tn),
        in_specs=in_specs,
        out_specs=out_specs,
        out_shape=out_shape,
        scratch_shapes=[pltpu.VMEM((tm, d), BF16)],
        compiler_params=_params("parallel", "arbitrary"),
        name="norm_mod_matmul",
    )(*args)


def _mlstm_kernel(q_ref, qp_ref, k_ref, kp_ref, v_ref, o_ref, cw_ref, cb_ref, hg_ref, gc_ref, gr_ref,
                  out_ref, ext_ref, c_ref, n_ref, m_ref):
    chunk = pl.program_id(1)
    nh = c_ref.shape[0]
    L, dm = q_ref.shape
    dh = dm // nh

    @pl.when(chunk == 0)
    def _():
        c_ref[...] = jnp.zeros_like(c_ref)
        n_ref[...] = jnp.zeros_like(n_ref)
        m_ref[...] = jnp.zeros_like(m_ref)

    def conv_silu(cur_ref, prev_ref, off):
        ext_ref[pl.ds(0, SUBLANES), :] = jnp.where(chunk == 0, 0.0, prev_ref[...])
        ext_ref[pl.ds(SUBLANES, L), :] = cur_ref[...]
        y = cb_ref[:, pl.ds(off, dm)]
        for j in range(CONV_K):
            y = y + cw_ref[pl.ds(j, 1), pl.ds(off, dm)] * ext_ref[pl.ds(SUBLANES - CONV_K + 1 + j, L), :]
        return y * _sigmoid(y)

    q_all = conv_silu(q_ref, qp_ref, 0) * (dh ** -0.5)
    k_all = conv_silu(k_ref, kp_ref, dm)
    gc = gc_ref[...]
    row = lax.broadcasted_iota(I32, (L, L), 0)
    col = lax.broadcasted_iota(I32, (L, L), 1)
    causal = col <= row

    for head in range(nh):
        cols = pl.ds(head * dh, dh)
        q = q_all[:, head * dh:(head + 1) * dh]
        k = k_all[:, head * dh:(head + 1) * dh]
        qb = q.astype(BF16)
        kb = k.astype(BF16)
        vb = v_ref[:, cols].astype(BF16)

        i_col = gc[:, head:head + 1]
        f_col = gc[:, nh + head:nh + head + 1]
        i_row = gr_ref[pl.ds(head, 1), :]
        f_row = gr_ref[pl.ds(nh + head, 1), :]
        logf_col = _log_sigmoid(f_col)
        logf_row = _log_sigmoid(f_row)
        b_col = jnp.sum(jnp.where(causal, logf_row, 0.0), axis=1, keepdims=True)
        b_row = jnp.sum(jnp.where(row <= col, logf_col, 0.0), axis=0, keepdims=True)

        m_prev = m_ref[pl.ds(head, 1), pl.ds(0, 1)]
        n_prev = n_ref[pl.ds(head, 1), :]
        dmat = jnp.where(causal, b_col - b_row + i_row, -jnp.inf)
        inter = b_col + m_prev
        m_t = jnp.maximum(inter, jnp.max(dmat, axis=1, keepdims=True))
        w_inter = jnp.exp(inter - m_t)
        scores = lax.dot_general(qb, kb, (((1,), (1,)), ((), ())), preferred_element_type=F32)
        wmat = jnp.exp(dmat - m_t) * scores
        c_old = c_ref[head]
        num = (w_inter * jnp.dot(qb, c_old.astype(BF16), preferred_element_type=F32)
               + jnp.dot(wmat.astype(BF16), vb, preferred_element_type=F32))
        den = w_inter * jnp.sum(q * n_prev, axis=1, keepdims=True) + jnp.sum(wmat, axis=1, keepdims=True)
        hval = num / jnp.maximum(jnp.abs(den), jnp.exp(-m_t))

        g_end = jnp.sum(logf_row, axis=1, keepdims=True)
        wl = g_end - b_col + i_col
        m_new = jnp.maximum(g_end + m_prev, jnp.max(wl, axis=0, keepdims=True))
        decay = jnp.exp(g_end + m_prev - m_new)
        wk = jnp.exp(wl - m_new) * k
        c_ref[head] = decay * c_old + lax.dot_general(wk.astype(BF16), vb, (((0,), (0,)), ((), ())),
                                                      preferred_element_type=F32)
        n_ref[pl.ds(head, 1), :] = decay * n_prev + jnp.sum(wk, axis=0, keepdims=True)
        m_ref[pl.ds(head, 1), :] = jnp.broadcast_to(m_new, (1, m_ref.shape[1]))

        ms = jnp.mean(hval * hval, axis=-1, keepdims=True)
        hn = hval * lax.rsqrt(ms + EPS) * hg_ref[:, cols]
        out_ref[:, cols] = (hn * _sigmoid(o_ref[:, cols])).astype(out_ref.dtype)


def _mlstm(proj, gates_col, gates_row, conv_w, conv_b, head_g, nb, seq):
    n = proj.shape[0]
    nh = MLSTM_HEADS
    dm = head_g.shape[0]
    dh = dm // nh
    L = _tile(seq, MLSTM_TILE)
    nc = seq // L
    rb = L // SUBLANES

    assert 2 * nh <= SUBLANES

    def cur(off):
        return pl.BlockSpec((L, dm), lambda b, c: (b * nc + c, off))

    def prev(off):
        return pl.BlockSpec((SUBLANES, dm), lambda b, c: (jnp.maximum((b * nc + c) * rb - 1, 0), off))

    return pl.pallas_call(
        _mlstm_kernel,
        grid=(nb, nc),
        in_specs=[
            cur(0), prev(0), cur(1), prev(1), cur(2), cur(3),
            pl.BlockSpec((CONV_K, 2 * dm), lambda b, c: (0, 0)),
            pl.BlockSpec((1, 2 * dm), lambda b, c: (0, 0)),
            pl.BlockSpec((1, dm), lambda b, c: (0, 0)),
            pl.BlockSpec((L, LANES), lambda b, c: (b * nc + c, 0)),
            pl.BlockSpec((SUBLANES, L), lambda b, c: (0, b * nc + c)),
        ],
        out_specs=pl.BlockSpec((L, dm), lambda b, c: (b * nc + c, 0)),
        out_shape=jax.ShapeDtypeStruct((n, dm), BF16),
        scratch_shapes=[
            pltpu.VMEM((L + SUBLANES, dm), F32),
            pltpu.VMEM((nh, dh, dh), F32),
            pltpu.VMEM((SUBLANES, dh), F32),
            pltpu.VMEM((SUBLANES, LANES), F32),
        ],
        compiler_params=_params("parallel", "arbitrary"),
        name="mlstm",
    )(proj, proj, proj, proj, proj, proj, conv_w, conv_b.reshape(1, -1), head_g.reshape(1, dm),
      gates_col, gates_row)


def _pool_kernel(x_ref, xp_ref, pw_ref, ps_ref, o_ref, buf_ref, *, tiles_per_seq):
    ts, dp = x_ref.shape
    dg = dp // len(POOL_WINDOWS)
    t_in_seq = pl.program_id(0) % tiles_per_seq
    buf_ref[pl.ds(0, POOL_HALO), :] = jnp.where(t_in_seq == 0, 0.0, xp_ref[...])
    buf_ref[pl.ds(POOL_HALO, ts), :] = x_ref[...]
    pos = t_in_seq * ts + lax.broadcasted_iota(I32, (ts, 1), 0)
    for g, w in enumerate(POOL_WINDOWS):
        cols = pl.ds(g * dg, dg)
        xg = buf_ref[pl.ds(POOL_HALO, ts), cols]
        s = xg
        for j in range(1, w):
            s = s + buf_ref[pl.ds(POOL_HALO - j, ts), cols]
        cnt = jnp.minimum(pos + 1, w).astype(F32)
        pooled = s / cnt - xg
        mixed = jnp.dot(pooled.astype(BF16), pw_ref[g].astype(BF16), preferred_element_type=F32)
        o_ref[:, cols] = (mixed * ps_ref[:, cols]).astype(o_ref.dtype)


def _pool(proj, col_block, pool_w, pool_scale, seq):
    n = proj.shape[0]
    dp = pool_scale.shape[0]
    ts = _tile(seq, 512)
    hb = ts // POOL_HALO
    return pl.pallas_call(
        functools.partial(_pool_kernel, tiles_per_seq=seq // ts),
        grid=(n // ts,),
        in_specs=[
            pl.BlockSpec((ts, dp), lambda i: (i, col_block)),
            pl.BlockSpec((POOL_HALO, dp), lambda i: (jnp.maximum(i * hb - 1, 0), col_block)),
            pl.BlockSpec(pool_w.shape, lambda i: (0, 0, 0)),
            pl.BlockSpec((1, dp), lambda i: (0, 0)),
        ],
        out_specs=pl.BlockSpec((ts, dp), lambda i: (i, 0)),
        out_shape=jax.ShapeDtypeStruct((n, dp), BF16),
        scratch_shapes=[pltpu.VMEM((ts + POOL_HALO, dp), F32)],
        compiler_params=_params("parallel"),
        name="pool",
    )(proj, proj, pool_w, pool_scale.reshape(1, dp))


def _mm_res_kernel(*refs, n_a):
    a_refs, w_refs = refs[:n_a], refs[n_a:2 * n_a]
    x_ref, gate_ref, o_ref = refs[2 * n_a:]
    acc = jnp.dot(a_refs[0][...], w_refs[0][...].astype(BF16), preferred_element_type=F32)
    for a_ref, w_ref in zip(a_refs[1:], w_refs[1:]):
        acc = acc + jnp.dot(a_ref[...], w_ref[...].astype(BF16), preferred_element_type=F32)
    o_ref[...] = x_ref[...] + gate_ref[...] * acc


def _mm_res(a_list, w, x, gate, seq):
    n, d = x.shape
    n_a = len(a_list)
    ka = a_list[0].shape[1]
    assert all(a.shape[1] == ka for a in a_list) and w.shape[0] == n_a * ka
    tm = _tile(seq, PROJ_ROWS)
    tn = _tile(d, 512)
    per_b = seq // tm
    nb = gate.shape[0]
    in_specs = [pl.BlockSpec((tm, ka), lambda i, j: (i, 0)) for _ in a_list]
    in_specs += [pl.BlockSpec((ka, tn), functools.partial(lambda i, j, r: (r, j), r=r)) for r in range(n_a)]
    in_specs += [pl.BlockSpec((tm, tn), lambda i, j: (i, j)),
                 pl.BlockSpec((None, 1, tn), lambda i, j: (i // per_b, 0, j))]
    return pl.pallas_call(
        functools.partial(_mm_res_kernel, n_a=n_a),
        grid=(n // tm, d // tn),
        in_specs=in_specs,
        out_specs=pl.BlockSpec((tm, tn), lambda i, j: (i, j)),
        out_shape=jax.ShapeDtypeStruct((n, d), F32),
        compiler_params=_params("parallel", "parallel"),
        name="matmul_residual",
    )(*a_list, *([w] * n_a), x, gate.reshape(nb, 1, d))


def _ffn_kernel(x_ref, g_ref, sh_ref, sc_ref, gate_ref, w1_ref, w3_ref, w2_ref, o_ref, h_ref):
    f = pl.program_id(1)

    @pl.when(f == 0)
    def _():
        h_ref[...] = _norm_mod(x_ref[...], g_ref[...], sh_ref[...], sc_ref[...]).astype(BF16)
        o_ref[...] = jnp.zeros_like(o_ref)

    h = h_ref[...]
    a = jnp.dot(h, w1_ref[...].astype(BF16), preferred_element_type=F32)
    b = jnp.dot(h, w3_ref[...].astype(BF16), preferred_element_type=F32)
    act = (a * _sigmoid(a) * b).astype(BF16)
    o_ref[...] += jnp.dot(act, w2_ref[...].astype(BF16), preferred_element_type=F32)

    @pl.when(f == pl.num_programs(1) - 1)
    def _():
        o_ref[...] = x_ref[...] + gate_ref[...] * o_ref[...]


def _ffn(x, g, shift, scale, gate, w1, w3, w2, seq):
    n, d = x.shape
    dff = w1.shape[1]
    tm = _tile(seq, 1024)
    tf = _tile(dff, 256)
    per_b = seq // tm
    nb = gate.shape[0]
    vec = lambda a: a.reshape(nb, 1, d)
    bvec = pl.BlockSpec((None, 1, d), lambda i, f: (i // per_b, 0, 0))
    return pl.pallas_call(
        _ffn_kernel,
        grid=(n // tm, dff // tf),
        in_specs=[
            pl.BlockSpec((tm, d), lambda i, f: (i, 0), pipeline_mode=pl.Buffered(1)),
            pl.BlockSpec((1, d), lambda i, f: (0, 0)),
            bvec, bvec, bvec,
            pl.BlockSpec((d, tf), lambda i, f: (0, f)),
            pl.BlockSpec((d, tf), lambda i, f: (0, f)),
            pl.BlockSpec((tf, d), lambda i, f: (f, 0)),
        ],
        out_specs=pl.BlockSpec((tm, d), lambda i, f: (i, 0)),
        out_shape=jax.ShapeDtypeStruct((n, d), F32),
        scratch_shapes=[pltpu.VMEM((tm, d), BF16)],
        compiler_params=_params("parallel", "arbitrary"),
        name="swiglu_ffn",
    )(x, g.reshape(1, d), vec(shift), vec(scale), vec(gate), w1, w3, w2)


def _sb_attn_kernel(q_ref, k_ref, v_ref, o_ref):
    qi = pl.program_id(2)
    T = q_ref.shape[0]
    dh = q_ref.shape[1] // ATTN_HEADS_PER_STEP
    scale = dh ** -0.5
    row = lax.broadcasted_iota(I32, (T, T), 0)
    col = lax.broadcasted_iota(I32, (T, T), 1)
    later = (row > col).astype(BF16)

    def prepare(j, g, diagonal):
        start = pl.multiple_of(j * T, T)
        cols = pl.ds(g * dh, dh)
        kj = k_ref[pl.ds(start, T), cols]
        z = lax.dot_general(q_ref[:, cols], kj, (((1,), (1,)), ((), ())), preferred_element_type=F32) * scale
        sp = jnp.maximum(z, 0.0) + jnp.log(1.0 + jnp.exp(-jnp.abs(z)))
        lom = -sp
        if diagonal:
            strict = col < row
            lom = jnp.where(strict, lom, 0.0)
        hi = lom.astype(BF16)
        lo = (lom - hi.astype(F32)).astype(BF16)
        suffix = (jnp.dot(hi, later, preferred_element_type=F32)
                  + jnp.dot(lo, later, preferred_element_type=F32))
        base = z - sp + suffix
        if diagonal:
            base = jnp.where(strict, base, -jnp.inf)
        return base, jnp.sum(lom, axis=1, keepdims=True), v_ref[pl.ds(start, T), cols]

    def walk(tiles, accs, rems):
        new_accs, new_rems = [], []
        for g in range(ATTN_HEADS_PER_STEP):
            parts = [prepare(j, g, diagonal) for j, diagonal in tiles]
            acc, rem = accs[g], rems[g]
            for base, total, vj in parts:
                a = jnp.exp(base + rem)
                acc = acc + jnp.dot(a.astype(BF16), vj, preferred_element_type=F32)
                rem = rem + total
            new_accs.append(acc)
            new_rems.append(rem)
        return tuple(new_accs), tuple(new_rems)

    def alive(rems):
        top = rems[0]
        for r in rems[1:]:
            top = jnp.maximum(top, r)
        return jnp.max(top) > ATTN_LOG_CUTOFF

    def finish(accs):
        for g in range(ATTN_HEADS_PER_STEP):
            o_ref[:, pl.ds(g * dh, dh)] = accs[g].astype(o_ref.dtype)

    zeros = lambda w: tuple(jnp.zeros((T, w), F32) for _ in range(ATTN_HEADS_PER_STEP))

    @pl.when(qi == 0)
    def _():
        accs, _ = walk([(qi, True)], zeros(dh), zeros(1))
        finish(accs)

    @pl.when(qi > 0)
    def _():
        accs, rems = walk([(qi, True), (qi - 1, False)], zeros(dh), zeros(1))

        def cond(c):
            s, live, _, _ = c
            return jnp.logical_and(s < qi, live)

        def body(c):
            s, _, accs, rems = c
            accs, rems = walk([(qi - 1 - s, False)], accs, rems)
            return s + 1, alive(rems), accs, rems

        _, _, accs, _ = lax.while_loop(cond, body, (jnp.int32(1), alive(rems), accs, rems))
        finish(accs)


def _sb_attention(qkv, nb, seq):
    n, d3 = qkv.shape
    d = d3 // 3
    nh = SB_HEADS
    dh = d // nh
    T = _tile(seq, ATTN_TILE)
    nq = seq // T
    hg = ATTN_HEADS_PER_STEP
    ng = nh // hg
    return pl.pallas_call(
        _sb_attn_kernel,
        grid=(nb, ng, nq),
        in_specs=[
            pl.BlockSpec((T, hg * dh), lambda b, h, i: (b * nq + i, h)),
            pl.BlockSpec((seq, hg * dh), lambda b, h, i: (b, ng + h)),
            pl.BlockSpec((seq, hg * dh), lambda b, h, i: (b, 2 * ng + h)),
        ],
        out_specs=pl.BlockSpec((T, hg * dh), lambda b, h, i: (b * nq + i, h)),
        out_shape=jax.ShapeDtypeStruct((n, d), BF16),
        compiler_params=_params("parallel", "parallel", "parallel"),
        name="stick_breaking_attention",
    )(qkv, qkv, qkv)


def _router_kernel(x_ref, g_ref, sh_ref, sc_ref, wr_ref, h_ref, info_ref, gate_ref, cnt_ref, carry_ref):
    tm = x_ref.shape[0]

    @pl.when(pl.program_id(0) == 0)
    def _():
        carry_ref[...] = jnp.zeros_like(carry_ref)

    h = _norm_mod(x_ref[...], g_ref[...], sh_ref[...], sc_ref[...])
    h_ref[...] = h
    logits = jnp.dot(h, wr_ref[...], precision=HIGHEST, preferred_element_type=F32)
    lane = lax.broadcasted_iota(I32, (tm, LANES), 1)
    lane_f = lane.astype(F32)
    lg = jnp.where(lane < N_EXPERTS, logits, -jnp.inf)
    v0 = jnp.max(lg, axis=1, keepdims=True)
    i0 = jnp.min(jnp.where(lg == v0, lane_f, float(LANES)), axis=1, keepdims=True)
    lg1 = jnp.where(lane_f == i0, -jnp.inf, lg)
    v1 = jnp.max(lg1, axis=1, keepdims=True)
    i1 = jnp.min(jnp.where(lg1 == v1, lane_f, float(LANES)), axis=1, keepdims=True)
    ex = jnp.exp(v1 - v0)
    g0 = 1.0 / (1.0 + ex)
    g1 = ex / (1.0 + ex)
    sel0 = lane_f == i0
    sel1 = lane_f == i1
    onehot = jnp.where(sel0 | sel1, 1.0, 0.0)
    row = lax.broadcasted_iota(I32, (tm, tm), 0)
    col = lax.broadcasted_iota(I32, (tm, tm), 1)
    before = (col < row).astype(BF16)
    earlier = jnp.dot(before, onehot.astype(BF16), preferred_element_type=F32) + carry_ref[...]
    r0 = jnp.sum(jnp.where(sel0, earlier, 0.0), axis=1, keepdims=True)
    r1 = jnp.sum(jnp.where(sel1, earlier, 0.0), axis=1, keepdims=True)
    carry_ref[...] += jnp.sum(onehot, axis=0, keepdims=True)
    info = jnp.where(lane == 0, i0, jnp.where(lane == 1, i1, jnp.where(lane == 2, r0, jnp.where(lane == 3, r1, 0.0))))
    info_ref[...] = info.astype(I32)
    gate_ref[...] = jnp.where(lane == 0, g0, jnp.where(lane == 1, g1, 0.0))
    cnt_ref[...] = carry_ref[...].astype(I32)


def _router(x, g, shift, scale, w_router, seq):
    n, d = x.shape
    tm = _tile(seq, 512)
    per_b = seq // tm
    nb = shift.shape[0]
    vec = lambda a: a.reshape(nb, 1, d)
    bvec = pl.BlockSpec((None, 1, d), lambda i: (i // per_b, 0, 0))
    wr = jnp.pad(w_router, ((0, 0), (0, LANES - w_router.shape[1])))
    return pl.pallas_call(
        _router_kernel,
        grid=(n // tm,),
        in_specs=[
            pl.BlockSpec((tm, d), lambda i: (i, 0)),
            pl.BlockSpec((1, d), lambda i: (0, 0)),
            bvec, bvec,
            pl.BlockSpec((d, LANES), lambda i: (0, 0)),
        ],
        out_specs=[
            pl.BlockSpec((tm, d), lambda i: (i, 0)),
            pl.BlockSpec((tm, LANES), lambda i: (i, 0)),
            pl.BlockSpec((tm, LANES), lambda i: (i, 0)),
            pl.BlockSpec((1, LANES), lambda i: (0, 0)),
        ],
        out_shape=[
            jax.ShapeDtypeStruct((n, d), F32),
            jax.ShapeDtypeStruct((n, LANES), I32),
            jax.ShapeDtypeStruct((n, LANES), F32),
            jax.ShapeDtypeStruct((1, LANES), I32),
        ],
        scratch_shapes=[pltpu.VMEM((1, LANES), F32)],
        compiler_params=_params("arbitrary"),
        name="router",
    )(x, g.reshape(1, d), vec(shift), vec(scale), wr)


def _route_plan(info, counts, tm):
    n = info.shape[0]
    ne = N_EXPERTS
    e0, e1, r0, r1 = info[:, 0], info[:, 1], info[:, 2], info[:, 3]
    cnt = counts[0, :ne]
    padded = (cnt + MOE_SUB - 1) // MOE_SUB * MOE_SUB
    ends = jnp.cumsum(padded)
    starts = ends - padded
    pos0 = (starts[e0] + r0).astype(I32)
    pos1 = (starts[e1] + r1).astype(I32)
    r_pad = TOP_K * n + ne * MOE_SUB
    nv = -(-r_pad // tm) + ne
    ntiles = (padded + tm - 1) // tm
    v_end = jnp.cumsum(ntiles)
    v_first = v_end - ntiles
    total = v_end[-1]
    vid = jnp.arange(nv, dtype=I32)
    vc = jnp.minimum(vid, total - 1)
    grp = jnp.sum((v_end[None, :] <= vc[:, None]).astype(I32), axis=1)
    k = vc - v_first[grp]
    v_start = (starts[grp] + k * tm).astype(I32)
    v_nsub = jnp.where(vid < total, jnp.minimum(tm, padded[grp] - k * tm) // MOE_SUB, 0).astype(I32)
    tail = ends[-1] + MOE_SUB * jnp.arange(ne, dtype=I32)
    z_start = jnp.concatenate([jnp.maximum(ends - MOE_SUB, 0), tail]).astype(I32)
    z_on = jnp.concatenate([padded > 0, tail < r_pad]).astype(I32)
    return pos0, pos1, (grp.astype(I32), v_start, v_nsub), (z_start, z_on), r_pad


def _dispatch_kernel(p0_ref, p1_ref, zs_ref, zn_ref, h_ref, hs_hbm, zero_ref, sem, zsem):
    i = pl.program_id(0)
    td = h_ref.shape[0]
    sub = zero_ref.shape[0]

    def zero_copy(b):
        dst = hs_hbm.at[pl.ds(pl.multiple_of(zs_ref[b], sub), sub), :]
        return pltpu.make_async_copy(zero_ref, dst, zsem)

    @pl.when(i == 0)
    def _():
        zero_ref[...] = jnp.zeros_like(zero_ref)
        for b in range(zs_ref.shape[0]):
            pl.when(zn_ref[b] == 1)(lambda: zero_copy(b).start())
        for b in range(zs_ref.shape[0]):
            pl.when(zn_ref[b] == 1)(lambda: zero_copy(b).wait())

    def issue(r, c):
        t = i * td + r
        src = h_ref.at[pl.ds(r, 1), :]
        pltpu.make_async_copy(src, hs_hbm.at[pl.ds(p0_ref[t], 1), :], sem.at[0]).start()
        pltpu.make_async_copy(src, hs_hbm.at[pl.ds(p1_ref[t], 1), :], sem.at[1]).start()
        return c
    lax.fori_loop(0, td, issue, 0, unroll=DMA_UNROLL)
    for stream in range(TOP_K):
        pltpu.make_async_copy(h_ref, hs_hbm.at[pl.ds(0, td), :], sem.at[stream]).wait()


def _dispatch(h, pos0, pos1, zero_blocks, r_pad, seq):
    n, d = h.shape
    td = _tile(seq, 512)
    z_start, z_on = zero_blocks
    return pl.pallas_call(
        _dispatch_kernel,
        grid_spec=pltpu.PrefetchScalarGridSpec(
            num_scalar_prefetch=4,
            grid=(n // td,),
            in_specs=[pl.BlockSpec((td, d), lambda i, p0, p1, zs, zn: (i, 0))],
            out_specs=pl.BlockSpec(memory_space=pl.ANY),
            scratch_shapes=[
                pltpu.VMEM((MOE_SUB, d), F32),
                pltpu.SemaphoreType.DMA((TOP_K,)),
                pltpu.SemaphoreType.DMA(()),
            ],
        ),
        out_shape=jax.ShapeDtypeStruct((r_pad, d), F32),
        compiler_params=_params("arbitrary"),
        name="moe_dispatch",
    )(pos0, pos1, z_start, z_on, h)


def _moe_kernel(vg_ref, vs_ref, vn_ref, hs_hbm, w1_ref, w3_ref, w2_ref, y_hbm,
                stage_ref, hb_ref, acc_ref, wb1_ref, wb3_ref, wb2_ref, sem_in, sem_out):
    v = pl.program_id(0)
    f = pl.program_id(1)
    sub = stage_ref.shape[1]
    start = vs_ref[v]
    nsub = vn_ref[v]

    def rows(sb):
        return pl.ds(pl.multiple_of(sb * sub, sub), sub)

    def hbm_rows(row0, sb):
        return pl.ds(pl.multiple_of(row0 + sb * sub, sub), sub)

    def in_copy(sb, slot):
        return pltpu.make_async_copy(hs_hbm.at[hbm_rows(start, sb), :], stage_ref.at[slot], sem_in.at[slot])

    def out_copy(row0, sb):
        return pltpu.make_async_copy(acc_ref.at[rows(sb), :], y_hbm.at[hbm_rows(row0, sb), :], sem_out)

    def each_block(count, fn):
        def body(sb, c):
            fn(sb)
            return c
        lax.fori_loop(0, count, body, 0)

    @pl.when(f == 0)
    def _():
        @pl.when(nsub > 0)
        def _():
            in_copy(0, 0).start()

        def load(sb):
            slot = sb % 2

            @pl.when(sb + 1 < nsub)
            def _():
                in_copy(sb + 1, 1 - slot).start()
            in_copy(sb, slot).wait()
            hb_ref[rows(sb), :] = stage_ref[slot].astype(BF16)
        each_block(nsub, load)

        prev = jnp.maximum(v - 1, 0)
        n_prev = jnp.where(v > 0, vn_ref[prev], 0)
        each_block(n_prev, lambda sb: out_copy(vs_ref[prev], sb).wait())

        def clear(sb):
            acc_ref[rows(sb), :] = jnp.zeros((sub, acc_ref.shape[1]), F32)
        each_block(nsub, clear)

    @pl.when(nsub > 0)
    def _():
        wb1_ref[...] = w1_ref[...].astype(BF16)
        wb3_ref[...] = w3_ref[...].astype(BF16)
        wb2_ref[...] = w2_ref[...].astype(BF16)

        def blocks(sb, count):
            r = pl.ds(pl.multiple_of(sb * sub, sub), count * sub)
            hs = hb_ref[r, :]
            a = jnp.dot(hs, wb1_ref[...], preferred_element_type=F32)
            b = jnp.dot(hs, wb3_ref[...], preferred_element_type=F32)
            act = (a * _sigmoid(a) * b).astype(BF16)
            acc_ref[r, :] += jnp.dot(act, wb2_ref[...], preferred_element_type=F32)

        n_long = nsub // MOE_LONG
        each_block(n_long, lambda i: blocks(i * MOE_LONG, MOE_LONG))
        for rest in range(1, MOE_LONG):
            pl.when(nsub - n_long * MOE_LONG == rest)(lambda: blocks(n_long * MOE_LONG, rest))

    @pl.when(f == pl.num_programs(1) - 1)
    def _():
        each_block(nsub, lambda sb: out_copy(start, sb).start())

        @pl.when(v == pl.num_programs(0) - 1)
        def _():
            each_block(nsub, lambda sb: out_copy(start, sb).wait())


def _moe(hs, visits, w1, w3, w2, tm):
    r_pad, d = hs.shape
    _, _, dff = w1.shape
    v_group, v_start, v_nsub = visits
    nv = v_group.shape[0]
    tf = _tile(dff, 256)
    nf = dff // tf

    def f_eff(v, f, vn):
        return jnp.where(vn[v] > 0, f, nf - 1)

    return pl.pallas_call(
        _moe_kernel,
        grid_spec=pltpu.PrefetchScalarGridSpec(
            num_scalar_prefetch=3,
            grid=(nv, nf),
            in_specs=[
                pl.BlockSpec(memory_space=pl.ANY),
                pl.BlockSpec((None, d, tf), lambda v, f, vg, vs, vn: (vg[v], 0, f_eff(v, f, vn))),
                pl.BlockSpec((None, d, tf), lambda v, f, vg, vs, vn: (vg[v], 0, f_eff(v, f, vn))),
                pl.BlockSpec((None, tf, d), lambda v, f, vg, vs, vn: (vg[v], f_eff(v, f, vn), 0)),
            ],
            out_specs=pl.BlockSpec(memory_space=pl.ANY),
            scratch_shapes=[
                pltpu.VMEM((2, MOE_SUB, d), F32),
                pltpu.VMEM((tm, d), BF16),
                pltpu.VMEM((tm, d), F32),
                pltpu.VMEM((d, tf), BF16),
                pltpu.VMEM((d, tf), BF16),
                pltpu.VMEM((tf, d), BF16),
                pltpu.SemaphoreType.DMA((2,)),
                pltpu.SemaphoreType.DMA(()),
            ],
        ),
        out_shape=jax.ShapeDtypeStruct((r_pad, d), F32),
        input_output_aliases={3: 0},
        compiler_params=_params("arbitrary", "arbitrary"),
        name="moe_experts",
    )(v_group, v_start, v_nsub, hs, w1, w3, w2)


def _combine_kernel(p0_ref, p1_ref, y_hbm, x_ref, gate_ref, rg_ref, gf_ref, o_ref, buf_ref, sem):
    i = pl.program_id(0)
    tc = x_ref.shape[0]
    slot = i % 2

    def gather(tile, into):
        def issue(r, c):
            t = tile * tc + r
            for k, p_ref in enumerate((p0_ref, p1_ref)):
                pltpu.make_async_copy(y_hbm.at[pl.ds(p_ref[t], 1), :],
                                      buf_ref.at[into, k, pl.ds(r, 1), :], sem.at[into, k]).start()
            return c
        lax.fori_loop(0, tc, issue, 0, unroll=DMA_UNROLL)

    @pl.when(i == 0)
    def _():
        gather(0, 0)

    @pl.when(i + 1 < pl.num_programs(0))
    def _():
        gather(i + 1, 1 - slot)

    for k in range(TOP_K):
        pltpu.make_async_copy(y_hbm.at[pl.ds(0, tc), :], buf_ref.at[slot, k], sem.at[slot, k]).wait()

    rg = rg_ref[...]
    moe = rg[:, 0:1] * buf_ref[slot, 0] + rg[:, 1:2] * buf_ref[slot, 1]
    xn = x_ref[...] + gate_ref[...] * moe
    ms = jnp.mean(xn * xn, axis=-1, keepdims=True)
    o_ref[...] = xn * lax.rsqrt(ms + EPS) * gf_ref[...]


def _combine(y, pos0, pos1, route_gates, x, gate, g_final, seq):
    n, d = x.shape
    tc = _tile(seq, 256)
    per_b = seq // tc
    nb = gate.shape[0]
    return pl.pallas_call(
        _combine_kernel,
        grid_spec=pltpu.PrefetchScalarGridSpec(
            num_scalar_prefetch=2,
            grid=(n // tc,),
            in_specs=[
                pl.BlockSpec(memory_space=pl.ANY),
                pl.BlockSpec((tc, d), lambda i, p0, p1: (i, 0)),
                pl.BlockSpec((None, 1, d), lambda i, p0, p1: (i // per_b, 0, 0)),
                pl.BlockSpec((tc, LANES), lambda i, p0, p1: (i, 0)),
                pl.BlockSpec((1, d), lambda i, p0, p1: (0, 0)),
            ],
            out_specs=pl.BlockSpec((tc, d), lambda i, p0, p1: (i, 0)),
            scratch_shapes=[
                pltpu.VMEM((2, TOP_K, tc, d), F32),
                pltpu.SemaphoreType.DMA((2, TOP_K)),
            ],
        ),
        out_shape=jax.ShapeDtypeStruct((n, d), F32),
        compiler_params=_params("arbitrary"),
        name="moe_combine_final_norm",
    )(pos0, pos1, y, x, gate.reshape(nb, 1, d), route_gates, g_final.reshape(1, d))


def _even_layer(x, mod, seq, g_mix, g_ffn, w_in, b_gates, conv_w, conv_b, head_g, pool_w, pool_scale,
                w_out, w1, w3, w2):
    n, d = x.shape
    nb = mod.shape[0]
    sh1, sc1, g1, sh2, sc2, g2 = jnp.split(mod, 6, axis=-1)
    dm = head_g.shape[0]
    dp = pool_scale.shape[0]
    ng = 2 * MLSTM_HEADS
    assert dm == dp and w_in.shape[1] == 4 * dm + ng + dp
    w_pool = w_in[:, 4 * dm + ng:]
    w_gate = jnp.pad(w_in[:, 4 * dm:4 * dm + ng], ((0, 0), (0, LANES - ng)))
    b_gate = jnp.pad(b_gates, (0, LANES - ng)).reshape(1, LANES)
    proj, gates = _nm_matmul(x, g_mix, sh1, sc1, w_in, F32, seq, head_cols=4 * dm,
                             side=(w_pool, w_gate, b_gate))
    gates_row = gates[:, :SUBLANES].T
    hm = _mlstm(proj, gates, gates_row, conv_w, conv_b, head_g, nb, seq)
    hp = _pool(proj, 4 * dm // dp, pool_w, pool_scale, seq)
    x = _mm_res([hm, hp], w_out, x, g1, seq)
    return _ffn(x, g_ffn, sh2, sc2, g2, w1, w3, w2, seq)


def _odd_layer(x, mod, seq, g_mix, g_ffn, g_final, w_qkv, w_o, w_router, w1, w3, w2):
    n, d = x.shape
    nb = mod.shape[0]
    sh1, sc1, g1, sh2, sc2, g2 = jnp.split(mod, 6, axis=-1)
    qkv = _nm_matmul(x, g_mix, sh1, sc1, w_qkv, BF16, seq)
    att = _sb_attention(qkv, nb, seq)
    x = _mm_res([att], w_o, x, g1, seq)
    h, info, gates, counts = _router(x, g_ffn, sh2, sc2, w_router, seq)
    tm = MOE_TILE_SUBS * MOE_SUB
    pos0, pos1, visits, zero_blocks, r_pad = _route_plan(info, counts, tm)
    hs = _dispatch(h, pos0, pos1, zero_blocks, r_pad, seq)
    short = tuple(a[:N_EXPERTS] for a in visits)
    y = lax.cond(visits[2][N_EXPERTS] == 0,
                 lambda rows: _moe(rows, short, w1, w3, w2, tm),
                 lambda rows: _moe(rows, visits, w1, w3, w2, tm), hs)
    return _combine(y, pos0, pos1, gates, x, g2, g_final, seq)


def kernel(x, c, ada_w, ada_b, norm_mix_g, norm_ffn_g, norm_final_g, ev_w_in, ev_b_gates, ev_conv_w,
           ev_conv_b, ev_head_g, ev_pool_w, ev_pool_scale, ev_w_out, ev_ffn_w1, ev_ffn_w3, ev_ffn_w2,
           od_w_qkv, od_w_o, od_router, od_moe_w1, od_moe_w3, od_moe_w2):
    nb, seq, d = x.shape
    assert ada_w.shape[0] == 2, "one even and one odd layer"
    mod = _adaln(c, ada_w, ada_b)
    xf = x.reshape(nb * seq, d)
    xf = _even_layer(xf, mod[0], seq, norm_mix_g[0], norm_ffn_g[0], ev_w_in[0], ev_b_gates[0],
                     ev_conv_w[0], ev_conv_b[0], ev_head_g[0], ev_pool_w[0], ev_pool_scale[0],
                     ev_w_out[0], ev_ffn_w1[0], ev_ffn_w3[0], ev_ffn_w2[0])
    out = _odd_layer(xf, mod[1], seq, norm_mix_g[1], norm_ffn_g[1], norm_final_g, od_w_qkv[0], od_w_o[0],
                     od_router[0], od_moe_w1[0], od_moe_w3[0], od_moe_w2[0])
    return out.reshape(nb, seq, d)
```

```python
import functools

import jax
import jax.numpy as jnp
from jax import lax
from jax.experimental import pallas as pl
from jax.experimental.pallas import tpu as pltpu

F32 = jnp.float32
BF16 = jnp.bfloat16
I32 = jnp.int32

EPS = 1e-6
MLSTM_HEADS = 4
CONV_K = 4
POOL_WINDOWS = (2, 4, 8, 16)
SB_HEADS = 16
N_EXPERTS = 8
TOP_K = 2

LANES = 128
SUBLANES = 8
VMEM_LIMIT = 56 * 1024 * 1024
PROJ_ROWS = 2048
NORM_ROWS = 512
MLSTM_TILE = 256
POOL_HALO = 16
ATTN_TILE = 256
ATTN_HEADS_PER_STEP = 4
ATTN_LOG_CUTOFF = -120.0
MOE_SUB = 128
MOE_TILE_SUBS = 18
MOE_LONG = 6
DMA_UNROLL = 8


def _split_bf16(a):
    hi = a.astype(BF16)
    return hi, (a - hi.astype(F32)).astype(BF16)


def _dot_f32(a, b):
    (ah, al), (bh, bl) = _split_bf16(a), _split_bf16(b)
    dot = functools.partial(jnp.dot, preferred_element_type=F32)
    return dot(ah, bh) + (dot(al, bh) + dot(ah, bl))


def _params(*sem):
    return pltpu.CompilerParams(dimension_semantics=sem, vmem_limit_bytes=VMEM_LIMIT)


def _tile(n, pref):
    t = min(n, pref)
    assert n % t == 0, (n, pref)
    return t


def _sigmoid(x):
    return 1.0 / (1.0 + jnp.exp(-x))


def _log_sigmoid(x):
    return jnp.minimum(x, 0.0) - jnp.log1p(jnp.exp(-jnp.abs(x)))


def _norm_mod(x, g, shift, scale):
    ms = jnp.mean(x * x, axis=-1, keepdims=True)
    y = x * lax.rsqrt(ms + EPS) * g
    return y * (1.0 + scale) + shift


def _adaln_kernel(cb_ref, w_ref, b_ref, o_ref):
    nb = cb_ref.shape[0]
    for j in range(w_ref.shape[1] // LANES):
        cols = pl.ds(j * LANES, LANES)
        w = w_ref[:, cols]
        for b in range(nb):
            o_ref[pl.ds(b, 1), cols] = jnp.sum(w * cb_ref[b], axis=0, keepdims=True) + b_ref[:, cols]


def _adaln(c, ada_w, ada_b):
    depth, d, n6 = ada_w.shape
    nb = c.shape[0]
    tn = _tile(n6, 1024)
    c_act = c * _sigmoid(c)
    cb = jnp.broadcast_to(c_act[:, :, None], (nb, d, LANES))
    return pl.pallas_call(
        _adaln_kernel,
        grid=(depth, n6 // tn),
        in_specs=[
            pl.BlockSpec((nb, d, LANES), lambda l, j: (0, 0, 0)),
            pl.BlockSpec((None, d, tn), lambda l, j: (l, 0, j)),
            pl.BlockSpec((None, 1, tn), lambda l, j: (l, 0, j)),
        ],
        out_specs=pl.BlockSpec((None, nb, tn), lambda l, j: (l, 0, j)),
        out_shape=jax.ShapeDtypeStruct((depth, nb, n6), F32),
        compiler_params=_params("parallel", "parallel"),
        name="adaln",
    )(cb, ada_w, ada_b.reshape(depth, 1, n6))


def _nm_matmul_kernel(x_ref, g_ref, sh_ref, sc_ref, w_ref, *rest, with_side, n_head):
    if with_side:
        wt_ref, ws_ref, bs_ref, o_ref, side_ref, h_ref = rest
    else:
        o_ref, h_ref = rest
    j = pl.program_id(1)

    @pl.when(j == 0)
    def _():
        tm = x_ref.shape[0]
        chunk = _tile(tm, NORM_ROWS)
        for c in range(tm // chunk):
            rows = pl.ds(c * chunk, chunk)
            h = _norm_mod(x_ref[rows, :], g_ref[...], sh_ref[...], sc_ref[...])
            h_ref[rows, :] = h.astype(BF16)
            if with_side:
                side_ref[rows, :] = _dot_f32(h, ws_ref[...]) + bs_ref[...]

    def project(weights_ref):
        o_ref[...] = jnp.dot(h_ref[...], weights_ref[...].astype(BF16),
                             preferred_element_type=F32).astype(o_ref.dtype)

    if with_side:
        pl.when(j < n_head)(lambda: project(w_ref))
        pl.when(j >= n_head)(lambda: project(wt_ref))
    else:
        project(w_ref)


def _nm_matmul(x, g, shift, scale, w, out_dtype, seq, head_cols=None, side=None):
    n, d = x.shape
    tm = _tile(seq, PROJ_ROWS)
    per_b = seq // tm
    nb = shift.shape[0]
    vec = lambda a: a.reshape(nb, 1, d)
    if side is None:
        nout = w.shape[1]
        tn = _tile(nout, 512)
        n_head = nout // tn
    else:
        nout = head_cols + side[0].shape[1]
        tn = _tile(head_cols, 256)
        assert side[0].shape[1] % tn == 0
        n_head = head_cols // tn
    in_specs = [
        pl.BlockSpec((tm, d), lambda i, j: (i, 0), pipeline_mode=pl.Buffered(1)),
        pl.BlockSpec((1, d), lambda i, j: (0, 0)),
        pl.BlockSpec((None, 1, d), lambda i, j: (i // per_b, 0, 0)),
        pl.BlockSpec((None, 1, d), lambda i, j: (i // per_b, 0, 0)),
        pl.BlockSpec((d, tn), lambda i, j: (0, jnp.minimum(j, n_head - 1))),
    ]
    args = [x, g.reshape(1, d), vec(shift), vec(scale), w]
    out_specs = pl.BlockSpec((tm, tn), lambda i, j: (i, j))
    out_shape = jax.ShapeDtypeStruct((n, nout), out_dtype)
    if side is not None:
        in_specs += [pl.BlockSpec((d, tn), lambda i, j: (0, jnp.maximum(j - n_head, 0))),
                     pl.BlockSpec((d, LANES), lambda i, j: (0, 0)),
                     pl.BlockSpec((1, LANES), lambda i, j: (0, 0))]
        args += list(side)
        out_specs = [out_specs, pl.BlockSpec((tm, LANES), lambda i, j: (i, 0))]
        out_shape = [out_shape, jax.ShapeDtypeStruct((n, LANES), F32)]
    return pl.pallas_call(
        functools.partial(_nm_matmul_kernel, with_side=side is not None, n_head=n_head),
        grid=(n // tm, nout // tn),
        in_specs=in_specs,
        out_specs=out_specs,
        out_shape=out_shape,
        scratch_shapes=[pltpu.VMEM((tm, d), BF16)],
        compiler_params=_params("parallel", "arbitrary"),
        name="norm_mod_matmul",
    )(*args)


def _mlstm_kernel(q_ref, qp_ref, k_ref, kp_ref, v_ref, o_ref, cw_ref, cb_ref, hg_ref, gc_ref, gr_ref,
                  out_ref, ext_ref, c_ref, n_ref, m_ref):
    chunk = pl.program_id(1)
    nh = c_ref.shape[0]
    L, dm = q_ref.shape
    dh = dm // nh

    @pl.when(chunk == 0)
    def _():
        c_ref[...] = jnp.zeros_like(c_ref)
        n_ref[...] = jnp.zeros_like(n_ref)
        m_ref[...] = jnp.zeros_like(m_ref)

    def conv_silu(cur_ref, prev_ref, off):
        ext_ref[pl.ds(0, SUBLANES), :] = jnp.where(chunk == 0, 0.0, prev_ref[...])
        ext_ref[pl.ds(SUBLANES, L), :] = cur_ref[...]
        y = cb_ref[:, pl.ds(off, dm)]
        for j in range(CONV_K):
            y = y + cw_ref[pl.ds(j, 1), pl.ds(off, dm)] * ext_ref[pl.ds(SUBLANES - CONV_K + 1 + j, L), :]
        return y * _sigmoid(y)

    q_all = conv_silu(q_ref, qp_ref, 0) * (dh ** -0.5)
    k_all = conv_silu(k_ref, kp_ref, dm)
    gc = gc_ref[...]
    row = lax.broadcasted_iota(I32, (L, L), 0)
    col = lax.broadcasted_iota(I32, (L, L), 1)
    causal = col <= row

    for head in range(nh):
        cols = pl.ds(head * dh, dh)
        q = q_all[:, head * dh:(head + 1) * dh]
        k = k_all[:, head * dh:(head + 1) * dh]
        qb = q.astype(BF16)
        kb = k.astype(BF16)
        vb = v_ref[:, cols].astype(BF16)

        i_col = gc[:, head:head + 1]
        f_col = gc[:, nh + head:nh + head + 1]
        i_row = gr_ref[pl.ds(head, 1), :]
        f_row = gr_ref[pl.ds(nh + head, 1), :]
        logf_col = _log_sigmoid(f_col)
        logf_row = _log_sigmoid(f_row)
        b_col = jnp.sum(jnp.where(causal, logf_row, 0.0), axis=1, keepdims=True)
        b_row = jnp.sum(jnp.where(row <= col, logf_col, 0.0), axis=0, keepdims=True)

        m_prev = m_ref[pl.ds(head, 1), pl.ds(0, 1)]
        n_prev = n_ref[pl.ds(head, 1), :]
        dmat = jnp.where(causal, b_col - b_row + i_row, -jnp.inf)
        inter = b_col + m_prev
        m_t = jnp.maximum(inter, jnp.max(dmat, axis=1, keepdims=True))
        w_inter = jnp.exp(inter - m_t)
        scores = lax.dot_general(qb, kb, (((1,), (1,)), ((), ())), preferred_element_type=F32)
        wmat = jnp.exp(dmat - m_t) * scores
        c_old = c_ref[head]
        num = (w_inter * jnp.dot(qb, c_old.astype(BF16), preferred_element_type=F32)
               + jnp.dot(wmat.astype(BF16), vb, preferred_element_type=F32))
        den = w_inter * jnp.sum(q * n_prev, axis=1, keepdims=True) + jnp.sum(wmat, axis=1, keepdims=True)
        hval = num / jnp.maximum(jnp.abs(den), jnp.exp(-m_t))

        g_end = jnp.sum(logf_row, axis=1, keepdims=True)
        wl = g_end - b_col + i_col
        m_new = jnp.maximum(g_end + m_prev, jnp.max(wl, axis=0, keepdims=True))
        decay = jnp.exp(g_end + m_prev - m_new)
        wk = jnp.exp(wl - m_new) * k
        c_ref[head] = decay * c_old + lax.dot_general(wk.astype(BF16), vb, (((0,), (0,)), ((), ())),
                                                      preferred_element_type=F32)
        n_ref[pl.ds(head, 1), :] = decay * n_prev + jnp.sum(wk, axis=0, keepdims=True)
        m_ref[pl.ds(head, 1), :] = jnp.broadcast_to(m_new, (1, m_ref.shape[1]))

        ms = jnp.mean(hval * hval, axis=-1, keepdims=True)
        hn = hval * lax.rsqrt(ms + EPS) * hg_ref[:, cols]
        out_ref[:, cols] = (hn * _sigmoid(o_ref[:, cols])).astype(out_ref.dtype)


def _mlstm(proj, gates_col, gates_row, conv_w, conv_b, head_g, nb, seq):
    n = proj.shape[0]
    nh = MLSTM_HEADS
    dm = head_g.shape[0]
    dh = dm // nh
    L = _tile(seq, MLSTM_TILE)
    nc = seq // L
    rb = L // SUBLANES

    assert 2 * nh <= SUBLANES

    def cur(off):
        return pl.BlockSpec((L, dm), lambda b, c: (b * nc + c, off))

    def prev(off):
        return pl.BlockSpec((SUBLANES, dm), lambda b, c: (jnp.maximum((b * nc + c) * rb - 1, 0), off))

    return pl.pallas_call(
        _mlstm_kernel,
        grid=(nb, nc),
        in_specs=[
            cur(0), prev(0), cur(1), prev(1), cur(2), cur(3),
            pl.BlockSpec((CONV_K, 2 * dm), lambda b, c: (0, 0)),
            pl.BlockSpec((1, 2 * dm), lambda b, c: (0, 0)),
            pl.BlockSpec((1, dm), lambda b, c: (0, 0)),
            pl.BlockSpec((L, LANES), lambda b, c: (b * nc + c, 0)),
            pl.BlockSpec((SUBLANES, L), lambda b, c: (0, b * nc + c)),
        ],
        out_specs=pl.BlockSpec((L, dm), lambda b, c: (b * nc + c, 0)),
        out_shape=jax.ShapeDtypeStruct((n, dm), BF16),
        scratch_shapes=[
            pltpu.VMEM((L + SUBLANES, dm), F32),
            pltpu.VMEM((nh, dh, dh), F32),
            pltpu.VMEM((SUBLANES, dh), F32),
            pltpu.VMEM((SUBLANES, LANES), F32),
        ],
        compiler_params=_params("parallel", "arbitrary"),
        name="mlstm",
    )(proj, proj, proj, proj, proj, proj, conv_w, conv_b.reshape(1, -1), head_g.reshape(1, dm),
      gates_col, gates_row)


def _pool_kernel(x_ref, xp_ref, pw_ref, ps_ref, o_ref, buf_ref, *, tiles_per_seq):
    ts, dp = x_ref.shape
    dg = dp // len(POOL_WINDOWS)
    t_in_seq = pl.program_id(0) % tiles_per_seq
    buf_ref[pl.ds(0, POOL_HALO), :] = jnp.where(t_in_seq == 0, 0.0, xp_ref[...])
    buf_ref[pl.ds(POOL_HALO, ts), :] = x_ref[...]
    pos = t_in_seq * ts + lax.broadcasted_iota(I32, (ts, 1), 0)
    for g, w in enumerate(POOL_WINDOWS):
        cols = pl.ds(g * dg, dg)
        xg = buf_ref[pl.ds(POOL_HALO, ts), cols]
        s = xg
        for j in range(1, w):
            s = s + buf_ref[pl.ds(POOL_HALO - j, ts), cols]
        cnt = jnp.minimum(pos + 1, w).astype(F32)
        pooled = s / cnt - xg
        mixed = jnp.dot(pooled.astype(BF16), pw_ref[g].astype(BF16), preferred_element_type=F32)
        o_ref[:, cols] = (mixed * ps_ref[:, cols]).astype(o_ref.dtype)


def _pool(proj, col_block, pool_w, pool_scale, seq):
    n = proj.shape[0]
    dp = pool_scale.shape[0]
    ts = _tile(seq, 512)
    hb = ts // POOL_HALO
    return pl.pallas_call(
        functools.partial(_pool_kernel, tiles_per_seq=seq // ts),
        grid=(n // ts,),
        in_specs=[
            pl.BlockSpec((ts, dp), lambda i: (i, col_block)),
            pl.BlockSpec((POOL_HALO, dp), lambda i: (jnp.maximum(i * hb - 1, 0), col_block)),
            pl.BlockSpec(pool_w.shape, lambda i: (0, 0, 0)),
            pl.BlockSpec((1, dp), lambda i: (0, 0)),
        ],
        out_specs=pl.BlockSpec((ts, dp), lambda i: (i, 0)),
        out_shape=jax.ShapeDtypeStruct((n, dp), BF16),
        scratch_shapes=[pltpu.VMEM((ts + POOL_HALO, dp), F32)],
        compiler_params=_params("parallel"),
        name="pool",
    )(proj, proj, pool_w, pool_scale.reshape(1, dp))


def _mm_res_kernel(*refs, n_a):
    a_refs, w_refs = refs[:n_a], refs[n_a:2 * n_a]
    x_ref, gate_ref, o_ref = refs[2 * n_a:]
    acc = jnp.dot(a_refs[0][...], w_refs[0][...].astype(BF16), preferred_element_type=F32)
    for a_ref, w_ref in zip(a_refs[1:], w_refs[1:]):
        acc = acc + jnp.dot(a_ref[...], w_ref[...].astype(BF16), preferred_element_type=F32)
    o_ref[...] = x_ref[...] + gate_ref[...] * acc


def _mm_res(a_list, w, x, gate, seq):
    n, d = x.shape
    n_a = len(a_list)
    ka = a_list[0].shape[1]
    assert all(a.shape[1] == ka for a in a_list) and w.shape[0] == n_a * ka
    tm = _tile(seq, PROJ_ROWS)
    tn = _tile(d, 512)
    per_b = seq // tm
    nb = gate.shape[0]
    in_specs = [pl.BlockSpec((tm, ka), lambda i, j: (i, 0)) for _ in a_list]
    in_specs += [pl.BlockSpec((ka, tn), functools.partial(lambda i, j, r: (r, j), r=r)) for r in range(n_a)]
    in_specs += [pl.BlockSpec((tm, tn), lambda i, j: (i, j)),
                 pl.BlockSpec((None, 1, tn), lambda i, j: (i // per_b, 0, j))]
    return pl.pallas_call(
        functools.partial(_mm_res_kernel, n_a=n_a),
        grid=(n // tm, d // tn),
        in_specs=in_specs,
        out_specs=pl.BlockSpec((tm, tn), lambda i, j: (i, j)),
        out_shape=jax.ShapeDtypeStruct((n, d), F32),
        compiler_params=_params("parallel", "parallel"),
        name="matmul_residual",
    )(*a_list, *([w] * n_a), x, gate.reshape(nb, 1, d))


def _ffn_kernel(x_ref, g_ref, sh_ref, sc_ref, gate_ref, w1_ref, w3_ref, w2_ref, o_ref, h_ref):
    f = pl.program_id(1)

    @pl.when(f == 0)
    def _():
        h_ref[...] = _norm_mod(x_ref[...], g_ref[...], sh_ref[...], sc_ref[...]).astype(BF16)
        o_ref[...] = jnp.zeros_like(o_ref)

    h = h_ref[...]
    a = jnp.dot(h, w1_ref[...].astype(BF16), preferred_element_type=F32)
    b = jnp.dot(h, w3_ref[...].astype(BF16), preferred_element_type=F32)
    act = (a * _sigmoid(a) * b).astype(BF16)
    o_ref[...] += jnp.dot(act, w2_ref[...].astype(BF16), preferred_element_type=F32)

    @pl.when(f == pl.num_programs(1) - 1)
    def _():
        o_ref[...] = x_ref[...] + gate_ref[...] * o_ref[...]


def _ffn(x, g, shift, scale, gate, w1, w3, w2, seq):
    n, d = x.shape
    dff = w1.shape[1]
    tm = _tile(seq, 1024)
    tf = _tile(dff, 256)
    per_b = seq // tm
    nb = gate.shape[0]
    vec = lambda a: a.reshape(nb, 1, d)
    bvec = pl.BlockSpec((None, 1, d), lambda i, f: (i // per_b, 0, 0))
    return pl.pallas_call(
        _ffn_kernel,
        grid=(n // tm, dff // tf),
        in_specs=[
            pl.BlockSpec((tm, d), lambda i, f: (i, 0), pipeline_mode=pl.Buffered(1)),
            pl.BlockSpec((1, d), lambda i, f: (0, 0)),
            bvec, bvec, bvec,
            pl.BlockSpec((d, tf), lambda i, f: (0, f)),
            pl.BlockSpec((d, tf), lambda i, f: (0, f)),
            pl.BlockSpec((tf, d), lambda i, f: (f, 0)),
        ],
        out_specs=pl.BlockSpec((tm, d), lambda i, f: (i, 0)),
        out_shape=jax.ShapeDtypeStruct((n, d), F32),
        scratch_shapes=[pltpu.VMEM((tm, d), BF16)],
        compiler_params=_params("parallel", "arbitrary"),
        name="swiglu_ffn",
    )(x, g.reshape(1, d), vec(shift), vec(scale), vec(gate), w1, w3, w2)


def _sb_attn_kernel(q_ref, k_ref, v_ref, o_ref):
    qi = pl.program_id(2)
    T = q_ref.shape[0]
    dh = q_ref.shape[1] // ATTN_HEADS_PER_STEP
    scale = dh ** -0.5
    row = lax.broadcasted_iota(I32, (T, T), 0)
    col = lax.broadcasted_iota(I32, (T, T), 1)
    later = (row > col).astype(BF16)

    def prepare(j, g, diagonal):
        start = pl.multiple_of(j * T, T)
        cols = pl.ds(g * dh, dh)
        kj = k_ref[pl.ds(start, T), cols]
        z = lax.dot_general(q_ref[:, cols], kj, (((1,), (1,)), ((), ())), preferred_element_type=F32) * scale
        sp = jnp.maximum(z, 0.0) + jnp.log(1.0 + jnp.exp(-jnp.abs(z)))
        lom = -sp
        if diagonal:
            strict = col < row
            lom = jnp.where(strict, lom, 0.0)
        hi = lom.astype(BF16)
        lo = (lom - hi.astype(F32)).astype(BF16)
        suffix = (jnp.dot(hi, later, preferred_element_type=F32)
                  + jnp.dot(lo, later, preferred_element_type=F32))
        base = z - sp + suffix
        if diagonal:
            base = jnp.where(strict, base, -jnp.inf)
        return base, jnp.sum(lom, axis=1, keepdims=True), v_ref[pl.ds(start, T), cols]

    def walk(tiles, accs, rems):
        new_accs, new_rems = [], []
        for g in range(ATTN_HEADS_PER_STEP):
            parts = [prepare(j, g, diagonal) for j, diagonal in tiles]
            acc, rem = accs[g], rems[g]
            for base, total, vj in parts:
                a = jnp.exp(base + rem)
                acc = acc + jnp.dot(a.astype(BF16), vj, preferred_element_type=F32)
                rem = rem + total
            new_accs.append(acc)
            new_rems.append(rem)
        return tuple(new_accs), tuple(new_rems)

    def alive(rems):
        top = rems[0]
        for r in rems[1:]:
            top = jnp.maximum(top, r)
        return jnp.max(top) > ATTN_LOG_CUTOFF

    def finish(accs):
        for g in range(ATTN_HEADS_PER_STEP):
            o_ref[:, pl.ds(g * dh, dh)] = accs[g].astype(o_ref.dtype)

    zeros = lambda w: tuple(jnp.zeros((T, w), F32) for _ in range(ATTN_HEADS_PER_STEP))

    @pl.when(qi == 0)
    def _():
        accs, _ = walk([(qi, True)], zeros(dh), zeros(1))
        finish(accs)

    @pl.when(qi > 0)
    def _():
        accs, rems = walk([(qi, True), (qi - 1, False)], zeros(dh), zeros(1))

        def cond(c):
            s, live, _, _ = c
            return jnp.logical_and(s < qi, live)

        def body(c):
            s, _, accs, rems = c
            accs, rems = walk([(qi - 1 - s, False)], accs, rems)
            return s + 1, alive(rems), accs, rems

        _, _, accs, _ = lax.while_loop(cond, body, (jnp.int32(1), alive(rems), accs, rems))
        finish(accs)


def _sb_attention(qkv, nb, seq):
    n, d3 = qkv.shape
    d = d3 // 3
    nh = SB_HEADS
    dh = d // nh
    T = _tile(seq, ATTN_TILE)
    nq = seq // T
    hg = ATTN_HEADS_PER_STEP
    ng = nh // hg
    return pl.pallas_call(
        _sb_attn_kernel,
        grid=(nb, ng, nq),
        in_specs=[
            pl.BlockSpec((T, hg * dh), lambda b, h, i: (b * nq + i, h)),
            pl.BlockSpec((seq, hg * dh), lambda b, h, i: (b, ng + h)),
            pl.BlockSpec((seq, hg * dh), lambda b, h, i: (b, 2 * ng + h)),
        ],
        out_specs=pl.BlockSpec((T, hg * dh), lambda b, h, i: (b * nq + i, h)),
        out_shape=jax.ShapeDtypeStruct((n, d), BF16),
        compiler_params=_params("parallel", "parallel", "parallel"),
        name="stick_breaking_attention",
    )(qkv, qkv, qkv)


def _router_kernel(x_ref, g_ref, sh_ref, sc_ref, wr_ref, h_ref, info_ref, gate_ref, cnt_ref, carry_ref):
    tm = x_ref.shape[0]

    @pl.when(pl.program_id(0) == 0)
    def _():
        carry_ref[...] = jnp.zeros_like(carry_ref)

    h = _norm_mod(x_ref[...], g_ref[...], sh_ref[...], sc_ref[...])
    h_ref[...] = h
    logits = _dot_f32(h, wr_ref[...])
    lane = lax.broadcasted_iota(I32, (tm, LANES), 1)
    lane_f = lane.astype(F32)
    lg = jnp.where(lane < N_EXPERTS, logits, -jnp.inf)
    v0 = jnp.max(lg, axis=1, keepdims=True)
    i0 = jnp.min(jnp.where(lg == v0, lane_f, float(LANES)), axis=1, keepdims=True)
    lg1 = jnp.where(lane_f == i0, -jnp.inf, lg)
    v1 = jnp.max(lg1, axis=1, keepdims=True)
    i1 = jnp.min(jnp.where(lg1 == v1, lane_f, float(LANES)), axis=1, keepdims=True)
    ex = jnp.exp(v1 - v0)
    g0 = 1.0 / (1.0 + ex)
    g1 = ex / (1.0 + ex)
    sel0 = lane_f == i0
    sel1 = lane_f == i1
    onehot = jnp.where(sel0 | sel1, 1.0, 0.0)
    row = lax.broadcasted_iota(I32, (tm, tm), 0)
    col = lax.broadcasted_iota(I32, (tm, tm), 1)
    before = (col < row).astype(BF16)
    earlier = jnp.dot(before, onehot.astype(BF16), preferred_element_type=F32) + carry_ref[...]
    r0 = jnp.sum(jnp.where(sel0, earlier, 0.0), axis=1, keepdims=True)
    r1 = jnp.sum(jnp.where(sel1, earlier, 0.0), axis=1, keepdims=True)
    carry_ref[...] += jnp.sum(onehot, axis=0, keepdims=True)
    info = jnp.where(lane == 0, i0, jnp.where(lane == 1, i1, jnp.where(lane == 2, r0, jnp.where(lane == 3, r1, 0.0))))
    info_ref[...] = info.astype(I32)
    gate_ref[...] = jnp.where(lane == 0, g0, jnp.where(lane == 1, g1, 0.0))
    cnt_ref[...] = carry_ref[...].astype(I32)


def _router(x, g, shift, scale, w_router, seq):
    n, d = x.shape
    tm = _tile(seq, 512)
    per_b = seq // tm
    nb = shift.shape[0]
    vec = lambda a: a.reshape(nb, 1, d)
    bvec = pl.BlockSpec((None, 1, d), lambda i: (i // per_b, 0, 0))
    wr = jnp.pad(w_router, ((0, 0), (0, LANES - w_router.shape[1])))
    return pl.pallas_call(
        _router_kernel,
        grid=(n // tm,),
        in_specs=[
            pl.BlockSpec((tm, d), lambda i: (i, 0)),
            pl.BlockSpec((1, d), lambda i: (0, 0)),
            bvec, bvec,
            pl.BlockSpec((d, LANES), lambda i: (0, 0)),
        ],
        out_specs=[
            pl.BlockSpec((tm, d), lambda i: (i, 0)),
            pl.BlockSpec((tm, LANES), lambda i: (i, 0)),
            pl.BlockSpec((tm, LANES), lambda i: (i, 0)),
            pl.BlockSpec((1, LANES), lambda i: (0, 0)),
        ],
        out_shape=[
            jax.ShapeDtypeStruct((n, d), F32),
            jax.ShapeDtypeStruct((n, LANES), I32),
            jax.ShapeDtypeStruct((n, LANES), F32),
            jax.ShapeDtypeStruct((1, LANES), I32),
        ],
        scratch_shapes=[pltpu.VMEM((1, LANES), F32)],
        compiler_params=_params("arbitrary"),
        name="router",
    )(x, g.reshape(1, d), vec(shift), vec(scale), wr)


def _route_plan(info, counts, tm):
    n = info.shape[0]
    ne = N_EXPERTS
    e0, e1, r0, r1 = info[:, 0], info[:, 1], info[:, 2], info[:, 3]
    cnt = counts[0, :ne]
    padded = (cnt + MOE_SUB - 1) // MOE_SUB * MOE_SUB
    ends = jnp.cumsum(padded)
    starts = ends - padded
    pos0 = (starts[e0] + r0).astype(I32)
    pos1 = (starts[e1] + r1).astype(I32)
    r_pad = TOP_K * n + ne * MOE_SUB
    nv = -(-r_pad // tm) + ne
    ntiles = (padded + tm - 1) // tm
    v_end = jnp.cumsum(ntiles)
    v_first = v_end - ntiles
    total = v_end[-1]
    vid = jnp.arange(nv, dtype=I32)
    vc = jnp.minimum(vid, total - 1)
    grp = jnp.sum((v_end[None, :] <= vc[:, None]).astype(I32), axis=1)
    k = vc - v_first[grp]
    v_start = (starts[grp] + k * tm).astype(I32)
    v_nsub = jnp.where(vid < total, jnp.minimum(tm, padded[grp] - k * tm) // MOE_SUB, 0).astype(I32)
    tail = ends[-1] + MOE_SUB * jnp.arange(ne, dtype=I32)
    z_start = jnp.concatenate([jnp.maximum(ends - MOE_SUB, 0), tail]).astype(I32)
    z_on = jnp.concatenate([padded > 0, tail < r_pad]).astype(I32)
    return pos0, pos1, (grp.astype(I32), v_start, v_nsub), (z_start, z_on), r_pad


def _dispatch_kernel(p0_ref, p1_ref, zs_ref, zn_ref, h_ref, hs_hbm, zero_ref, sem, zsem):
    i = pl.program_id(0)
    td = h_ref.shape[0]
    sub = zero_ref.shape[0]

    def zero_copy(b):
        dst = hs_hbm.at[pl.ds(pl.multiple_of(zs_ref[b], sub), sub), :]
        return pltpu.make_async_copy(zero_ref, dst, zsem)

    @pl.when(i == 0)
    def _():
        zero_ref[...] = jnp.zeros_like(zero_ref)
        for b in range(zs_ref.shape[0]):
            pl.when(zn_ref[b] == 1)(lambda: zero_copy(b).start())
        for b in range(zs_ref.shape[0]):
            pl.when(zn_ref[b] == 1)(lambda: zero_copy(b).wait())

    def issue(r, c):
        t = i * td + r
        src = h_ref.at[pl.ds(r, 1), :]
        pltpu.make_async_copy(src, hs_hbm.at[pl.ds(p0_ref[t], 1), :], sem.at[0]).start()
        pltpu.make_async_copy(src, hs_hbm.at[pl.ds(p1_ref[t], 1), :], sem.at[1]).start()
        return c
    lax.fori_loop(0, td, issue, 0, unroll=DMA_UNROLL)
    for stream in range(TOP_K):
        pltpu.make_async_copy(h_ref, hs_hbm.at[pl.ds(0, td), :], sem.at[stream]).wait()


def _dispatch(h, pos0, pos1, zero_blocks, r_pad, seq):
    n, d = h.shape
    td = _tile(seq, 512)
    z_start, z_on = zero_blocks
    return pl.pallas_call(
        _dispatch_kernel,
        grid_spec=pltpu.PrefetchScalarGridSpec(
            num_scalar_prefetch=4,
            grid=(n // td,),
            in_specs=[pl.BlockSpec((td, d), lambda i, p0, p1, zs, zn: (i, 0))],
            out_specs=pl.BlockSpec(memory_space=pl.ANY),
            scratch_shapes=[
                pltpu.VMEM((MOE_SUB, d), F32),
                pltpu.SemaphoreType.DMA((TOP_K,)),
                pltpu.SemaphoreType.DMA(()),
            ],
        ),
        out_shape=jax.ShapeDtypeStruct((r_pad, d), F32),
        compiler_params=_params("arbitrary"),
        name="moe_dispatch",
    )(pos0, pos1, z_start, z_on, h)


def _moe_kernel(vg_ref, vs_ref, vn_ref, hs_hbm, w1_ref, w3_ref, w2_ref, y_hbm,
                stage_ref, hb_ref, acc_ref, wb1_ref, wb3_ref, wb2_ref, sem_in, sem_out):
    v = pl.program_id(0)
    f = pl.program_id(1)
    sub = stage_ref.shape[1]
    start = vs_ref[v]
    nsub = vn_ref[v]

    def rows(sb):
        return pl.ds(pl.multiple_of(sb * sub, sub), sub)

    def hbm_rows(row0, sb):
        return pl.ds(pl.multiple_of(row0 + sb * sub, sub), sub)

    def in_copy(sb, slot):
        return pltpu.make_async_copy(hs_hbm.at[hbm_rows(start, sb), :], stage_ref.at[slot], sem_in.at[slot])

    def out_copy(row0, sb):
        return pltpu.make_async_copy(acc_ref.at[rows(sb), :], y_hbm.at[hbm_rows(row0, sb), :], sem_out)

    def each_block(count, fn):
        def body(sb, c):
            fn(sb)
            return c
        lax.fori_loop(0, count, body, 0)

    @pl.when(f == 0)
    def _():
        @pl.when(nsub > 0)
        def _():
            in_copy(0, 0).start()

        def load(sb):
            slot = sb % 2

            @pl.when(sb + 1 < nsub)
            def _():
                in_copy(sb + 1, 1 - slot).start()
            in_copy(sb, slot).wait()
            hb_ref[rows(sb), :] = stage_ref[slot].astype(BF16)
        each_block(nsub, load)

        prev = jnp.maximum(v - 1, 0)
        n_prev = jnp.where(v > 0, vn_ref[prev], 0)
        each_block(n_prev, lambda sb: out_copy(vs_ref[prev], sb).wait())

        def clear(sb):
            acc_ref[rows(sb), :] = jnp.zeros((sub, acc_ref.shape[1]), F32)
        each_block(nsub, clear)

    @pl.when(nsub > 0)
    def _():
        def cast_weights():
            w1b = w1_ref[...].astype(BF16)
            w3b = w3_ref[...].astype(BF16)
            w2b = w2_ref[...].astype(BF16)
            wb1_ref[...] = w1b
            wb3_ref[...] = w3b
            wb2_ref[...] = w2b
            return w1b, w3b, w2b

        def blocks(sb, count, first=False):
            w1b, w3b, w2b = cast_weights() if first else (wb1_ref[...], wb3_ref[...], wb2_ref[...])
            r = pl.ds(pl.multiple_of(sb * sub, sub), count * sub)
            hs = hb_ref[r, :]
            a = jnp.dot(hs, w1b, preferred_element_type=F32)
            b = jnp.dot(hs, w3b, preferred_element_type=F32)
            act = (a * _sigmoid(a) * b).astype(BF16)
            acc_ref[r, :] += jnp.dot(act, w2b, preferred_element_type=F32)

        n_long = nsub // MOE_LONG
        n_rest = nsub - n_long * MOE_LONG
        pl.when(n_long > 0)(lambda: blocks(0, MOE_LONG, first=True))
        each_block(n_long - 1, lambda i: blocks((i + 1) * MOE_LONG, MOE_LONG))
        for rest in range(1, MOE_LONG):
            pl.when(jnp.logical_and(n_rest == rest, n_long > 0))(lambda: blocks(n_long * MOE_LONG, rest))
            pl.when(jnp.logical_and(n_rest == rest, n_long == 0))(lambda: blocks(0, rest, first=True))

    @pl.when(f == pl.num_programs(1) - 1)
    def _():
        each_block(nsub, lambda sb: out_copy(start, sb).start())

        @pl.when(v == pl.num_programs(0) - 1)
        def _():
            each_block(nsub, lambda sb: out_copy(start, sb).wait())


def _moe(hs, visits, w1, w3, w2, tm):
    r_pad, d = hs.shape
    _, _, dff = w1.shape
    v_group, v_start, v_nsub = visits
    nv = v_group.shape[0]
    tf = _tile(dff, 256)
    nf = dff // tf

    def f_eff(v, f, vn):
        return jnp.where(vn[v] > 0, f, nf - 1)

    return pl.pallas_call(
        _moe_kernel,
        grid_spec=pltpu.PrefetchScalarGridSpec(
            num_scalar_prefetch=3,
            grid=(nv, nf),
            in_specs=[
                pl.BlockSpec(memory_space=pl.ANY),
                pl.BlockSpec((None, d, tf), lambda v, f, vg, vs, vn: (vg[v], 0, f_eff(v, f, vn))),
                pl.BlockSpec((None, d, tf), lambda v, f, vg, vs, vn: (vg[v], 0, f_eff(v, f, vn))),
                pl.BlockSpec((None, tf, d), lambda v, f, vg, vs, vn: (vg[v], f_eff(v, f, vn), 0)),
            ],
            out_specs=pl.BlockSpec(memory_space=pl.ANY),
            scratch_shapes=[
                pltpu.VMEM((2, MOE_SUB, d), F32),
                pltpu.VMEM((tm, d), BF16),
                pltpu.VMEM((tm, d), F32),
                pltpu.VMEM((d, tf), BF16),
                pltpu.VMEM((d, tf), BF16),
                pltpu.VMEM((tf, d), BF16),
                pltpu.SemaphoreType.DMA((2,)),
                pltpu.SemaphoreType.DMA(()),
            ],
        ),
        out_shape=jax.ShapeDtypeStruct((r_pad, d), F32),
        input_output_aliases={3: 0},
        compiler_params=_params("arbitrary", "arbitrary"),
        name="moe_experts",
    )(v_group, v_start, v_nsub, hs, w1, w3, w2)


def _combine_kernel(p0_ref, p1_ref, y_hbm, x_ref, gate_ref, rg_ref, gf_ref, o_ref, buf_ref, sem):
    i = pl.program_id(0)
    tc = x_ref.shape[0]
    slot = i % 2

    def gather(tile, into):
        def issue(r, c):
            t = tile * tc + r
            for k, p_ref in enumerate((p0_ref, p1_ref)):
                pltpu.make_async_copy(y_hbm.at[pl.ds(p_ref[t], 1), :],
                                      buf_ref.at[into, k, pl.ds(r, 1), :], sem.at[into, k]).start()
            return c
        lax.fori_loop(0, tc, issue, 0, unroll=DMA_UNROLL)

    @pl.when(i == 0)
    def _():
        gather(0, 0)

    @pl.when(i + 1 < pl.num_programs(0))
    def _():
        gather(i + 1, 1 - slot)

    for k in range(TOP_K):
        pltpu.make_async_copy(y_hbm.at[pl.ds(0, tc), :], buf_ref.at[slot, k], sem.at[slot, k]).wait()

    rg = rg_ref[...]
    moe = rg[:, 0:1] * buf_ref[slot, 0] + rg[:, 1:2] * buf_ref[slot, 1]
    xn = x_ref[...] + gate_ref[...] * moe
    ms = jnp.mean(xn * xn, axis=-1, keepdims=True)
    o_ref[...] = xn * lax.rsqrt(ms + EPS) * gf_ref[...]


def _combine(y, pos0, pos1, route_gates, x, gate, g_final, seq):
    n, d = x.shape
    tc = _tile(seq, 256)
    per_b = seq // tc
    nb = gate.shape[0]
    return pl.pallas_call(
        _combine_kernel,
        grid_spec=pltpu.PrefetchScalarGridSpec(
            num_scalar_prefetch=2,
            grid=(n // tc,),
            in_specs=[
                pl.BlockSpec(memory_space=pl.ANY),
                pl.BlockSpec((tc, d), lambda i, p0, p1: (i, 0)),
                pl.BlockSpec((None, 1, d), lambda i, p0, p1: (i // per_b, 0, 0)),
                pl.BlockSpec((tc, LANES), lambda i, p0, p1: (i, 0)),
                pl.BlockSpec((1, d), lambda i, p0, p1: (0, 0)),
            ],
            out_specs=pl.BlockSpec((tc, d), lambda i, p0, p1: (i, 0)),
            scratch_shapes=[
                pltpu.VMEM((2, TOP_K, tc, d), F32),
                pltpu.SemaphoreType.DMA((2, TOP_K)),
            ],
        ),
        out_shape=jax.ShapeDtypeStruct((n, d), F32),
        compiler_params=_params("arbitrary"),
        name="moe_combine_final_norm",
    )(pos0, pos1, y, x, gate.reshape(nb, 1, d), route_gates, g_final.reshape(1, d))


def _even_layer(x, mod, seq, g_mix, g_ffn, w_in, b_gates, conv_w, conv_b, head_g, pool_w, pool_scale,
                w_out, w1, w3, w2):
    n, d = x.shape
    nb = mod.shape[0]
    sh1, sc1, g1, sh2, sc2, g2 = jnp.split(mod, 6, axis=-1)
    dm = head_g.shape[0]
    dp = pool_scale.shape[0]
    ng = 2 * MLSTM_HEADS
    assert dm == dp and w_in.shape[1] == 4 * dm + ng + dp
    w_pool = w_in[:, 4 * dm + ng:]
    w_gate = jnp.pad(w_in[:, 4 * dm:4 * dm + ng], ((0, 0), (0, LANES - ng)))
    b_gate = jnp.pad(b_gates, (0, LANES - ng)).reshape(1, LANES)
    proj, gates = _nm_matmul(x, g_mix, sh1, sc1, w_in, F32, seq, head_cols=4 * dm,
                             side=(w_pool, w_gate, b_gate))
    gates_row = gates[:, :SUBLANES].T
    hm = _mlstm(proj, gates, gates_row, conv_w, conv_b, head_g, nb, seq)
    hp = _pool(proj, 4 * dm // dp, pool_w, pool_scale, seq)
    x = _mm_res([hm, hp], w_out, x, g1, seq)
    return _ffn(x, g_ffn, sh2, sc2, g2, w1, w3, w2, seq)


def _odd_layer(x, mod, seq, g_mix, g_ffn, g_final, w_qkv, w_o, w_router, w1, w3, w2):
    n, d = x.shape
    nb = mod.shape[0]
    sh1, sc1, g1, sh2, sc2, g2 = jnp.split(mod, 6, axis=-1)
    qkv = _nm_matmul(x, g_mix, sh1, sc1, w_qkv, BF16, seq)
    att = _sb_attention(qkv, nb, seq)
    x = _mm_res([att], w_o, x, g1, seq)
    h, info, gates, counts = _router(x, g_ffn, sh2, sc2, w_router, seq)
    tm = MOE_TILE_SUBS * MOE_SUB
    pos0, pos1, visits, zero_blocks, r_pad = _route_plan(info, counts, tm)
    hs = _dispatch(h, pos0, pos1, zero_blocks, r_pad, seq)
    short = tuple(a[:N_EXPERTS] for a in visits)
    y = lax.cond(visits[2][N_EXPERTS] == 0,
                 lambda rows: _moe(rows, short, w1, w3, w2, tm),
                 lambda rows: _moe(rows, visits, w1, w3, w2, tm), hs)
    return _combine(y, pos0, pos1, gates, x, g2, g_final, seq)


def kernel(x, c, ada_w, ada_b, norm_mix_g, norm_ffn_g, norm_final_g, ev_w_in, ev_b_gates, ev_conv_w,
           ev_conv_b, ev_head_g, ev_pool_w, ev_pool_scale, ev_w_out, ev_ffn_w1, ev_ffn_w3, ev_ffn_w2,
           od_w_qkv, od_w_o, od_router, od_moe_w1, od_moe_w3, od_moe_w2):
    nb, seq, d = x.shape
    assert ada_w.shape[0] == 2, "one even and one odd layer"
    mod = _adaln(c, ada_w, ada_b)
    xf = x.reshape(nb * seq, d)
    xf = _even_layer(xf, mod[0], seq, norm_mix_g[0], norm_ffn_g[0], ev_w_in[0], ev_b_gates[0],
                     ev_conv_w[0], ev_conv_b[0], ev_head_g[0], ev_pool_w[0], ev_pool_scale[0],
                     ev_w_out[0], ev_ffn_w1[0], ev_ffn_w3[0], ev_ffn_w2[0])
    out = _odd_layer(xf, mod[1], seq, norm_mix_g[1], norm_ffn_g[1], norm_final_g, od_w_qkv[0], od_w_o[0],
                     od_router[0], od_moe_w1[0], od_moe_w3[0], od_moe_w2[0])
    return out.reshape(nb, seq, d)
```

```python
import functools

import jax
import jax.numpy as jnp
from jax import lax
from jax.experimental import pallas as pl
from jax.experimental.pallas import tpu as pltpu

F32 = jnp.float32
BF16 = jnp.bfloat16
I32 = jnp.int32

EPS = 1e-6
MLSTM_HEADS = 4
CONV_K = 4
POOL_WINDOWS = (2, 4, 8, 16)
SB_HEADS = 16
N_EXPERTS = 8
TOP_K = 2

LANES = 128
SUBLANES = 8
VMEM_LIMIT = 56 * 1024 * 1024
PROJ_ROWS = 2048
NORM_ROWS = 512
MLSTM_TILE = 256
POOL_HALO = 16
ATTN_TILE = 256
ATTN_HEADS_PER_STEP = 8
ATTN_LOG_CUTOFF = -120.0
MOE_SUB = 128
MOE_TILE_SUBS = 18
MOE_LONG = 6
DMA_UNROLL = 8


def _split_bf16(a):
    hi = a.astype(BF16)
    return hi, (a - hi.astype(F32)).astype(BF16)


def _dot_f32(a, b):
    (ah, al), (bh, bl) = _split_bf16(a), _split_bf16(b)
    dot = functools.partial(jnp.dot, preferred_element_type=F32)
    return dot(ah, bh) + (dot(al, bh) + dot(ah, bl))


def _params(*sem):
    return pltpu.CompilerParams(dimension_semantics=sem, vmem_limit_bytes=VMEM_LIMIT)


def _tile(n, pref):
    t = min(n, pref)
    assert n % t == 0, (n, pref)
    return t


def _sigmoid(x):
    return 1.0 / (1.0 + jnp.exp(-x))


def _log_sigmoid(x):
    return jnp.minimum(x, 0.0) - jnp.log1p(jnp.exp(-jnp.abs(x)))


def _norm_mod(x, g, shift, scale):
    ms = jnp.mean(x * x, axis=-1, keepdims=True)
    y = x * lax.rsqrt(ms + EPS) * g
    return y * (1.0 + scale) + shift


def _adaln_kernel(cb_ref, w_ref, b_ref, o_ref):
    nb = cb_ref.shape[0]
    for j in range(w_ref.shape[1] // LANES):
        cols = pl.ds(j * LANES, LANES)
        w = w_ref[:, cols]
        for b in range(nb):
            o_ref[pl.ds(b, 1), cols] = jnp.sum(w * cb_ref[b], axis=0, keepdims=True) + b_ref[:, cols]


def _adaln(c, ada_w, ada_b):
    depth, d, n6 = ada_w.shape
    nb = c.shape[0]
    tn = _tile(n6, 1024)
    c_act = c * _sigmoid(c)
    cb = jnp.broadcast_to(c_act[:, :, None], (nb, d, LANES))
    return pl.pallas_call(
        _adaln_kernel,
        grid=(depth, n6 // tn),
        in_specs=[
            pl.BlockSpec((nb, d, LANES), lambda l, j: (0, 0, 0)),
            pl.BlockSpec((None, d, tn), lambda l, j: (l, 0, j)),
            pl.BlockSpec((None, 1, tn), lambda l, j: (l, 0, j)),
        ],
        out_specs=pl.BlockSpec((None, nb, tn), lambda l, j: (l, 0, j)),
        out_shape=jax.ShapeDtypeStruct((depth, nb, n6), F32),
        compiler_params=_params("parallel", "parallel"),
        name="adaln",
    )(cb, ada_w, ada_b.reshape(depth, 1, n6))


def _nm_matmul_kernel(x_hbm, g_ref, sh_ref, sc_ref, w_ref, *rest, with_side, n_head):
    if with_side:
        wt_ref, ws_ref, bs_ref, o_ref, side_ref, h_ref, xbuf_ref, sem = rest
    else:
        o_ref, h_ref, xbuf_ref, sem = rest
    i = pl.program_id(0)
    j = pl.program_id(1)

    @pl.when(j == 0)
    def _():
        tm = h_ref.shape[0]
        chunk = xbuf_ref.shape[1]

        def x_copy(c):
            src = x_hbm.at[pl.ds(pl.multiple_of(i * tm + c * chunk, chunk), chunk), :]
            return pltpu.make_async_copy(src, xbuf_ref.at[c % 2], sem.at[c % 2])

        x_copy(0).start()
        for c in range(tm // chunk):
            if (c + 1) * chunk < tm:
                x_copy(c + 1).start()
            x_copy(c).wait()
            rows = pl.ds(c * chunk, chunk)
            h = _norm_mod(xbuf_ref[c % 2], g_ref[...], sh_ref[...], sc_ref[...])
            h_ref[rows, :] = h.astype(BF16)
            if with_side:
                side_ref[rows, :] = _dot_f32(h, ws_ref[...]) + bs_ref[...]

    def project(weights_ref):
        o_ref[...] = jnp.dot(h_ref[...], weights_ref[...].astype(BF16),
                             preferred_element_type=F32).astype(o_ref.dtype)

    if with_side:
        pl.when(j < n_head)(lambda: project(w_ref))
        pl.when(j >= n_head)(lambda: project(wt_ref))
    else:
        project(w_ref)


def _nm_matmul(x, g, shift, scale, w, out_dtype, seq, head_cols=None, side=None):
    n, d = x.shape
    tm = _tile(seq, PROJ_ROWS)
    per_b = seq // tm
    nb = shift.shape[0]
    vec = lambda a: a.reshape(nb, 1, d)
    if side is None:
        nout = w.shape[1]
        tn = _tile(nout, 512)
        n_head = nout // tn
    else:
        nout = head_cols + side[0].shape[1]
        tn = _tile(head_cols, 512)
        assert side[0].shape[1] % tn == 0
        n_head = head_cols // tn
    in_specs = [
        pl.BlockSpec(memory_space=pl.ANY),
        pl.BlockSpec((1, d), lambda i, j: (0, 0)),
        pl.BlockSpec((None, 1, d), lambda i, j: (i // per_b, 0, 0)),
        pl.BlockSpec((None, 1, d), lambda i, j: (i // per_b, 0, 0)),
        pl.BlockSpec((d, tn), lambda i, j: (0, jnp.minimum(j, n_head - 1))),
    ]
    args = [x, g.reshape(1, d), vec(shift), vec(scale), w]
    out_specs = pl.BlockSpec((tm, tn), lambda i, j: (i, j))
    out_shape = jax.ShapeDtypeStruct((n, nout), out_dtype)
    if side is not None:
        in_specs += [pl.BlockSpec((d, tn), lambda i, j: (0, jnp.maximum(j - n_head, 0))),
                     pl.BlockSpec((d, LANES), lambda i, j: (0, 0)),
                     pl.BlockSpec((1, LANES), lambda i, j: (0, 0))]
        args += list(side)
        out_specs = [out_specs, pl.BlockSpec((tm, LANES), lambda i, j: (i, 0))]
        out_shape = [out_shape, jax.ShapeDtypeStruct((n, LANES), F32)]
    return pl.pallas_call(
        functools.partial(_nm_matmul_kernel, with_side=side is not None, n_head=n_head),
        grid=(n // tm, nout // tn),
        in_specs=in_specs,
        out_specs=out_specs,
        out_shape=out_shape,
        scratch_shapes=[
            pltpu.VMEM((tm, d), BF16),
            pltpu.VMEM((2, _tile(tm, NORM_ROWS), d), F32),
            pltpu.SemaphoreType.DMA((2,)),
        ],
        compiler_params=_params("parallel", "arbitrary"),
        name="norm_mod_matmul",
    )(*args)


def _mlstm_kernel(q_ref, qp_ref, k_ref, kp_ref, v_ref, o_ref, cw_ref, cb_ref, hg_ref, gc_ref, gr_ref,
                  out_ref, ext_ref, c_ref, n_ref, m_ref):
    chunk = pl.program_id(1)
    nh = c_ref.shape[0]
    L, dm = q_ref.shape
    dh = dm // nh

    @pl.when(chunk == 0)
    def _():
        c_ref[...] = jnp.zeros_like(c_ref)
        n_ref[...] = jnp.zeros_like(n_ref)
        m_ref[...] = jnp.zeros_like(m_ref)

    def conv_silu(cur_ref, prev_ref, off):
        ext_ref[pl.ds(0, SUBLANES), :] = jnp.where(chunk == 0, 0.0, prev_ref[...])
        ext_ref[pl.ds(SUBLANES, L), :] = cur_ref[...]
        y = cb_ref[:, pl.ds(off, dm)]
        for j in range(CONV_K):
            y = y + cw_ref[pl.ds(j, 1), pl.ds(off, dm)] * ext_ref[pl.ds(SUBLANES - CONV_K + 1 + j, L), :]
        return y * _sigmoid(y)

    q_all = conv_silu(q_ref, qp_ref, 0) * (dh ** -0.5)
    k_all = conv_silu(k_ref, kp_ref, dm)
    gc = gc_ref[...]
    row = lax.broadcasted_iota(I32, (L, L), 0)
    col = lax.broadcasted_iota(I32, (L, L), 1)
    causal = col <= row

    for head in range(nh):
        cols = pl.ds(head * dh, dh)
        q = q_all[:, head * dh:(head + 1) * dh]
        k = k_all[:, head * dh:(head + 1) * dh]
        qb = q.astype(BF16)
        kb = k.astype(BF16)
        vb = v_ref[:, cols].astype(BF16)

        i_col = gc[:, head:head + 1]
        f_col = gc[:, nh + head:nh + head + 1]
        i_row = gr_ref[pl.ds(head, 1), :]
        f_row = gr_ref[pl.ds(nh + head, 1), :]
        logf_col = _log_sigmoid(f_col)
        logf_row = _log_sigmoid(f_row)
        b_col = jnp.sum(jnp.where(causal, logf_row, 0.0), axis=1, keepdims=True)
        b_row = jnp.sum(jnp.where(row <= col, logf_col, 0.0), axis=0, keepdims=True)

        m_prev = m_ref[pl.ds(head, 1), pl.ds(0, 1)]
        n_prev = n_ref[pl.ds(head, 1), :]
        dmat = jnp.where(causal, b_col - b_row + i_row, -jnp.inf)
        inter = b_col + m_prev
        m_t = jnp.maximum(inter, jnp.max(dmat, axis=1, keepdims=True))
        w_inter = jnp.exp(inter - m_t)
        scores = lax.dot_general(qb, kb, (((1,), (1,)), ((), ())), preferred_element_type=F32)
        wmat = jnp.exp(dmat - m_t) * scores
        c_old = c_ref[head]
        num = (w_inter * jnp.dot(qb, c_old.astype(BF16), preferred_element_type=F32)
               + jnp.dot(wmat.astype(BF16), vb, preferred_element_type=F32))
        den = w_inter * jnp.sum(q * n_prev, axis=1, keepdims=True) + jnp.sum(wmat, axis=1, keepdims=True)
        hval = num / jnp.maximum(jnp.abs(den), jnp.exp(-m_t))

        g_end = jnp.sum(logf_row, axis=1, keepdims=True)
        wl = g_end - b_col + i_col
        m_new = jnp.maximum(g_end + m_prev, jnp.max(wl, axis=0, keepdims=True))
        decay = jnp.exp(g_end + m_prev - m_new)
        wk = jnp.exp(wl - m_new) * k
        c_ref[head] = decay * c_old + lax.dot_general(wk.astype(BF16), vb, (((0,), (0,)), ((), ())),
                                                      preferred_element_type=F32)
        n_ref[pl.ds(head, 1), :] = decay * n_prev + jnp.sum(wk, axis=0, keepdims=True)
        m_ref[pl.ds(head, 1), :] = jnp.broadcast_to(m_new, (1, m_ref.shape[1]))

        ms = jnp.mean(hval * hval, axis=-1, keepdims=True)
        hn = hval * lax.rsqrt(ms + EPS) * hg_ref[:, cols]
        out_ref[:, cols] = (hn * _sigmoid(o_ref[:, cols])).astype(out_ref.dtype)


def _mlstm(proj, gates_col, gates_row, conv_w, conv_b, head_g, nb, seq):
    n = proj.shape[0]
    nh = MLSTM_HEADS
    dm = head_g.shape[0]
    dh = dm // nh
    L = _tile(seq, MLSTM_TILE)
    nc = seq // L
    rb = L // SUBLANES

    assert 2 * nh <= SUBLANES

    def cur(off):
        return pl.BlockSpec((L, dm), lambda b, c: (b * nc + c, off))

    def prev(off):
        return pl.BlockSpec((SUBLANES, dm), lambda b, c: (jnp.maximum((b * nc + c) * rb - 1, 0), off))

    return pl.pallas_call(
        _mlstm_kernel,
        grid=(nb, nc),
        in_specs=[
            cur(0), prev(0), cur(1), prev(1), cur(2), cur(3),
            pl.BlockSpec((CONV_K, 2 * dm), lambda b, c: (0, 0)),
            pl.BlockSpec((1, 2 * dm), lambda b, c: (0, 0)),
            pl.BlockSpec((1, dm), lambda b, c: (0, 0)),
            pl.BlockSpec((L, LANES), lambda b, c: (b * nc + c, 0)),
            pl.BlockSpec((SUBLANES, L), lambda b, c: (0, b * nc + c)),
        ],
        out_specs=pl.BlockSpec((L, dm), lambda b, c: (b * nc + c, 0)),
        out_shape=jax.ShapeDtypeStruct((n, dm), BF16),
        scratch_shapes=[
            pltpu.VMEM((L + SUBLANES, dm), F32),
            pltpu.VMEM((nh, dh, dh), F32),
            pltpu.VMEM((SUBLANES, dh), F32),
            pltpu.VMEM((SUBLANES, LANES), F32),
        ],
        compiler_params=_params("parallel", "arbitrary"),
        name="mlstm",
    )(proj, proj, proj, proj, proj, proj, conv_w, conv_b.reshape(1, -1), head_g.reshape(1, dm),
      gates_col, gates_row)


def _pool_kernel(x_ref, xp_ref, pw_ref, ps_ref, o_ref, buf_ref, *, tiles_per_seq):
    ts, dp = x_ref.shape
    dg = dp // len(POOL_WINDOWS)
    t_in_seq = pl.program_id(0) % tiles_per_seq
    buf_ref[pl.ds(0, POOL_HALO), :] = jnp.where(t_in_seq == 0, 0.0, xp_ref[...])
    buf_ref[pl.ds(POOL_HALO, ts), :] = x_ref[...]
    pos = t_in_seq * ts + lax.broadcasted_iota(I32, (ts, 1), 0)
    for g, w in enumerate(POOL_WINDOWS):
        cols = pl.ds(g * dg, dg)
        xg = buf_ref[pl.ds(POOL_HALO, ts), cols]
        s = xg
        for j in range(1, w):
            s = s + buf_ref[pl.ds(POOL_HALO - j, ts), cols]
        cnt = jnp.minimum(pos + 1, w).astype(F32)
        pooled = s / cnt - xg
        mixed = jnp.dot(pooled.astype(BF16), pw_ref[g].astype(BF16), preferred_element_type=F32)
        o_ref[:, cols] = (mixed * ps_ref[:, cols]).astype(o_ref.dtype)


def _pool(proj, col_block, pool_w, pool_scale, seq):
    n = proj.shape[0]
    dp = pool_scale.shape[0]
    ts = _tile(seq, 512)
    hb = ts // POOL_HALO
    return pl.pallas_call(
        functools.partial(_pool_kernel, tiles_per_seq=seq // ts),
        grid=(n // ts,),
        in_specs=[
            pl.BlockSpec((ts, dp), lambda i: (i, col_block)),
            pl.BlockSpec((POOL_HALO, dp), lambda i: (jnp.maximum(i * hb - 1, 0), col_block)),
            pl.BlockSpec(pool_w.shape, lambda i: (0, 0, 0)),
            pl.BlockSpec((1, dp), lambda i: (0, 0)),
        ],
        out_specs=pl.BlockSpec((ts, dp), lambda i: (i, 0)),
        out_shape=jax.ShapeDtypeStruct((n, dp), BF16),
        scratch_shapes=[pltpu.VMEM((ts + POOL_HALO, dp), F32)],
        compiler_params=_params("parallel"),
        name="pool",
    )(proj, proj, pool_w, pool_scale.reshape(1, dp))


def _mm_res_kernel(*refs, n_a):
    a_refs, w_refs = refs[:n_a], refs[n_a:2 * n_a]
    x_ref, gate_ref, o_ref = refs[2 * n_a:]
    acc = jnp.dot(a_refs[0][...], w_refs[0][...].astype(BF16), preferred_element_type=F32)
    for a_ref, w_ref in zip(a_refs[1:], w_refs[1:]):
        acc = acc + jnp.dot(a_ref[...], w_ref[...].astype(BF16), preferred_element_type=F32)
    o_ref[...] = x_ref[...] + gate_ref[...] * acc


def _mm_res(a_list, w, x, gate, seq):
    n, d = x.shape
    n_a = len(a_list)
    ka = a_list[0].shape[1]
    assert all(a.shape[1] == ka for a in a_list) and w.shape[0] == n_a * ka
    tm = _tile(seq, PROJ_ROWS)
    tn = _tile(d, 512)
    per_b = seq // tm
    nb = gate.shape[0]
    in_specs = [pl.BlockSpec((tm, ka), lambda i, j: (i, 0)) for _ in a_list]
    in_specs += [pl.BlockSpec((ka, tn), functools.partial(lambda i, j, r: (r, j), r=r)) for r in range(n_a)]
    in_specs += [pl.BlockSpec((tm, tn), lambda i, j: (i, j)),
                 pl.BlockSpec((None, 1, tn), lambda i, j: (i // per_b, 0, j))]
    return pl.pallas_call(
        functools.partial(_mm_res_kernel, n_a=n_a),
        grid=(n // tm, d // tn),
        in_specs=in_specs,
        out_specs=pl.BlockSpec((tm, tn), lambda i, j: (i, j)),
        out_shape=jax.ShapeDtypeStruct((n, d), F32),
        compiler_params=_params("parallel", "parallel"),
        name="matmul_residual",
    )(*a_list, *([w] * n_a), x, gate.reshape(nb, 1, d))


def _ffn_kernel(x_ref, g_ref, sh_ref, sc_ref, gate_ref, w1_ref, w3_ref, w2_ref, o_ref, h_ref):
    f = pl.program_id(1)

    @pl.when(f == 0)
    def _():
        h_ref[...] = _norm_mod(x_ref[...], g_ref[...], sh_ref[...], sc_ref[...]).astype(BF16)
        o_ref[...] = jnp.zeros_like(o_ref)

    h = h_ref[...]
    a = jnp.dot(h, w1_ref[...].astype(BF16), preferred_element_type=F32)
    b = jnp.dot(h, w3_ref[...].astype(BF16), preferred_element_type=F32)
    act = (a * _sigmoid(a) * b).astype(BF16)
    o_ref[...] += jnp.dot(act, w2_ref[...].astype(BF16), preferred_element_type=F32)

    @pl.when(f == pl.num_programs(1) - 1)
    def _():
        o_ref[...] = x_ref[...] + gate_ref[...] * o_ref[...]


def _ffn(x, g, shift, scale, gate, w1, w3, w2, seq):
    n, d = x.shape
    dff = w1.shape[1]
    tm = _tile(seq, 1024)
    tf = _tile(dff, 256)
    per_b = seq // tm
    nb = gate.shape[0]
    vec = lambda a: a.reshape(nb, 1, d)
    bvec = pl.BlockSpec((None, 1, d), lambda i, f: (i // per_b, 0, 0))
    return pl.pallas_call(
        _ffn_kernel,
        grid=(n // tm, dff // tf),
        in_specs=[
            pl.BlockSpec((tm, d), lambda i, f: (i, 0), pipeline_mode=pl.Buffered(1)),
            pl.BlockSpec((1, d), lambda i, f: (0, 0)),
            bvec, bvec, bvec,
            pl.BlockSpec((d, tf), lambda i, f: (0, f)),
            pl.BlockSpec((d, tf), lambda i, f: (0, f)),
            pl.BlockSpec((tf, d), lambda i, f: (f, 0)),
        ],
        out_specs=pl.BlockSpec((tm, d), lambda i, f: (i, 0)),
        out_shape=jax.ShapeDtypeStruct((n, d), F32),
        scratch_shapes=[pltpu.VMEM((tm, d), BF16)],
        compiler_params=_params("parallel", "arbitrary"),
        name="swiglu_ffn",
    )(x, g.reshape(1, d), vec(shift), vec(scale), vec(gate), w1, w3, w2)


def _sb_attn_kernel(q_ref, k_ref, v_ref, o_ref):
    qi = pl.program_id(2)
    T = q_ref.shape[0]
    dh = q_ref.shape[1] // ATTN_HEADS_PER_STEP
    scale = dh ** -0.5
    row = lax.broadcasted_iota(I32, (T, T), 0)
    col = lax.broadcasted_iota(I32, (T, T), 1)
    later = (row > col).astype(BF16)

    def prepare(j, g, diagonal):
        start = pl.multiple_of(j * T, T)
        cols = pl.ds(g * dh, dh)
        kj = k_ref[pl.ds(start, T), cols]
        z = lax.dot_general(q_ref[:, cols], kj, (((1,), (1,)), ((), ())), preferred_element_type=F32) * scale
        sp = jnp.maximum(z, 0.0) + jnp.log(1.0 + jnp.exp(-jnp.abs(z)))
        lom = -sp
        if diagonal:
            strict = col < row
            lom = jnp.where(strict, lom, 0.0)
        hi = lom.astype(BF16)
        lo = (lom - hi.astype(F32)).astype(BF16)
        suffix = (jnp.dot(hi, later, preferred_element_type=F32)
                  + jnp.dot(lo, later, preferred_element_type=F32))
        base = z - sp + suffix
        if diagonal:
            base = jnp.where(strict, base, -jnp.inf)
        return base, jnp.sum(lom, axis=1, keepdims=True), v_ref[pl.ds(start, T), cols]

    def walk(tiles, accs, rems):
        new_accs, new_rems = [], []
        for g in range(ATTN_HEADS_PER_STEP):
            parts = [prepare(j, g, diagonal) for j, diagonal in tiles]
            acc, rem = accs[g], rems[g]
            for base, total, vj in parts:
                a = jnp.exp(base + rem)
                acc = acc + jnp.dot(a.astype(BF16), vj, preferred_element_type=F32)
                rem = rem + total
            new_accs.append(acc)
            new_rems.append(rem)
        return tuple(new_accs), tuple(new_rems)

    def alive(rems):
        top = rems[0]
        for r in rems[1:]:
            top = jnp.maximum(top, r)
        return jnp.max(top) > ATTN_LOG_CUTOFF

    def finish(accs):
        for g in range(ATTN_HEADS_PER_STEP):
            o_ref[:, pl.ds(g * dh, dh)] = accs[g].astype(o_ref.dtype)

    zeros = lambda w: tuple(jnp.zeros((T, w), F32) for _ in range(ATTN_HEADS_PER_STEP))

    @pl.when(qi == 0)
    def _():
        accs, _ = walk([(qi, True)], zeros(dh), zeros(1))
        finish(accs)

    @pl.when(qi > 0)
    def _():
        accs, rems = walk([(qi, True), (qi - 1, False)], zeros(dh), zeros(1))

        def cond(c):
            s, live, _, _ = c
            return jnp.logical_and(s < qi, live)

        def body(c):
            s, _, accs, rems = c
            accs, rems = walk([(qi - 1 - s, False)], accs, rems)
            return s + 1, alive(rems), accs, rems

        _, _, accs, _ = lax.while_loop(cond, body, (jnp.int32(1), alive(rems), accs, rems))
        finish(accs)


def _sb_attention(qkv, nb, seq):
    n, d3 = qkv.shape
    d = d3 // 3
    nh = SB_HEADS
    dh = d // nh
    T = _tile(seq, ATTN_TILE)
    nq = seq // T
    hg = ATTN_HEADS_PER_STEP
    ng = nh // hg
    return pl.pallas_call(
        _sb_attn_kernel,
        grid=(nb, ng, nq),
        in_specs=[
            pl.BlockSpec((T, hg * dh), lambda b, h, i: (b * nq + i, h)),
            pl.BlockSpec((seq, hg * dh), lambda b, h, i: (b, ng + h)),
            pl.BlockSpec((seq, hg * dh), lambda b, h, i: (b, 2 * ng + h)),
        ],
        out_specs=pl.BlockSpec((T, hg * dh), lambda b, h, i: (b * nq + i, h)),
        out_shape=jax.ShapeDtypeStruct((n, d), BF16),
        compiler_params=_params("parallel", "parallel", "parallel"),
        name="stick_breaking_attention",
    )(qkv, qkv, qkv)


def _router_kernel(x_ref, g_ref, sh_ref, sc_ref, wr_ref, h_ref, info_ref, gate_ref, cnt_ref, carry_ref):
    tm = x_ref.shape[0]

    @pl.when(pl.program_id(0) == 0)
    def _():
        carry_ref[...] = jnp.zeros_like(carry_ref)

    h = _norm_mod(x_ref[...], g_ref[...], sh_ref[...], sc_ref[...])
    h_ref[...] = h
    logits = _dot_f32(h, wr_ref[...])
    lane = lax.broadcasted_iota(I32, (tm, LANES), 1)
    lane_f = lane.astype(F32)
    lg = jnp.where(lane < N_EXPERTS, logits, -jnp.inf)
    v0 = jnp.max(lg, axis=1, keepdims=True)
    i0 = jnp.min(jnp.where(lg == v0, lane_f, float(LANES)), axis=1, keepdims=True)
    lg1 = jnp.where(lane_f == i0, -jnp.inf, lg)
    v1 = jnp.max(lg1, axis=1, keepdims=True)
    i1 = jnp.min(jnp.where(lg1 == v1, lane_f, float(LANES)), axis=1, keepdims=True)
    ex = jnp.exp(v1 - v0)
    g0 = 1.0 / (1.0 + ex)
    g1 = ex / (1.0 + ex)
    sel0 = lane_f == i0
    sel1 = lane_f == i1
    onehot = jnp.where(sel0 | sel1, 1.0, 0.0)
    row = lax.broadcasted_iota(I32, (tm, tm), 0)
    col = lax.broadcasted_iota(I32, (tm, tm), 1)
    before = (col < row).astype(BF16)
    earlier = jnp.dot(before, onehot.astype(BF16), preferred_element_type=F32) + carry_ref[...]
    r0 = jnp.sum(jnp.where(sel0, earlier, 0.0), axis=1, keepdims=True)
    r1 = jnp.sum(jnp.where(sel1, earlier, 0.0), axis=1, keepdims=True)
    carry_ref[...] += jnp.sum(onehot, axis=0, keepdims=True)
    info = jnp.where(lane == 0, i0, jnp.where(lane == 1, i1, jnp.where(lane == 2, r0, jnp.where(lane == 3, r1, 0.0))))
    info_ref[...] = info.astype(I32)
    gate_ref[...] = jnp.where(lane == 0, g0, jnp.where(lane == 1, g1, 0.0))
    cnt_ref[...] = carry_ref[...].astype(I32)


def _router(x, g, shift, scale, w_router, seq):
    n, d = x.shape
    tm = _tile(seq, 512)
    per_b = seq // tm
    nb = shift.shape[0]
    vec = lambda a: a.reshape(nb, 1, d)
    bvec = pl.BlockSpec((None, 1, d), lambda i: (i // per_b, 0, 0))
    wr = jnp.pad(w_router, ((0, 0), (0, LANES - w_router.shape[1])))
    return pl.pallas_call(
        _router_kernel,
        grid=(n // tm,),
        in_specs=[
            pl.BlockSpec((tm, d), lambda i: (i, 0)),
            pl.BlockSpec((1, d), lambda i: (0, 0)),
            bvec, bvec,
            pl.BlockSpec((d, LANES), lambda i: (0, 0)),
        ],
        out_specs=[
            pl.BlockSpec((tm, d), lambda i: (i, 0)),
            pl.BlockSpec((tm, LANES), lambda i: (i, 0)),
            pl.BlockSpec((tm, LANES), lambda i: (i, 0)),
            pl.BlockSpec((1, LANES), lambda i: (0, 0)),
        ],
        out_shape=[
            jax.ShapeDtypeStruct((n, d), F32),
            jax.ShapeDtypeStruct((n, LANES), I32),
            jax.ShapeDtypeStruct((n, LANES), F32),
            jax.ShapeDtypeStruct((1, LANES), I32),
        ],
        scratch_shapes=[pltpu.VMEM((1, LANES), F32)],
        compiler_params=_params("arbitrary"),
        name="router",
    )(x, g.reshape(1, d), vec(shift), vec(scale), wr)


def _route_plan(info, counts, tm):
    n = info.shape[0]
    ne = N_EXPERTS
    e0, e1, r0, r1 = info[:, 0], info[:, 1], info[:, 2], info[:, 3]
    cnt = counts[0, :ne]
    padded = (cnt + MOE_SUB - 1) // MOE_SUB * MOE_SUB
    ends = jnp.cumsum(padded)
    starts = ends - padded
    pos0 = (starts[e0] + r0).astype(I32)
    pos1 = (starts[e1] + r1).astype(I32)
    r_pad = TOP_K * n + ne * MOE_SUB
    nv = -(-r_pad // tm) + ne
    ntiles = (padded + tm - 1) // tm
    v_end = jnp.cumsum(ntiles)
    v_first = v_end - ntiles
    total = v_end[-1]
    vid = jnp.arange(nv, dtype=I32)
    vc = jnp.minimum(vid, total - 1)
    grp = jnp.sum((v_end[None, :] <= vc[:, None]).astype(I32), axis=1)
    k = vc - v_first[grp]
    v_start = (starts[grp] + k * tm).astype(I32)
    v_nsub = jnp.where(vid < total, jnp.minimum(tm, padded[grp] - k * tm) // MOE_SUB, 0).astype(I32)
    tail = ends[-1] + MOE_SUB * jnp.arange(ne, dtype=I32)
    z_start = jnp.concatenate([jnp.maximum(ends - MOE_SUB, 0), tail]).astype(I32)
    z_on = jnp.concatenate([padded > 0, tail < r_pad]).astype(I32)
    return pos0, pos1, (grp.astype(I32), v_start, v_nsub), (z_start, z_on), r_pad


def _dispatch_kernel(p0_ref, p1_ref, zs_ref, zn_ref, h_ref, hs_hbm, zero_ref, sem, zsem):
    i = pl.program_id(0)
    td = h_ref.shape[0]
    sub = zero_ref.shape[0]

    def zero_copy(b):
        dst = hs_hbm.at[pl.ds(pl.multiple_of(zs_ref[b], sub), sub), :]
        return pltpu.make_async_copy(zero_ref, dst, zsem)

    @pl.when(i == 0)
    def _():
        zero_ref[...] = jnp.zeros_like(zero_ref)
        for b in range(zs_ref.shape[0]):
            pl.when(zn_ref[b] == 1)(lambda: zero_copy(b).start())
        for b in range(zs_ref.shape[0]):
            pl.when(zn_ref[b] == 1)(lambda: zero_copy(b).wait())

    def issue(r, c):
        t = i * td + r
        src = h_ref.at[pl.ds(r, 1), :]
        pltpu.make_async_copy(src, hs_hbm.at[pl.ds(p0_ref[t], 1), :], sem.at[0]).start()
        pltpu.make_async_copy(src, hs_hbm.at[pl.ds(p1_ref[t], 1), :], sem.at[1]).start()
        return c
    lax.fori_loop(0, td, issue, 0, unroll=DMA_UNROLL)
    for stream in range(TOP_K):
        pltpu.make_async_copy(h_ref, hs_hbm.at[pl.ds(0, td), :], sem.at[stream]).wait()


def _dispatch(h, pos0, pos1, zero_blocks, r_pad, seq):
    n, d = h.shape
    td = _tile(seq, 512)
    z_start, z_on = zero_blocks
    return pl.pallas_call(
        _dispatch_kernel,
        grid_spec=pltpu.PrefetchScalarGridSpec(
            num_scalar_prefetch=4,
            grid=(n // td,),
            in_specs=[pl.BlockSpec((td, d), lambda i, p0, p1, zs, zn: (i, 0))],
            out_specs=pl.BlockSpec(memory_space=pl.ANY),
            scratch_shapes=[
                pltpu.VMEM((MOE_SUB, d), F32),
                pltpu.SemaphoreType.DMA((TOP_K,)),
                pltpu.SemaphoreType.DMA(()),
            ],
        ),
        out_shape=jax.ShapeDtypeStruct((r_pad, d), F32),
        compiler_params=_params("arbitrary"),
        name="moe_dispatch",
    )(pos0, pos1, z_start, z_on, h)


def _moe_kernel(vg_ref, vs_ref, vn_ref, hs_hbm, w1_ref, w3_ref, w2_ref, y_hbm,
                stage_ref, hb_ref, acc_ref, wb1_ref, wb3_ref, wb2_ref, sem_in, sem_out):
    v = pl.program_id(0)
    f = pl.program_id(1)
    sub = stage_ref.shape[1]
    start = vs_ref[v]
    nsub = vn_ref[v]

    def rows(sb):
        return pl.ds(pl.multiple_of(sb * sub, sub), sub)

    def hbm_rows(row0, sb):
        return pl.ds(pl.multiple_of(row0 + sb * sub, sub), sub)

    def in_copy(sb, slot):
        return pltpu.make_async_copy(hs_hbm.at[hbm_rows(start, sb), :], stage_ref.at[slot], sem_in.at[slot])

    def out_copy(row0, sb):
        return pltpu.make_async_copy(acc_ref.at[rows(sb), :], y_hbm.at[hbm_rows(row0, sb), :], sem_out)

    def each_block(count, fn):
        def body(sb, c):
            fn(sb)
            return c
        lax.fori_loop(0, count, body, 0)

    @pl.when(f == 0)
    def _():
        @pl.when(nsub > 0)
        def _():
            in_copy(0, 0).start()

        def load(sb):
            slot = sb % 2

            @pl.when(sb + 1 < nsub)
            def _():
                in_copy(sb + 1, 1 - slot).start()
            in_copy(sb, slot).wait()
            hb_ref[rows(sb), :] = stage_ref[slot].astype(BF16)
        each_block(nsub, load)

        prev = jnp.maximum(v - 1, 0)
        n_prev = jnp.where(v > 0, vn_ref[prev], 0)
        each_block(n_prev, lambda sb: out_copy(vs_ref[prev], sb).wait())

        def clear(sb):
            acc_ref[rows(sb), :] = jnp.zeros((sub, acc_ref.shape[1]), F32)
        each_block(nsub, clear)

    @pl.when(nsub > 0)
    def _():
        def cast_weights():
            w1b = w1_ref[...].astype(BF16)
            w3b = w3_ref[...].astype(BF16)
            w2b = w2_ref[...].astype(BF16)
            wb1_ref[...] = w1b
            wb3_ref[...] = w3b
            wb2_ref[...] = w2b
            return w1b, w3b, w2b

        def blocks(sb, count, first=False):
            w1b, w3b, w2b = cast_weights() if first else (wb1_ref[...], wb3_ref[...], wb2_ref[...])
            r = pl.ds(pl.multiple_of(sb * sub, sub), count * sub)
            hs = hb_ref[r, :]
            a = jnp.dot(hs, w1b, preferred_element_type=F32)
            b = jnp.dot(hs, w3b, preferred_element_type=F32)
            act = (a * _sigmoid(a) * b).astype(BF16)
            acc_ref[r, :] += jnp.dot(act, w2b, preferred_element_type=F32)

        n_long = nsub // MOE_LONG
        n_rest = nsub - n_long * MOE_LONG
        pl.when(n_long > 0)(lambda: blocks(0, MOE_LONG, first=True))
        each_block(n_long - 1, lambda i: blocks((i + 1) * MOE_LONG, MOE_LONG))
        for rest in range(1, MOE_LONG):
            pl.when(jnp.logical_and(n_rest == rest, n_long > 0))(lambda: blocks(n_long * MOE_LONG, rest))
            pl.when(jnp.logical_and(n_rest == rest, n_long == 0))(lambda: blocks(0, rest, first=True))

    @pl.when(f == pl.num_programs(1) - 1)
    def _():
        each_block(nsub, lambda sb: out_copy(start, sb).start())

        @pl.when(v == pl.num_programs(0) - 1)
        def _():
            each_block(nsub, lambda sb: out_copy(start, sb).wait())


def _moe(hs, visits, w1, w3, w2, tm):
    r_pad, d = hs.shape
    _, _, dff = w1.shape
    v_group, v_start, v_nsub = visits
    nv = v_group.shape[0]
    tf = _tile(dff, 256)
    nf = dff // tf

    def f_eff(v, f, vn):
        return jnp.where(vn[v] > 0, f, nf - 1)

    return pl.pallas_call(
        _moe_kernel,
        grid_spec=pltpu.PrefetchScalarGridSpec(
            num_scalar_prefetch=3,
            grid=(nv, nf),
            in_specs=[
                pl.BlockSpec(memory_space=pl.ANY),
                pl.BlockSpec((None, d, tf), lambda v, f, vg, vs, vn: (vg[v], 0, f_eff(v, f, vn))),
                pl.BlockSpec((None, d, tf), lambda v, f, vg, vs, vn: (vg[v], 0, f_eff(v, f, vn))),
                pl.BlockSpec((None, tf, d), lambda v, f, vg, vs, vn: (vg[v], f_eff(v, f, vn), 0)),
            ],
            out_specs=pl.BlockSpec(memory_space=pl.ANY),
            scratch_shapes=[
                pltpu.VMEM((2, MOE_SUB, d), F32),
                pltpu.VMEM((tm, d), BF16),
                pltpu.VMEM((tm, d), F32),
                pltpu.VMEM((d, tf), BF16),
                pltpu.VMEM((d, tf), BF16),
                pltpu.VMEM((tf, d), BF16),
                pltpu.SemaphoreType.DMA((2,)),
                pltpu.SemaphoreType.DMA(()),
            ],
        ),
        out_shape=jax.ShapeDtypeStruct((r_pad, d), F32),
        input_output_aliases={3: 0},
        compiler_params=_params("arbitrary", "arbitrary"),
        name="moe_experts",
    )(v_group, v_start, v_nsub, hs, w1, w3, w2)


def _combine_kernel(p0_ref, p1_ref, y_hbm, x_ref, gate_ref, rg_ref, gf_ref, o_ref, buf_ref, sem):
    i = pl.program_id(0)
    tc = x_ref.shape[0]
    slot = i % 2

    def gather(tile, into):
        def issue(r, c):
            t = tile * tc + r
            for k, p_ref in enumerate((p0_ref, p1_ref)):
                pltpu.make_async_copy(y_hbm.at[pl.ds(p_ref[t], 1), :],
                                      buf_ref.at[into, k, pl.ds(r, 1), :], sem.at[into, k]).start()
            return c
        lax.fori_loop(0, tc, issue, 0, unroll=DMA_UNROLL)

    @pl.when(i == 0)
    def _():
        gather(0, 0)

    @pl.when(i + 1 < pl.num_programs(0))
    def _():
        gather(i + 1, 1 - slot)

    for k in range(TOP_K):
        pltpu.make_async_copy(y_hbm.at[pl.ds(0, tc), :], buf_ref.at[slot, k], sem.at[slot, k]).wait()

    rg = rg_ref[...]
    moe = rg[:, 0:1] * buf_ref[slot, 0] + rg[:, 1:2] * buf_ref[slot, 1]
    xn = x_ref[...] + gate_ref[...] * moe
    ms = jnp.mean(xn * xn, axis=-1, keepdims=True)
    o_ref[...] = xn * lax.rsqrt(ms + EPS) * gf_ref[...]


def _combine(y, pos0, pos1, route_gates, x, gate, g_final, seq):
    n, d = x.shape
    tc = _tile(seq, 512)
    per_b = seq // tc
    nb = gate.shape[0]
    return pl.pallas_call(
        _combine_kernel,
        grid_spec=pltpu.PrefetchScalarGridSpec(
            num_scalar_prefetch=2,
            grid=(n // tc,),
            in_specs=[
                pl.BlockSpec(memory_space=pl.ANY),
                pl.BlockSpec((tc, d), lambda i, p0, p1: (i, 0)),
                pl.BlockSpec((None, 1, d), lambda i, p0, p1: (i // per_b, 0, 0)),
                pl.BlockSpec((tc, LANES), lambda i, p0, p1: (i, 0)),
                pl.BlockSpec((1, d), lambda i, p0, p1: (0, 0)),
            ],
            out_specs=pl.BlockSpec((tc, d), lambda i, p0, p1: (i, 0)),
            scratch_shapes=[
                pltpu.VMEM((2, TOP_K, tc, d), F32),
                pltpu.SemaphoreType.DMA((2, TOP_K)),
            ],
        ),
        out_shape=jax.ShapeDtypeStruct((n, d), F32),
        compiler_params=_params("arbitrary"),
        name="moe_combine_final_norm",
    )(pos0, pos1, y, x, gate.reshape(nb, 1, d), route_gates, g_final.reshape(1, d))


def _even_layer(x, mod, seq, g_mix, g_ffn, w_in, b_gates, conv_w, conv_b, head_g, pool_w, pool_scale,
                w_out, w1, w3, w2):
    n, d = x.shape
    nb = mod.shape[0]
    sh1, sc1, g1, sh2, sc2, g2 = jnp.split(mod, 6, axis=-1)
    dm = head_g.shape[0]
    dp = pool_scale.shape[0]
    ng = 2 * MLSTM_HEADS
    assert dm == dp and w_in.shape[1] == 4 * dm + ng + dp
    w_pool = w_in[:, 4 * dm + ng:]
    w_gate = jnp.pad(w_in[:, 4 * dm:4 * dm + ng], ((0, 0), (0, LANES - ng)))
    b_gate = jnp.pad(b_gates, (0, LANES - ng)).reshape(1, LANES)
    proj, gates = _nm_matmul(x, g_mix, sh1, sc1, w_in, F32, seq, head_cols=4 * dm,
                             side=(w_pool, w_gate, b_gate))
    gates_row = gates[:, :SUBLANES].T
    hm = _mlstm(proj, gates, gates_row, conv_w, conv_b, head_g, nb, seq)
    hp = _pool(proj, 4 * dm // dp, pool_w, pool_scale, seq)
    x = _mm_res([hm, hp], w_out, x, g1, seq)
    return _ffn(x, g_ffn, sh2, sc2, g2, w1, w3, w2, seq)


def _odd_layer(x, mod, seq, g_mix, g_ffn, g_final, w_qkv, w_o, w_router, w1, w3, w2):
    n, d = x.shape
    nb = mod.shape[0]
    sh1, sc1, g1, sh2, sc2, g2 = jnp.split(mod, 6, axis=-1)
    qkv = _nm_matmul(x, g_mix, sh1, sc1, w_qkv, BF16, seq)
    att = _sb_attention(qkv, nb, seq)
    x = _mm_res([att], w_o, x, g1, seq)
    h, info, gates, counts = _router(x, g_ffn, sh2, sc2, w_router, seq)
    tm = MOE_TILE_SUBS * MOE_SUB
    pos0, pos1, visits, zero_blocks, r_pad = _route_plan(info, counts, tm)
    hs = _dispatch(h, pos0, pos1, zero_blocks, r_pad, seq)
    short = tuple(a[:N_EXPERTS] for a in visits)
    y = lax.cond(visits[2][N_EXPERTS] == 0,
                 lambda rows: _moe(rows, short, w1, w3, w2, tm),
                 lambda rows: _moe(rows, visits, w1, w3, w2, tm), hs)
    return _combine(y, pos0, pos1, gates, x, g2, g_final, seq)


def kernel(x, c, ada_w, ada_b, norm_mix_g, norm_ffn_g, norm_final_g, ev_w_in, ev_b_gates, ev_conv_w,
           ev_conv_b, ev_head_g, ev_pool_w, ev_pool_scale, ev_w_out, ev_ffn_w1, ev_ffn_w3, ev_ffn_w2,
           od_w_qkv, od_w_o, od_router, od_moe_w1, od_moe_w3, od_moe_w2):
    nb, seq, d = x.shape
    assert ada_w.shape[0] == 2, "one even and one odd layer"
    mod = _adaln(c, ada_w, ada_b)
    xf = x.reshape(nb * seq, d)
    xf = _even_layer(xf, mod[0], seq, norm_mix_g[0], norm_ffn_g[0], ev_w_in[0], ev_b_gates[0],
                     ev_conv_w[0], ev_conv_b[0], ev_head_g[0], ev_pool_w[0], ev_pool_scale[0],
                     ev_w_out[0], ev_ffn_w1[0], ev_ffn_w3[0], ev_ffn_w2[0])
    out = _odd_layer(xf, mod[1], seq, norm_mix_g[1], norm_ffn_g[1], norm_final_g, od_w_qkv[0], od_w_o[0],
                     od_router[0], od_moe_w1[0], od_moe_w3[0], od_moe_w2[0])
    return out.reshape(nb, seq, d)
```

```python
import functools

import jax
import jax.numpy as jnp
from jax import lax
from jax.experimental import pallas as pl
from jax.experimental.pallas import tpu as pltpu

F32 = jnp.float32
BF16 = jnp.bfloat16
I32 = jnp.int32

EPS = 1e-6
MLSTM_HEADS = 4
CONV_K = 4
POOL_WINDOWS = (2, 4, 8, 16)
SB_HEADS = 16
N_EXPERTS = 8
TOP_K = 2

LANES = 128
SUBLANES = 8
VMEM_LIMIT = 56 * 1024 * 1024
PROJ_ROWS = 2048
NORM_ROWS = 512
MLSTM_TILE = 256
POOL_HALO = 16
ATTN_TILE = 256
ATTN_HEADS_PER_STEP = 4
ATTN_LOG_CUTOFF = -120.0
MOE_SUB = 128
MOE_TILE_SUBS = 18
MOE_LONG = 6
DMA_UNROLL = 8


def _split_bf16(a):
    hi = a.astype(BF16)
    return hi, (a - hi.astype(F32)).astype(BF16)


def _dot_f32(a, b):
    (ah, al), (bh, bl) = _split_bf16(a), _split_bf16(b)
    dot = functools.partial(jnp.dot, preferred_element_type=F32)
    return dot(ah, bh) + (dot(al, bh) + dot(ah, bl))


def _params(*sem):
    return pltpu.CompilerParams(dimension_semantics=sem, vmem_limit_bytes=VMEM_LIMIT)


def _tile(n, pref):
    t = min(n, pref)
    assert n % t == 0, (n, pref)
    return t


def _sigmoid(x):
    return 1.0 / (1.0 + jnp.exp(-x))


def _log_sigmoid(x):
    return jnp.minimum(x, 0.0) - jnp.log1p(jnp.exp(-jnp.abs(x)))


def _norm_mod(x, g, shift, scale):
    ms = jnp.mean(x * x, axis=-1, keepdims=True)
    y = x * lax.rsqrt(ms + EPS) * g
    return y * (1.0 + scale) + shift


def _adaln_kernel(cb_ref, w_ref, b_ref, o_ref):
    nb = cb_ref.shape[0]
    for j in range(w_ref.shape[1] // LANES):
        cols = pl.ds(j * LANES, LANES)
        w = w_ref[:, cols]
        for b in range(nb):
            o_ref[pl.ds(b, 1), cols] = jnp.sum(w * cb_ref[b], axis=0, keepdims=True) + b_ref[:, cols]


def _adaln(c, ada_w, ada_b):
    depth, d, n6 = ada_w.shape
    nb = c.shape[0]
    tn = _tile(n6, 1024)
    c_act = c * _sigmoid(c)
    cb = jnp.broadcast_to(c_act[:, :, None], (nb, d, LANES))
    return pl.pallas_call(
        _adaln_kernel,
        grid=(depth, n6 // tn),
        in_specs=[
            pl.BlockSpec((nb, d, LANES), lambda l, j: (0, 0, 0)),
            pl.BlockSpec((None, d, tn), lambda l, j: (l, 0, j)),
            pl.BlockSpec((None, 1, tn), lambda l, j: (l, 0, j)),
        ],
        out_specs=pl.BlockSpec((None, nb, tn), lambda l, j: (l, 0, j)),
        out_shape=jax.ShapeDtypeStruct((depth, nb, n6), F32),
        compiler_params=_params("parallel", "parallel"),
        name="adaln",
    )(cb, ada_w, ada_b.reshape(depth, 1, n6))


def _nm_matmul_kernel(x_hbm, g_ref, sh_ref, sc_ref, w_ref, *rest, with_side, n_head):
    if with_side:
        wt_ref, ws_ref, bs_ref, o_ref, side_ref, h_ref, xbuf_ref, sem = rest
    else:
        o_ref, h_ref, xbuf_ref, sem = rest
    i = pl.program_id(0)
    j = pl.program_id(1)

    @pl.when(j == 0)
    def _():
        tm = h_ref.shape[0]
        chunk = xbuf_ref.shape[1]

        def x_copy(c):
            src = x_hbm.at[pl.ds(pl.multiple_of(i * tm + c * chunk, chunk), chunk), :]
            return pltpu.make_async_copy(src, xbuf_ref.at[c % 2], sem.at[c % 2])

        x_copy(0).start()
        for c in range(tm // chunk):
            if (c + 1) * chunk < tm:
                x_copy(c + 1).start()
            x_copy(c).wait()
            rows = pl.ds(c * chunk, chunk)
            h = _norm_mod(xbuf_ref[c % 2], g_ref[...], sh_ref[...], sc_ref[...])
            h_ref[rows, :] = h.astype(BF16)
            if with_side:
                side_ref[rows, :] = _dot_f32(h, ws_ref[...]) + bs_ref[...]

    def project(weights_ref):
        o_ref[...] = jnp.dot(h_ref[...], weights_ref[...].astype(BF16),
                             preferred_element_type=F32).astype(o_ref.dtype)

    if with_side:
        pl.when(j < n_head)(lambda: project(w_ref))
        pl.when(j >= n_head)(lambda: project(wt_ref))
    else:
        project(w_ref)


def _nm_matmul(x, g, shift, scale, w, out_dtype, seq, head_cols=None, side=None):
    n, d = x.shape
    tm = _tile(seq, PROJ_ROWS)
    per_b = seq // tm
    nb = shift.shape[0]
    vec = lambda a: a.reshape(nb, 1, d)
    if side is None:
        nout = w.shape[1]
        tn = _tile(nout, 512)
        n_head = nout // tn
    else:
        nout = head_cols + side[0].shape[1]
        tn = _tile(head_cols, 512)
        assert side[0].shape[1] % tn == 0
        n_head = head_cols // tn
    in_specs = [
        pl.BlockSpec(memory_space=pl.ANY),
        pl.BlockSpec((1, d), lambda i, j: (0, 0)),
        pl.BlockSpec((None, 1, d), lambda i, j: (i // per_b, 0, 0)),
        pl.BlockSpec((None, 1, d), lambda i, j: (i // per_b, 0, 0)),
        pl.BlockSpec((d, tn), lambda i, j: (0, jnp.minimum(j, n_head - 1))),
    ]
    args = [x, g.reshape(1, d), vec(shift), vec(scale), w]
    out_specs = pl.BlockSpec((tm, tn), lambda i, j: (i, j))
    out_shape = jax.ShapeDtypeStruct((n, nout), out_dtype)
    if side is not None:
        in_specs += [pl.BlockSpec((d, tn), lambda i, j: (0, jnp.maximum(j - n_head, 0))),
                     pl.BlockSpec((d, LANES), lambda i, j: (0, 0)),
                     pl.BlockSpec((1, LANES), lambda i, j: (0, 0))]
        args += list(side)
        out_specs = [out_specs, pl.BlockSpec((tm, LANES), lambda i, j: (i, 0))]
        out_shape = [out_shape, jax.ShapeDtypeStruct((n, LANES), F32)]
    return pl.pallas_call(
        functools.partial(_nm_matmul_kernel, with_side=side is not None, n_head=n_head),
        grid=(n // tm, nout // tn),
        in_specs=in_specs,
        out_specs=out_specs,
        out_shape=out_shape,
        scratch_shapes=[
            pltpu.VMEM((tm, d), BF16),
            pltpu.VMEM((2, _tile(tm, NORM_ROWS), d), F32),
            pltpu.SemaphoreType.DMA((2,)),
        ],
        compiler_params=_params("parallel", "arbitrary"),
        name="norm_mod_matmul",
    )(*args)


def _mlstm_kernel(q_ref, qp_ref, k_ref, kp_ref, v_ref, o_ref, cw_ref, cb_ref, hg_ref, gc_ref, gr_ref,
                  out_ref, ext_ref, c_ref, n_ref, m_ref):
    chunk = pl.program_id(1)
    nh = c_ref.shape[0]
    L, dm = q_ref.shape
    dh = dm // nh

    @pl.when(chunk == 0)
    def _():
        c_ref[...] = jnp.zeros_like(c_ref)
        n_ref[...] = jnp.zeros_like(n_ref)
        m_ref[...] = jnp.zeros_like(m_ref)

    def conv_silu(cur_ref, prev_ref, off):
        ext_ref[pl.ds(0, SUBLANES), :] = jnp.where(chunk == 0, 0.0, prev_ref[...])
        ext_ref[pl.ds(SUBLANES, L), :] = cur_ref[...]
        y = cb_ref[:, pl.ds(off, dm)]
        for j in range(CONV_K):
            y = y + cw_ref[pl.ds(j, 1), pl.ds(off, dm)] * ext_ref[pl.ds(SUBLANES - CONV_K + 1 + j, L), :]
        return y * _sigmoid(y)

    q_all = conv_silu(q_ref, qp_ref, 0) * (dh ** -0.5)
    k_all = conv_silu(k_ref, kp_ref, dm)
    gc = gc_ref[...]
    row = lax.broadcasted_iota(I32, (L, L), 0)
    col = lax.broadcasted_iota(I32, (L, L), 1)
    causal = col <= row

    for head in range(nh):
        cols = pl.ds(head * dh, dh)
        q = q_all[:, head * dh:(head + 1) * dh]
        k = k_all[:, head * dh:(head + 1) * dh]
        qb = q.astype(BF16)
        kb = k.astype(BF16)
        vb = v_ref[:, cols].astype(BF16)

        i_col = gc[:, head:head + 1]
        f_col = gc[:, nh + head:nh + head + 1]
        i_row = gr_ref[pl.ds(head, 1), :]
        f_row = gr_ref[pl.ds(nh + head, 1), :]
        logf_col = _log_sigmoid(f_col)
        logf_row = _log_sigmoid(f_row)
        b_col = jnp.sum(jnp.where(causal, logf_row, 0.0), axis=1, keepdims=True)
        b_row = jnp.sum(jnp.where(row <= col, logf_col, 0.0), axis=0, keepdims=True)

        m_prev = m_ref[pl.ds(head, 1), pl.ds(0, 1)]
        n_prev = n_ref[pl.ds(head, 1), :]
        dmat = jnp.where(causal, b_col - b_row + i_row, -jnp.inf)
        inter = b_col + m_prev
        m_t = jnp.maximum(inter, jnp.max(dmat, axis=1, keepdims=True))
        w_inter = jnp.exp(inter - m_t)
        scores = lax.dot_general(qb, kb, (((1,), (1,)), ((), ())), preferred_element_type=F32)
        wmat = jnp.exp(dmat - m_t) * scores
        c_old = c_ref[head]
        num = (w_inter * jnp.dot(qb, c_old.astype(BF16), preferred_element_type=F32)
               + jnp.dot(wmat.astype(BF16), vb, preferred_element_type=F32))
        den = w_inter * jnp.sum(q * n_prev, axis=1, keepdims=True) + jnp.sum(wmat, axis=1, keepdims=True)
        hval = num / jnp.maximum(jnp.abs(den), jnp.exp(-m_t))

        g_end = jnp.sum(logf_row, axis=1, keepdims=True)
        wl = g_end - b_col + i_col
        m_new = jnp.maximum(g_end + m_prev, jnp.max(wl, axis=0, keepdims=True))
        decay = jnp.exp(g_end + m_prev - m_new)
        wk = jnp.exp(wl - m_new) * k
        c_ref[head] = decay * c_old + lax.dot_general(wk.astype(BF16), vb, (((0,), (0,)), ((), ())),
                                                      preferred_element_type=F32)
        n_ref[pl.ds(head, 1), :] = decay * n_prev + jnp.sum(wk, axis=0, keepdims=True)
        m_ref[pl.ds(head, 1), :] = jnp.broadcast_to(m_new, (1, m_ref.shape[1]))

        ms = jnp.mean(hval * hval, axis=-1, keepdims=True)
        hn = hval * lax.rsqrt(ms + EPS) * hg_ref[:, cols]
        out_ref[:, cols] = (hn * _sigmoid(o_ref[:, cols])).astype(out_ref.dtype)


def _mlstm(proj, gates_col, gates_row, conv_w, conv_b, head_g, nb, seq):
    n = proj.shape[0]
    nh = MLSTM_HEADS
    dm = head_g.shape[0]
    dh = dm // nh
    L = _tile(seq, MLSTM_TILE)
    nc = seq // L
    rb = L // SUBLANES

    assert 2 * nh <= SUBLANES

    def cur(off):
        return pl.BlockSpec((L, dm), lambda b, c: (b * nc + c, off))

    def prev(off):
        return pl.BlockSpec((SUBLANES, dm), lambda b, c: (jnp.maximum((b * nc + c) * rb - 1, 0), off))

    return pl.pallas_call(
        _mlstm_kernel,
        grid=(nb, nc),
        in_specs=[
            cur(0), prev(0), cur(1), prev(1), cur(2), cur(3),
            pl.BlockSpec((CONV_K, 2 * dm), lambda b, c: (0, 0)),
            pl.BlockSpec((1, 2 * dm), lambda b, c: (0, 0)),
            pl.BlockSpec((1, dm), lambda b, c: (0, 0)),
            pl.BlockSpec((L, LANES), lambda b, c: (b * nc + c, 0)),
            pl.BlockSpec((SUBLANES, L), lambda b, c: (0, b * nc + c)),
        ],
        out_specs=pl.BlockSpec((L, dm), lambda b, c: (b * nc + c, 0)),
        out_shape=jax.ShapeDtypeStruct((n, dm), BF16),
        scratch_shapes=[
            pltpu.VMEM((L + SUBLANES, dm), F32),
            pltpu.VMEM((nh, dh, dh), F32),
            pltpu.VMEM((SUBLANES, dh), F32),
            pltpu.VMEM((SUBLANES, LANES), F32),
        ],
        compiler_params=_params("parallel", "arbitrary"),
        name="mlstm",
    )(proj, proj, proj, proj, proj, proj, conv_w, conv_b.reshape(1, -1), head_g.reshape(1, dm),
      gates_col, gates_row)


def _pool_kernel(x_ref, xp_ref, pw_ref, ps_ref, o_ref, buf_ref, *, tiles_per_seq):
    ts, dp = x_ref.shape
    dg = dp // len(POOL_WINDOWS)
    t_in_seq = pl.program_id(0) % tiles_per_seq
    buf_ref[pl.ds(0, POOL_HALO), :] = jnp.where(t_in_seq == 0, 0.0, xp_ref[...])
    buf_ref[pl.ds(POOL_HALO, ts), :] = x_ref[...]
    pos = t_in_seq * ts + lax.broadcasted_iota(I32, (ts, 1), 0)
    for g, w in enumerate(POOL_WINDOWS):
        cols = pl.ds(g * dg, dg)
        xg = buf_ref[pl.ds(POOL_HALO, ts), cols]
        s = xg
        for j in range(1, w):
            s = s + buf_ref[pl.ds(POOL_HALO - j, ts), cols]
        cnt = jnp.minimum(pos + 1, w).astype(F32)
        pooled = s / cnt - xg
        mixed = jnp.dot(pooled.astype(BF16), pw_ref[g].astype(BF16), preferred_element_type=F32)
        o_ref[:, cols] = (mixed * ps_ref[:, cols]).astype(o_ref.dtype)


def _pool(proj, col_block, pool_w, pool_scale, seq):
    n = proj.shape[0]
    dp = pool_scale.shape[0]
    ts = _tile(seq, 512)
    hb = ts // POOL_HALO
    return pl.pallas_call(
        functools.partial(_pool_kernel, tiles_per_seq=seq // ts),
        grid=(n // ts,),
        in_specs=[
            pl.BlockSpec((ts, dp), lambda i: (i, col_block)),
            pl.BlockSpec((POOL_HALO, dp), lambda i: (jnp.maximum(i * hb - 1, 0), col_block)),
            pl.BlockSpec(pool_w.shape, lambda i: (0, 0, 0)),
            pl.BlockSpec((1, dp), lambda i: (0, 0)),
        ],
        out_specs=pl.BlockSpec((ts, dp), lambda i: (i, 0)),
        out_shape=jax.ShapeDtypeStruct((n, dp), BF16),
        scratch_shapes=[pltpu.VMEM((ts + POOL_HALO, dp), F32)],
        compiler_params=_params("parallel"),
        name="pool",
    )(proj, proj, pool_w, pool_scale.reshape(1, dp))


def _mm_res_kernel(*refs, n_a):
    a_refs, w_refs = refs[:n_a], refs[n_a:2 * n_a]
    x_ref, gate_ref, o_ref = refs[2 * n_a:]
    acc = jnp.dot(a_refs[0][...], w_refs[0][...].astype(BF16), preferred_element_type=F32)
    for a_ref, w_ref in zip(a_refs[1:], w_refs[1:]):
        acc = acc + jnp.dot(a_ref[...], w_ref[...].astype(BF16), preferred_element_type=F32)
    o_ref[...] = x_ref[...] + gate_ref[...] * acc


def _mm_res(a_list, w, x, gate, seq):
    n, d = x.shape
    n_a = len(a_list)
    ka = a_list[0].shape[1]
    assert all(a.shape[1] == ka for a in a_list) and w.shape[0] == n_a * ka
    tm = _tile(seq, PROJ_ROWS)
    tn = _tile(d, 512)
    per_b = seq // tm
    nb = gate.shape[0]
    in_specs = [pl.BlockSpec((tm, ka), lambda i, j: (i, 0)) for _ in a_list]
    in_specs += [pl.BlockSpec((ka, tn), functools.partial(lambda i, j, r: (r, j), r=r)) for r in range(n_a)]
    in_specs += [pl.BlockSpec((tm, tn), lambda i, j: (i, j)),
                 pl.BlockSpec((None, 1, tn), lambda i, j: (i // per_b, 0, j))]
    return pl.pallas_call(
        functools.partial(_mm_res_kernel, n_a=n_a),
        grid=(n // tm, d // tn),
        in_specs=in_specs,
        out_specs=pl.BlockSpec((tm, tn), lambda i, j: (i, j)),
        out_shape=jax.ShapeDtypeStruct((n, d), F32),
        compiler_params=_params("parallel", "parallel"),
        name="matmul_residual",
    )(*a_list, *([w] * n_a), x, gate.reshape(nb, 1, d))


def _ffn_kernel(x_ref, g_ref, sh_ref, sc_ref, gate_ref, w1_ref, w3_ref, w2_ref, o_ref, h_ref):
    f = pl.program_id(1)

    @pl.when(f == 0)
    def _():
        h_ref[...] = _norm_mod(x_ref[...], g_ref[...], sh_ref[...], sc_ref[...]).astype(BF16)
        o_ref[...] = jnp.zeros_like(o_ref)

    h = h_ref[...]
    a = jnp.dot(h, w1_ref[...].astype(BF16), preferred_element_type=F32)
    b = jnp.dot(h, w3_ref[...].astype(BF16), preferred_element_type=F32)
    act = (a * _sigmoid(a) * b).astype(BF16)
    o_ref[...] += jnp.dot(act, w2_ref[...].astype(BF16), preferred_element_type=F32)

    @pl.when(f == pl.num_programs(1) - 1)
    def _():
        o_ref[...] = x_ref[...] + gate_ref[...] * o_ref[...]


def _ffn(x, g, shift, scale, gate, w1, w3, w2, seq):
    n, d = x.shape
    dff = w1.shape[1]
    tm = _tile(seq, 1024)
    tf = _tile(dff, 256)
    per_b = seq // tm
    nb = gate.shape[0]
    vec = lambda a: a.reshape(nb, 1, d)
    bvec = pl.BlockSpec((None, 1, d), lambda i, f: (i // per_b, 0, 0))
    return pl.pallas_call(
        _ffn_kernel,
        grid=(n // tm, dff // tf),
        in_specs=[
            pl.BlockSpec((tm, d), lambda i, f: (i, 0), pipeline_mode=pl.Buffered(1)),
            pl.BlockSpec((1, d), lambda i, f: (0, 0)),
            bvec, bvec, bvec,
            pl.BlockSpec((d, tf), lambda i, f: (0, f)),
            pl.BlockSpec((d, tf), lambda i, f: (0, f)),
            pl.BlockSpec((tf, d), lambda i, f: (f, 0)),
        ],
        out_specs=pl.BlockSpec((tm, d), lambda i, f: (i, 0)),
        out_shape=jax.ShapeDtypeStruct((n, d), F32),
        scratch_shapes=[pltpu.VMEM((tm, d), BF16)],
        compiler_params=_params("parallel", "arbitrary"),
        name="swiglu_ffn",
    )(x, g.reshape(1, d), vec(shift), vec(scale), vec(gate), w1, w3, w2)


def _sb_attn_kernel(q_ref, k_ref, v_ref, o_ref):
    qi = pl.program_id(2)
    T = q_ref.shape[0]
    dh = q_ref.shape[1] // ATTN_HEADS_PER_STEP
    scale = dh ** -0.5
    row = lax.broadcasted_iota(I32, (T, T), 0)
    col = lax.broadcasted_iota(I32, (T, T), 1)
    later = (row > col).astype(BF16)

    def prepare(j, g, diagonal):
        start = pl.multiple_of(j * T, T)
        cols = pl.ds(g * dh, dh)
        kj = k_ref[pl.ds(start, T), cols]
        z = lax.dot_general(q_ref[:, cols], kj, (((1,), (1,)), ((), ())), preferred_element_type=F32) * scale
        sp = jnp.maximum(z, 0.0) + jnp.log(1.0 + jnp.exp(-jnp.abs(z)))
        lom = -sp
        if diagonal:
            strict = col < row
            lom = jnp.where(strict, lom, 0.0)
        hi = lom.astype(BF16)
        lo = (lom - hi.astype(F32)).astype(BF16)
        suffix = (jnp.dot(hi, later, preferred_element_type=F32)
                  + jnp.dot(lo, later, preferred_element_type=F32))
        base = z - sp + suffix
        if diagonal:
            base = jnp.where(strict, base, -jnp.inf)
        return base, jnp.sum(lom, axis=1, keepdims=True), v_ref[pl.ds(start, T), cols]

    def walk(tiles, accs, rems):
        new_accs, new_rems = [], []
        for g in range(ATTN_HEADS_PER_STEP):
            parts = [prepare(j, g, diagonal) for j, diagonal in tiles]
            acc, rem = accs[g], rems[g]
            for base, total, vj in parts:
                a = jnp.exp(base + rem)
                acc = acc + jnp.dot(a.astype(BF16), vj, preferred_element_type=F32)
                rem = rem + total
            new_accs.append(acc)
            new_rems.append(rem)
        return tuple(new_accs), tuple(new_rems)

    def alive(rems):
        top = rems[0]
        for r in rems[1:]:
            top = jnp.maximum(top, r)
        return jnp.max(top) > ATTN_LOG_CUTOFF

    def finish(accs):
        for g in range(ATTN_HEADS_PER_STEP):
            o_ref[:, pl.ds(g * dh, dh)] = accs[g].astype(o_ref.dtype)

    zeros = lambda w: tuple(jnp.zeros((T, w), F32) for _ in range(ATTN_HEADS_PER_STEP))

    @pl.when(qi == 0)
    def _():
        accs, _ = walk([(qi, True)], zeros(dh), zeros(1))
        finish(accs)

    @pl.when(qi > 0)
    def _():
        accs, rems = walk([(qi, True), (qi - 1, False)], zeros(dh), zeros(1))

        def cond(c):
            s, live, _, _ = c
            return jnp.logical_and(s < qi, live)

        def body(c):
            s, _, accs, rems = c
            accs, rems = walk([(qi - 1 - s, False)], accs, rems)
            return s + 1, alive(rems), accs, rems

        _, _, accs, _ = lax.while_loop(cond, body, (jnp.int32(1), alive(rems), accs, rems))
        finish(accs)


def _sb_attention(qkv, nb, seq):
    n, d3 = qkv.shape
    d = d3 // 3
    nh = SB_HEADS
    dh = d // nh
    T = _tile(seq, ATTN_TILE)
    nq = seq // T
    hg = ATTN_HEADS_PER_STEP
    ng = nh // hg
    return pl.pallas_call(
        _sb_attn_kernel,
        grid=(nb, ng, nq),
        in_specs=[
            pl.BlockSpec((T, hg * dh), lambda b, h, i: (b * nq + i, h)),
            pl.BlockSpec((seq, hg * dh), lambda b, h, i: (b, ng + h)),
            pl.BlockSpec((seq, hg * dh), lambda b, h, i: (b, 2 * ng + h)),
        ],
        out_specs=pl.BlockSpec((T, hg * dh), lambda b, h, i: (b * nq + i, h)),
        out_shape=jax.ShapeDtypeStruct((n, d), BF16),
        compiler_params=_params("parallel", "parallel", "parallel"),
        name="stick_breaking_attention",
    )(qkv, qkv, qkv)


def _router_kernel(x_ref, g_ref, sh_ref, sc_ref, wr_ref, h_ref, info_ref, gate_ref, cnt_ref, carry_ref):
    tm = x_ref.shape[0]

    @pl.when(pl.program_id(0) == 0)
    def _():
        carry_ref[...] = jnp.zeros_like(carry_ref)

    h = _norm_mod(x_ref[...], g_ref[...], sh_ref[...], sc_ref[...])
    h_ref[...] = h
    logits = _dot_f32(h, wr_ref[...])
    lane = lax.broadcasted_iota(I32, (tm, LANES), 1)
    lane_f = lane.astype(F32)
    lg = jnp.where(lane < N_EXPERTS, logits, -jnp.inf)
    v0 = jnp.max(lg, axis=1, keepdims=True)
    i0 = jnp.min(jnp.where(lg == v0, lane_f, float(LANES)), axis=1, keepdims=True)
    lg1 = jnp.where(lane_f == i0, -jnp.inf, lg)
    v1 = jnp.max(lg1, axis=1, keepdims=True)
    i1 = jnp.min(jnp.where(lg1 == v1, lane_f, float(LANES)), axis=1, keepdims=True)
    ex = jnp.exp(v1 - v0)
    g0 = 1.0 / (1.0 + ex)
    g1 = ex / (1.0 + ex)
    sel0 = lane_f == i0
    sel1 = lane_f == i1
    onehot = jnp.where(sel0 | sel1, 1.0, 0.0)
    row = lax.broadcasted_iota(I32, (tm, tm), 0)
    col = lax.broadcasted_iota(I32, (tm, tm), 1)
    before = (col < row).astype(BF16)
    earlier = jnp.dot(before, onehot.astype(BF16), preferred_element_type=F32) + carry_ref[...]
    r0 = jnp.sum(jnp.where(sel0, earlier, 0.0), axis=1, keepdims=True)
    r1 = jnp.sum(jnp.where(sel1, earlier, 0.0), axis=1, keepdims=True)
    carry_ref[...] += jnp.sum(onehot, axis=0, keepdims=True)
    info = jnp.where(lane == 0, i0, jnp.where(lane == 1, i1, jnp.where(lane == 2, r0, jnp.where(lane == 3, r1, 0.0))))
    info_ref[...] = info.astype(I32)
    gate_ref[...] = jnp.where(lane == 0, g0, jnp.where(lane == 1, g1, 0.0))
    cnt_ref[...] = carry_ref[...].astype(I32)


def _router(x, g, shift, scale, w_router, seq):
    n, d = x.shape
    tm = _tile(seq, 512)
    per_b = seq // tm
    nb = shift.shape[0]
    vec = lambda a: a.reshape(nb, 1, d)
    bvec = pl.BlockSpec((None, 1, d), lambda i: (i // per_b, 0, 0))
    wr = jnp.pad(w_router, ((0, 0), (0, LANES - w_router.shape[1])))
    return pl.pallas_call(
        _router_kernel,
        grid=(n // tm,),
        in_specs=[
            pl.BlockSpec((tm, d), lambda i: (i, 0)),
            pl.BlockSpec((1, d), lambda i: (0, 0)),
            bvec, bvec,
            pl.BlockSpec((d, LANES), lambda i: (0, 0)),
        ],
        out_specs=[
            pl.BlockSpec((tm, d), lambda i: (i, 0)),
            pl.BlockSpec((tm, LANES), lambda i: (i, 0)),
            pl.BlockSpec((tm, LANES), lambda i: (i, 0)),
            pl.BlockSpec((1, LANES), lambda i: (0, 0)),
        ],
        out_shape=[
            jax.ShapeDtypeStruct((n, d), F32),
            jax.ShapeDtypeStruct((n, LANES), I32),
            jax.ShapeDtypeStruct((n, LANES), F32),
            jax.ShapeDtypeStruct((1, LANES), I32),
        ],
        scratch_shapes=[pltpu.VMEM((1, LANES), F32)],
        compiler_params=_params("arbitrary"),
        name="router",
    )(x, g.reshape(1, d), vec(shift), vec(scale), wr)


def _route_plan(info, counts, tm):
    n = info.shape[0]
    ne = N_EXPERTS
    e0, e1, r0, r1 = info[:, 0], info[:, 1], info[:, 2], info[:, 3]
    cnt = counts[0, :ne]
    padded = (cnt + MOE_SUB - 1) // MOE_SUB * MOE_SUB
    ends = jnp.cumsum(padded)
    starts = ends - padded
    pos0 = (starts[e0] + r0).astype(I32)
    pos1 = (starts[e1] + r1).astype(I32)
    r_pad = TOP_K * n + ne * MOE_SUB
    nv = -(-r_pad // tm) + ne
    ntiles = (padded + tm - 1) // tm
    v_end = jnp.cumsum(ntiles)
    v_first = v_end - ntiles
    total = v_end[-1]
    vid = jnp.arange(nv, dtype=I32)
    vc = jnp.minimum(vid, total - 1)
    grp = jnp.sum((v_end[None, :] <= vc[:, None]).astype(I32), axis=1)
    k = vc - v_first[grp]
    v_start = (starts[grp] + k * tm).astype(I32)
    v_nsub = jnp.where(vid < total, jnp.minimum(tm, padded[grp] - k * tm) // MOE_SUB, 0).astype(I32)
    tail = ends[-1] + MOE_SUB * jnp.arange(ne, dtype=I32)
    z_start = jnp.concatenate([jnp.maximum(ends - MOE_SUB, 0), tail]).astype(I32)
    z_on = jnp.concatenate([padded > 0, tail < r_pad]).astype(I32)
    return pos0, pos1, (grp.astype(I32), v_start, v_nsub), (z_start, z_on), r_pad


def _dispatch_kernel(p0_ref, p1_ref, zs_ref, zn_ref, h_ref, hs_hbm, zero_ref, sem, zsem):
    i = pl.program_id(0)
    td = h_ref.shape[0]
    sub = zero_ref.shape[0]

    def zero_copy(b):
        dst = hs_hbm.at[pl.ds(pl.multiple_of(zs_ref[b], sub), sub), :]
        return pltpu.make_async_copy(zero_ref, dst, zsem)

    @pl.when(i == 0)
    def _():
        zero_ref[...] = jnp.zeros_like(zero_ref)
        for b in range(zs_ref.shape[0]):
            pl.when(zn_ref[b] == 1)(lambda: zero_copy(b).start())
        for b in range(zs_ref.shape[0]):
            pl.when(zn_ref[b] == 1)(lambda: zero_copy(b).wait())

    def issue(r, c):
        t = i * td + r
        src = h_ref.at[pl.ds(r, 1), :]
        pltpu.make_async_copy(src, hs_hbm.at[pl.ds(p0_ref[t], 1), :], sem.at[0]).start()
        pltpu.make_async_copy(src, hs_hbm.at[pl.ds(p1_ref[t], 1), :], sem.at[1]).start()
        return c
    lax.fori_loop(0, td, issue, 0, unroll=DMA_UNROLL)
    for stream in range(TOP_K):
        pltpu.make_async_copy(h_ref, hs_hbm.at[pl.ds(0, td), :], sem.at[stream]).wait()


def _dispatch(h, pos0, pos1, zero_blocks, r_pad, seq):
    n, d = h.shape
    td = _tile(seq, 512)
    z_start, z_on = zero_blocks
    return pl.pallas_call(
        _dispatch_kernel,
        grid_spec=pltpu.PrefetchScalarGridSpec(
            num_scalar_prefetch=4,
            grid=(n // td,),
            in_specs=[pl.BlockSpec((td, d), lambda i, p0, p1, zs, zn: (i, 0))],
            out_specs=pl.BlockSpec(memory_space=pl.ANY),
            scratch_shapes=[
                pltpu.VMEM((MOE_SUB, d), F32),
                pltpu.SemaphoreType.DMA((TOP_K,)),
                pltpu.SemaphoreType.DMA(()),
            ],
        ),
        out_shape=jax.ShapeDtypeStruct((r_pad, d), F32),
        compiler_params=_params("arbitrary"),
        name="moe_dispatch",
    )(pos0, pos1, z_start, z_on, h)


def _moe_kernel(vg_ref, vs_ref, vn_ref, hs_hbm, w1_ref, w3_ref, w2_ref, y_hbm,
                stage_ref, hb_ref, acc_ref, wb1_ref, wb3_ref, wb2_ref, sem_in, sem_out):
    v = pl.program_id(0)
    f = pl.program_id(1)
    sub = stage_ref.shape[1]
    start = vs_ref[v]
    nsub = vn_ref[v]

    def rows(sb):
        return pl.ds(pl.multiple_of(sb * sub, sub), sub)

    def hbm_rows(row0, sb):
        return pl.ds(pl.multiple_of(row0 + sb * sub, sub), sub)

    def in_copy(sb, slot):
        return pltpu.make_async_copy(hs_hbm.at[hbm_rows(start, sb), :], stage_ref.at[slot], sem_in.at[slot])

    def out_copy(row0, sb):
        return pltpu.make_async_copy(acc_ref.at[rows(sb), :], y_hbm.at[hbm_rows(row0, sb), :], sem_out)

    def each_block(count, fn):
        def body(sb, c):
            fn(sb)
            return c
        lax.fori_loop(0, count, body, 0)

    @pl.when(f == 0)
    def _():
        @pl.when(nsub > 0)
        def _():
            in_copy(0, 0).start()

        def load(sb):
            slot = sb % 2

            @pl.when(sb + 1 < nsub)
            def _():
                in_copy(sb + 1, 1 - slot).start()
            in_copy(sb, slot).wait()
            hb_ref[rows(sb), :] = stage_ref[slot].astype(BF16)
        each_block(nsub, load)

        prev = jnp.maximum(v - 1, 0)
        n_prev = jnp.where(v > 0, vn_ref[prev], 0)
        each_block(n_prev, lambda sb: out_copy(vs_ref[prev], sb).wait())

        def clear(sb):
            acc_ref[rows(sb), :] = jnp.zeros((sub, acc_ref.shape[1]), F32)
        each_block(nsub, clear)

    @pl.when(nsub > 0)
    def _():
        def cast_weights():
            w1b = w1_ref[...].astype(BF16)
            w3b = w3_ref[...].astype(BF16)
            w2b = w2_ref[...].astype(BF16)
            wb1_ref[...] = w1b
            wb3_ref[...] = w3b
            wb2_ref[...] = w2b
            return w1b, w3b, w2b

        def blocks(sb, count, first=False):
            w1b, w3b, w2b = cast_weights() if first else (wb1_ref[...], wb3_ref[...], wb2_ref[...])
            r = pl.ds(pl.multiple_of(sb * sub, sub), count * sub)
            hs = hb_ref[r, :]
            a = jnp.dot(hs, w1b, preferred_element_type=F32)
            b = jnp.dot(hs, w3b, preferred_element_type=F32)
            act = (a * _sigmoid(a) * b).astype(BF16)
            acc_ref[r, :] += jnp.dot(act, w2b, preferred_element_type=F32)

        n_long = nsub // MOE_LONG
        n_rest = nsub - n_long * MOE_LONG
        pl.when(n_long > 0)(lambda: blocks(0, MOE_LONG, first=True))
        each_block(n_long - 1, lambda i: blocks((i + 1) * MOE_LONG, MOE_LONG))
        for rest in range(1, MOE_LONG):
            pl.when(jnp.logical_and(n_rest == rest, n_long > 0))(lambda: blocks(n_long * MOE_LONG, rest))
            pl.when(jnp.logical_and(n_rest == rest, n_long == 0))(lambda: blocks(0, rest, first=True))

    @pl.when(f == pl.num_programs(1) - 1)
    def _():
        each_block(nsub, lambda sb: out_copy(start, sb).start())

        @pl.when(v == pl.num_programs(0) - 1)
        def _():
            each_block(nsub, lambda sb: out_copy(start, sb).wait())


def _moe(hs, visits, w1, w3, w2, tm):
    r_pad, d = hs.shape
    _, _, dff = w1.shape
    v_group, v_start, v_nsub = visits
    nv = v_group.shape[0]
    tf = _tile(dff, 256)
    nf = dff // tf

    def f_eff(v, f, vn):
        return jnp.where(vn[v] > 0, f, nf - 1)

    return pl.pallas_call(
        _moe_kernel,
        grid_spec=pltpu.PrefetchScalarGridSpec(
            num_scalar_prefetch=3,
            grid=(nv, nf),
            in_specs=[
                pl.BlockSpec(memory_space=pl.ANY),
                pl.BlockSpec((None, d, tf), lambda v, f, vg, vs, vn: (vg[v], 0, f_eff(v, f, vn))),
                pl.BlockSpec((None, d, tf), lambda v, f, vg, vs, vn: (vg[v], 0, f_eff(v, f, vn))),
                pl.BlockSpec((None, tf, d), lambda v, f, vg, vs, vn: (vg[v], f_eff(v, f, vn), 0)),
            ],
            out_specs=pl.BlockSpec(memory_space=pl.ANY),
            scratch_shapes=[
                pltpu.VMEM((2, MOE_SUB, d), F32),
                pltpu.VMEM((tm, d), BF16),
                pltpu.VMEM((tm, d), F32),
                pltpu.VMEM((d, tf), BF16),
                pltpu.VMEM((d, tf), BF16),
                pltpu.VMEM((tf, d), BF16),
                pltpu.SemaphoreType.DMA((2,)),
                pltpu.SemaphoreType.DMA(()),
            ],
        ),
        out_shape=jax.ShapeDtypeStruct((r_pad, d), F32),
        input_output_aliases={3: 0},
        compiler_params=_params("arbitrary", "arbitrary"),
        name="moe_experts",
    )(v_group, v_start, v_nsub, hs, w1, w3, w2)


def _combine_kernel(p0_ref, p1_ref, y_hbm, x_ref, gate_ref, rg_ref, gf_ref, o_ref, buf_ref, sem):
    i = pl.program_id(0)
    tc = x_ref.shape[0]
    slot = i % 2

    def gather(tile, into):
        def issue(r, c):
            t = tile * tc + r
            for k, p_ref in enumerate((p0_ref, p1_ref)):
                pltpu.make_async_copy(y_hbm.at[pl.ds(p_ref[t], 1), :],
                                      buf_ref.at[into, k, pl.ds(r, 1), :], sem.at[into, k]).start()
            return c
        lax.fori_loop(0, tc, issue, 0, unroll=DMA_UNROLL)

    @pl.when(i == 0)
    def _():
        gather(0, 0)

    @pl.when(i + 1 < pl.num_programs(0))
    def _():
        gather(i + 1, 1 - slot)

    for k in range(TOP_K):
        pltpu.make_async_copy(y_hbm.at[pl.ds(0, tc), :], buf_ref.at[slot, k], sem.at[slot, k]).wait()

    rg = rg_ref[...]
    moe = rg[:, 0:1] * buf_ref[slot, 0] + rg[:, 1:2] * buf_ref[slot, 1]
    xn = x_ref[...] + gate_ref[...] * moe
    ms = jnp.mean(xn * xn, axis=-1, keepdims=True)
    o_ref[...] = xn * lax.rsqrt(ms + EPS) * gf_ref[...]


def _combine(y, pos0, pos1, route_gates, x, gate, g_final, seq):
    n, d = x.shape
    tc = _tile(seq, 512)
    per_b = seq // tc
    nb = gate.shape[0]
    return pl.pallas_call(
        _combine_kernel,
        grid_spec=pltpu.PrefetchScalarGridSpec(
            num_scalar_prefetch=2,
            grid=(n // tc,),
            in_specs=[
                pl.BlockSpec(memory_space=pl.ANY),
                pl.BlockSpec((tc, d), lambda i, p0, p1: (i, 0)),
                pl.BlockSpec((None, 1, d), lambda i, p0, p1: (i // per_b, 0, 0)),
                pl.BlockSpec((tc, LANES), lambda i, p0, p1: (i, 0)),
                pl.BlockSpec((1, d), lambda i, p0, p1: (0, 0)),
            ],
            out_specs=pl.BlockSpec((tc, d), lambda i, p0, p1: (i, 0)),
            scratch_shapes=[
                pltpu.VMEM((2, TOP_K, tc, d), F32),
                pltpu.SemaphoreType.DMA((2, TOP_K)),
            ],
        ),
        out_shape=jax.ShapeDtypeStruct((n, d), F32),
        compiler_params=_params("arbitrary"),
        name="moe_combine_final_norm",
    )(pos0, pos1, y, x, gate.reshape(nb, 1, d), route_gates, g_final.reshape(1, d))


def _even_layer(x, mod, seq, g_mix, g_ffn, w_in, b_gates, conv_w, conv_b, head_g, pool_w, pool_scale,
                w_out, w1, w3, w2):
    n, d = x.shape
    nb = mod.shape[0]
    sh1, sc1, g1, sh2, sc2, g2 = jnp.split(mod, 6, axis=-1)
    dm = head_g.shape[0]
    dp = pool_scale.shape[0]
    ng = 2 * MLSTM_HEADS
    assert dm == dp and w_in.shape[1] == 4 * dm + ng + dp
    w_pool = w_in[:, 4 * dm + ng:]
    w_gate = jnp.pad(w_in[:, 4 * dm:4 * dm + ng], ((0, 0), (0, LANES - ng)))
    b_gate = jnp.pad(b_gates, (0, LANES - ng)).reshape(1, LANES)
    proj, gates = _nm_matmul(x, g_mix, sh1, sc1, w_in, F32, seq, head_cols=4 * dm,
                             side=(w_pool, w_gate, b_gate))
    gates_row = gates[:, :SUBLANES].T
    hm = _mlstm(proj, gates, gates_row, conv_w, conv_b, head_g, nb, seq)
    hp = _pool(proj, 4 * dm // dp, pool_w, pool_scale, seq)
    x = _mm_res([hm, hp], w_out, x, g1, seq)
    return _ffn(x, g_ffn, sh2, sc2, g2, w1, w3, w2, seq)


def _odd_layer(x, mod, seq, g_mix, g_ffn, g_final, w_qkv, w_o, w_router, w1, w3, w2):
    n, d = x.shape
    nb = mod.shape[0]
    sh1, sc1, g1, sh2, sc2, g2 = jnp.split(mod, 6, axis=-1)
    qkv = _nm_matmul(x, g_mix, sh1, sc1, w_qkv, BF16, seq)
    att = _sb_attention(qkv, nb, seq)
    x = _mm_res([att], w_o, x, g1, seq)
    h, info, gates, counts = _router(x, g_ffn, sh2, sc2, w_router, seq)
    tm = MOE_TILE_SUBS * MOE_SUB
    pos0, pos1, visits, zero_blocks, r_pad = _route_plan(info, counts, tm)
    hs = _dispatch(h, pos0, pos1, zero_blocks, r_pad, seq)
    short = tuple(a[:N_EXPERTS] for a in visits)
    y = lax.cond(visits[2][N_EXPERTS] == 0,
                 lambda rows: _moe(rows, short, w1, w3, w2, tm),
                 lambda rows: _moe(rows, visits, w1, w3, w2, tm), hs)
    return _combine(y, pos0, pos1, gates, x, g2, g_final, seq)


def kernel(x, c, ada_w, ada_b, norm_mix_g, norm_ffn_g, norm_final_g, ev_w_in, ev_b_gates, ev_conv_w,
           ev_conv_b, ev_head_g, ev_pool_w, ev_pool_scale, ev_w_out, ev_ffn_w1, ev_ffn_w3, ev_ffn_w2,
           od_w_qkv, od_w_o, od_router, od_moe_w1, od_moe_w3, od_moe_w2):
    nb, seq, d = x.shape
    assert ada_w.shape[0] == 2, "one even and one odd layer"
    mod = _adaln(c, ada_w, ada_b)
    xf = x.reshape(nb * seq, d)
    xf = _even_layer(xf, mod[0], seq, norm_mix_g[0], norm_ffn_g[0], ev_w_in[0], ev_b_gates[0],
                     ev_conv_w[0], ev_conv_b[0], ev_head_g[0], ev_pool_w[0], ev_pool_scale[0],
                     ev_w_out[0], ev_ffn_w1[0], ev_ffn_w3[0], ev_ffn_w2[0])
    out = _odd_layer(xf, mod[1], seq, norm_mix_g[1], norm_ffn_g[1], norm_final_g, od_w_qkv[0], od_w_o[0],
                     od_router[0], od_moe_w1[0], od_moe_w3[0], od_moe_w2[0])
    return out.reshape(nb, seq, d)
```

```python
import functools

import jax
import jax.numpy as jnp
from jax import lax
from jax.experimental import pallas as pl
from jax.experimental.pallas import tpu as pltpu

F32 = jnp.float32
BF16 = jnp.bfloat16
I32 = jnp.int32

EPS = 1e-6
MLSTM_HEADS = 4
CONV_K = 4
POOL_WINDOWS = (2, 4, 8, 16)
SB_HEADS = 16
N_EXPERTS = 8
TOP_K = 2

LANES = 128
SUBLANES = 8
VMEM_LIMIT = 56 * 1024 * 1024
PROJ_ROWS = 2048
NORM_ROWS = 512
MLSTM_TILE = 256
POOL_HALO = 16
ATTN_TILE = 256
ATTN_HEADS_PER_STEP = 4
ATTN_LOG_CUTOFF = -120.0
MOE_SUB = 128
MOE_TILE_SUBS = 18
MOE_LONG = 6
DMA_UNROLL = 8


def _split_bf16(a):
    hi = a.astype(BF16)
    return hi, (a - hi.astype(F32)).astype(BF16)


def _dot_f32(a, b):
    (ah, al), (bh, bl) = _split_bf16(a), _split_bf16(b)
    dot = functools.partial(jnp.dot, preferred_element_type=F32)
    return dot(ah, bh) + (dot(al, bh) + dot(ah, bl))


def _params(*sem):
    return pltpu.CompilerParams(dimension_semantics=sem, vmem_limit_bytes=VMEM_LIMIT)


def _tile(n, pref):
    t = min(n, pref)
    assert n % t == 0, (n, pref)
    return t


def _sigmoid(x):
    return 1.0 / (1.0 + jnp.exp(-x))


def _log_sigmoid(x):
    return jnp.minimum(x, 0.0) - jnp.log1p(jnp.exp(-jnp.abs(x)))


def _norm_mod(x, g, shift, scale):
    ms = jnp.mean(x * x, axis=-1, keepdims=True)
    y = x * lax.rsqrt(ms + EPS) * g
    return y * (1.0 + scale) + shift


def _adaln_kernel(cb_ref, w_ref, b_ref, o_ref):
    nb = cb_ref.shape[0]
    for j in range(w_ref.shape[1] // LANES):
        cols = pl.ds(j * LANES, LANES)
        w = w_ref[:, cols]
        for b in range(nb):
            o_ref[pl.ds(b, 1), cols] = jnp.sum(w * cb_ref[b], axis=0, keepdims=True) + b_ref[:, cols]


def _adaln(c, ada_w, ada_b):
    depth, d, n6 = ada_w.shape
    nb = c.shape[0]
    tn = _tile(n6, 1024)
    c_act = c * _sigmoid(c)
    cb = jnp.broadcast_to(c_act[:, :, None], (nb, d, LANES))
    return pl.pallas_call(
        _adaln_kernel,
        grid=(depth, n6 // tn),
        in_specs=[
            pl.BlockSpec((nb, d, LANES), lambda l, j: (0, 0, 0)),
            pl.BlockSpec((None, d, tn), lambda l, j: (l, 0, j)),
            pl.BlockSpec((None, 1, tn), lambda l, j: (l, 0, j)),
        ],
        out_specs=pl.BlockSpec((None, nb, tn), lambda l, j: (l, 0, j)),
        out_shape=jax.ShapeDtypeStruct((depth, nb, n6), F32),
        compiler_params=_params("parallel", "parallel"),
        name="adaln",
    )(cb, ada_w, ada_b.reshape(depth, 1, n6))


def _nm_matmul_kernel(x_hbm, g_ref, sh_ref, sc_ref, w_ref, *rest, with_side, n_head):
    if with_side:
        wt_ref, ws_ref, bs_ref, o_ref, side_ref, h_ref, xbuf_ref, sem = rest
    else:
        o_ref, h_ref, xbuf_ref, sem = rest
    i = pl.program_id(0)
    j = pl.program_id(1)

    @pl.when(j == 0)
    def _():
        tm = h_ref.shape[0]
        chunk = xbuf_ref.shape[1]

        def x_copy(c):
            src = x_hbm.at[pl.ds(pl.multiple_of(i * tm + c * chunk, chunk), chunk), :]
            return pltpu.make_async_copy(src, xbuf_ref.at[c % 2], sem.at[c % 2])

        x_copy(0).start()
        for c in range(tm // chunk):
            if (c + 1) * chunk < tm:
                x_copy(c + 1).start()
            x_copy(c).wait()
            rows = pl.ds(c * chunk, chunk)
            h = _norm_mod(xbuf_ref[c % 2], g_ref[...], sh_ref[...], sc_ref[...])
            h_ref[rows, :] = h.astype(BF16)
            if with_side:
                side_ref[rows, :] = _dot_f32(h, ws_ref[...]) + bs_ref[...]

    def project(weights_ref):
        o_ref[...] = jnp.dot(h_ref[...], weights_ref[...].astype(BF16),
                             preferred_element_type=F32).astype(o_ref.dtype)

    if with_side:
        pl.when(j < n_head)(lambda: project(w_ref))
        pl.when(j >= n_head)(lambda: project(wt_ref))
    else:
        project(w_ref)


def _nm_matmul(x, g, shift, scale, w, out_dtype, seq, head_cols=None, side=None):
    n, d = x.shape
    tm = _tile(seq, PROJ_ROWS)
    per_b = seq // tm
    nb = shift.shape[0]
    vec = lambda a: a.reshape(nb, 1, d)
    if side is None:
        nout = w.shape[1]
        tn = _tile(nout, 512)
        n_head = nout // tn
    else:
        nout = head_cols + side[0].shape[1]
        tn = _tile(head_cols, 512)
        assert side[0].shape[1] % tn == 0
        n_head = head_cols // tn
    in_specs = [
        pl.BlockSpec(memory_space=pl.ANY),
        pl.BlockSpec((1, d), lambda i, j: (0, 0)),
        pl.BlockSpec((None, 1, d), lambda i, j: (i // per_b, 0, 0)),
        pl.BlockSpec((None, 1, d), lambda i, j: (i // per_b, 0, 0)),
        pl.BlockSpec((d, tn), lambda i, j: (0, jnp.minimum(j, n_head - 1))),
    ]
    args = [x, g.reshape(1, d), vec(shift), vec(scale), w]
    out_specs = pl.BlockSpec((tm, tn), lambda i, j: (i, j))
    out_shape = jax.ShapeDtypeStruct((n, nout), out_dtype)
    if side is not None:
        in_specs += [pl.BlockSpec((d, tn), lambda i, j: (0, jnp.maximum(j - n_head, 0))),
                     pl.BlockSpec((d, LANES), lambda i, j: (0, 0)),
                     pl.BlockSpec((1, LANES), lambda i, j: (0, 0))]
        args += list(side)
        out_specs = [out_specs, pl.BlockSpec((tm, LANES), lambda i, j: (i, 0))]
        out_shape = [out_shape, jax.ShapeDtypeStruct((n, LANES), F32)]
    return pl.pallas_call(
        functools.partial(_nm_matmul_kernel, with_side=side is not None, n_head=n_head),
        grid=(n // tm, nout // tn),
        in_specs=in_specs,
        out_specs=out_specs,
        out_shape=out_shape,
        scratch_shapes=[
            pltpu.VMEM((tm, d), BF16),
            pltpu.VMEM((2, _tile(tm, NORM_ROWS), d), F32),
            pltpu.SemaphoreType.DMA((2,)),
        ],
        compiler_params=_params("parallel", "arbitrary"),
        name="norm_mod_matmul",
    )(*args)


def _mlstm_kernel(q_ref, qp_ref, k_ref, kp_ref, v_ref, o_ref, cw_ref, cb_ref, hg_ref, gc_ref, gr_ref,
                  out_ref, ext_ref, c_ref, n_ref, m_ref):
    chunk = pl.program_id(1)
    nh = c_ref.shape[0]
    L, dm = q_ref.shape
    dh = dm // nh

    @pl.when(chunk == 0)
    def _():
        c_ref[...] = jnp.zeros_like(c_ref)
        n_ref[...] = jnp.zeros_like(n_ref)
        m_ref[...] = jnp.zeros_like(m_ref)

    def conv_silu(cur_ref, prev_ref, off):
        ext_ref[pl.ds(0, SUBLANES), :] = jnp.where(chunk == 0, 0.0, prev_ref[...])
        ext_ref[pl.ds(SUBLANES, L), :] = cur_ref[...]
        y = cb_ref[:, pl.ds(off, dm)]
        for j in range(CONV_K):
            y = y + cw_ref[pl.ds(j, 1), pl.ds(off, dm)] * ext_ref[pl.ds(SUBLANES - CONV_K + 1 + j, L), :]
        return y * _sigmoid(y)

    q_all = conv_silu(q_ref, qp_ref, 0) * (dh ** -0.5)
    k_all = conv_silu(k_ref, kp_ref, dm)
    gc = gc_ref[...]
    row = lax.broadcasted_iota(I32, (L, L), 0)
    col = lax.broadcasted_iota(I32, (L, L), 1)
    causal = col <= row

    for head in range(nh):
        cols = pl.ds(head * dh, dh)
        q = q_all[:, head * dh:(head + 1) * dh]
        k = k_all[:, head * dh:(head + 1) * dh]
        qb = q.astype(BF16)
        kb = k.astype(BF16)
        vb = v_ref[:, cols].astype(BF16)

        i_col = gc[:, head:head + 1]
        f_col = gc[:, nh + head:nh + head + 1]
        i_row = gr_ref[pl.ds(head, 1), :]
        f_row = gr_ref[pl.ds(nh + head, 1), :]
        logf_col = _log_sigmoid(f_col)
        logf_row = _log_sigmoid(f_row)
        b_col = jnp.sum(jnp.where(causal, logf_row, 0.0), axis=1, keepdims=True)
        b_row = jnp.sum(jnp.where(row <= col, logf_col, 0.0), axis=0, keepdims=True)

        m_prev = m_ref[pl.ds(head, 1), pl.ds(0, 1)]
        n_prev = n_ref[pl.ds(head, 1), :]
        dmat = jnp.where(causal, b_col - b_row + i_row, -jnp.inf)
        inter = b_col + m_prev
        m_t = jnp.maximum(inter, jnp.max(dmat, axis=1, keepdims=True))
        w_inter = jnp.exp(inter - m_t)
        scores = lax.dot_general(qb, kb, (((1,), (1,)), ((), ())), preferred_element_type=F32)
        wmat = jnp.exp(dmat - m_t) * scores
        c_old = c_ref[head]
        num = (w_inter * jnp.dot(qb, c_old.astype(BF16), preferred_element_type=F32)
               + jnp.dot(wmat.astype(BF16), vb, preferred_element_type=F32))
        den = w_inter * jnp.sum(q * n_prev, axis=1, keepdims=True) + jnp.sum(wmat, axis=1, keepdims=True)
        hval = num / jnp.maximum(jnp.abs(den), jnp.exp(-m_t))

        g_end = jnp.sum(logf_row, axis=1, keepdims=True)
        wl = g_end - b_col + i_col
        m_new = jnp.maximum(g_end + m_prev, jnp.max(wl, axis=0, keepdims=True))
        decay = jnp.exp(g_end + m_prev - m_new)
        wk = jnp.exp(wl - m_new) * k
        c_ref[head] = decay * c_old + lax.dot_general(wk.astype(BF16), vb, (((0,), (0,)), ((), ())),
                                                      preferred_element_type=F32)
        n_ref[pl.ds(head, 1), :] = decay * n_prev + jnp.sum(wk, axis=0, keepdims=True)
        m_ref[pl.ds(head, 1), :] = jnp.broadcast_to(m_new, (1, m_ref.shape[1]))

        ms = jnp.mean(hval * hval, axis=-1, keepdims=True)
        hn = hval * lax.rsqrt(ms + EPS) * hg_ref[:, cols]
        out_ref[:, cols] = (hn * _sigmoid(o_ref[:, cols])).astype(out_ref.dtype)


def _mlstm(proj, gates_col, gates_row, conv_w, conv_b, head_g, nb, seq):
    n = proj.shape[0]
    nh = MLSTM_HEADS
    dm = head_g.shape[0]
    dh = dm // nh
    L = _tile(seq, MLSTM_TILE)
    nc = seq // L
    rb = L // SUBLANES

    assert 2 * nh <= SUBLANES

    def cur(off):
        return pl.BlockSpec((L, dm), lambda b, c: (b * nc + c, off))

    def prev(off):
        return pl.BlockSpec((SUBLANES, dm), lambda b, c: (jnp.maximum((b * nc + c) * rb - 1, 0), off))

    return pl.pallas_call(
        _mlstm_kernel,
        grid=(nb, nc),
        in_specs=[
            cur(0), prev(0), cur(1), prev(1), cur(2), cur(3),
            pl.BlockSpec((CONV_K, 2 * dm), lambda b, c: (0, 0)),
            pl.BlockSpec((1, 2 * dm), lambda b, c: (0, 0)),
            pl.BlockSpec((1, dm), lambda b, c: (0, 0)),
            pl.BlockSpec((L, LANES), lambda b, c: (b * nc + c, 0)),
            pl.BlockSpec((SUBLANES, L), lambda b, c: (0, b * nc + c)),
        ],
        out_specs=pl.BlockSpec((L, dm), lambda b, c: (b * nc + c, 0)),
        out_shape=jax.ShapeDtypeStruct((n, dm), BF16),
        scratch_shapes=[
            pltpu.VMEM((L + SUBLANES, dm), F32),
            pltpu.VMEM((nh, dh, dh), F32),
            pltpu.VMEM((SUBLANES, dh), F32),
            pltpu.VMEM((SUBLANES, LANES), F32),
        ],
        compiler_params=_params("parallel", "arbitrary"),
        name="mlstm",
    )(proj, proj, proj, proj, proj, proj, conv_w, conv_b.reshape(1, -1), head_g.reshape(1, dm),
      gates_col, gates_row)


def _pool_kernel(x_ref, xp_ref, pw_ref, ps_ref, o_ref, buf_ref, *, tiles_per_seq):
    ts, dp = x_ref.shape
    dg = dp // len(POOL_WINDOWS)
    t_in_seq = pl.program_id(0) % tiles_per_seq
    buf_ref[pl.ds(0, POOL_HALO), :] = jnp.where(t_in_seq == 0, 0.0, xp_ref[...])
    buf_ref[pl.ds(POOL_HALO, ts), :] = x_ref[...]
    pos = t_in_seq * ts + lax.broadcasted_iota(I32, (ts, 1), 0)
    for g, w in enumerate(POOL_WINDOWS):
        cols = pl.ds(g * dg, dg)
        xg = buf_ref[pl.ds(POOL_HALO, ts), cols]
        s = xg
        for j in range(1, w):
            s = s + buf_ref[pl.ds(POOL_HALO - j, ts), cols]
        cnt = jnp.minimum(pos + 1, w).astype(F32)
        pooled = s / cnt - xg
        mixed = jnp.dot(pooled.astype(BF16), pw_ref[g].astype(BF16), preferred_element_type=F32)
        o_ref[:, cols] = (mixed * ps_ref[:, cols]).astype(o_ref.dtype)


def _pool(proj, col_block, pool_w, pool_scale, seq):
    n = proj.shape[0]
    dp = pool_scale.shape[0]
    ts = _tile(seq, 512)
    hb = ts // POOL_HALO
    return pl.pallas_call(
        functools.partial(_pool_kernel, tiles_per_seq=seq // ts),
        grid=(n // ts,),
        in_specs=[
            pl.BlockSpec((ts, dp), lambda i: (i, col_block)),
            pl.BlockSpec((POOL_HALO, dp), lambda i: (jnp.maximum(i * hb - 1, 0), col_block)),
            pl.BlockSpec(pool_w.shape, lambda i: (0, 0, 0)),
            pl.BlockSpec((1, dp), lambda i: (0, 0)),
        ],
        out_specs=pl.BlockSpec((ts, dp), lambda i: (i, 0)),
        out_shape=jax.ShapeDtypeStruct((n, dp), BF16),
        scratch_shapes=[pltpu.VMEM((ts + POOL_HALO, dp), F32)],
        compiler_params=_params("parallel"),
        name="pool",
    )(proj, proj, pool_w, pool_scale.reshape(1, dp))


def _mm_res_kernel(*refs, n_a):
    a_refs, w_refs = refs[:n_a], refs[n_a:2 * n_a]
    x_ref, gate_ref, o_ref = refs[2 * n_a:]
    acc = jnp.dot(a_refs[0][...], w_refs[0][...].astype(BF16), preferred_element_type=F32)
    for a_ref, w_ref in zip(a_refs[1:], w_refs[1:]):
        acc = acc + jnp.dot(a_ref[...], w_ref[...].astype(BF16), preferred_element_type=F32)
    o_ref[...] = x_ref[...] + gate_ref[...] * acc


def _mm_res(a_list, w, x, gate, seq):
    n, d = x.shape
    n_a = len(a_list)
    ka = a_list[0].shape[1]
    assert all(a.shape[1] == ka for a in a_list) and w.shape[0] == n_a * ka
    tm = _tile(seq, PROJ_ROWS)
    tn = _tile(d, 512)
    per_b = seq // tm
    nb = gate.shape[0]
    in_specs = [pl.BlockSpec((tm, ka), lambda i, j: (i, 0)) for _ in a_list]
    in_specs += [pl.BlockSpec((ka, tn), functools.partial(lambda i, j, r: (r, j), r=r)) for r in range(n_a)]
    in_specs += [pl.BlockSpec((tm, tn), lambda i, j: (i, j)),
                 pl.BlockSpec((None, 1, tn), lambda i, j: (i // per_b, 0, j))]
    return pl.pallas_call(
        functools.partial(_mm_res_kernel, n_a=n_a),
        grid=(n // tm, d // tn),
        in_specs=in_specs,
        out_specs=pl.BlockSpec((tm, tn), lambda i, j: (i, j)),
        out_shape=jax.ShapeDtypeStruct((n, d), F32),
        compiler_params=_params("parallel", "parallel"),
        name="matmul_residual",
    )(*a_list, *([w] * n_a), x, gate.reshape(nb, 1, d))


def _ffn_kernel(x_ref, g_ref, sh_ref, sc_ref, gate_ref, w1_ref, w3_ref, w2_ref, o_ref, h_ref):
    f = pl.program_id(1)

    @pl.when(f == 0)
    def _():
        h_ref[...] = _norm_mod(x_ref[...], g_ref[...], sh_ref[...], sc_ref[...]).astype(BF16)
        o_ref[...] = jnp.zeros_like(o_ref)

    h = h_ref[...]
    a = jnp.dot(h, w1_ref[...].astype(BF16), preferred_element_type=F32)
    b = jnp.dot(h, w3_ref[...].astype(BF16), preferred_element_type=F32)
    act = (a * _sigmoid(a) * b).astype(BF16)
    o_ref[...] += jnp.dot(act, w2_ref[...].astype(BF16), preferred_element_type=F32)

    @pl.when(f == pl.num_programs(1) - 1)
    def _():
        o_ref[...] = x_ref[...] + gate_ref[...] * o_ref[...]


def _ffn(x, g, shift, scale, gate, w1, w3, w2, seq):
    n, d = x.shape
    dff = w1.shape[1]
    tm = _tile(seq, 1024)
    tf = _tile(dff, 256)
    per_b = seq // tm
    nb = gate.shape[0]
    vec = lambda a: a.reshape(nb, 1, d)
    bvec = pl.BlockSpec((None, 1, d), lambda i, f: (i // per_b, 0, 0))
    return pl.pallas_call(
        _ffn_kernel,
        grid=(n // tm, dff // tf),
        in_specs=[
            pl.BlockSpec((tm, d), lambda i, f: (i, 0), pipeline_mode=pl.Buffered(1)),
            pl.BlockSpec((1, d), lambda i, f: (0, 0)),
            bvec, bvec, bvec,
            pl.BlockSpec((d, tf), lambda i, f: (0, f)),
            pl.BlockSpec((d, tf), lambda i, f: (0, f)),
            pl.BlockSpec((tf, d), lambda i, f: (f, 0)),
        ],
        out_specs=pl.BlockSpec((tm, d), lambda i, f: (i, 0)),
        out_shape=jax.ShapeDtypeStruct((n, d), F32),
        scratch_shapes=[pltpu.VMEM((tm, d), BF16)],
        compiler_params=_params("parallel", "arbitrary"),
        name="swiglu_ffn",
    )(x, g.reshape(1, d), vec(shift), vec(scale), vec(gate), w1, w3, w2)


def _sb_attn_kernel(q_ref, k_ref, v_ref, o_ref):
    qi = pl.program_id(2)
    T = q_ref.shape[0]
    dh = q_ref.shape[1] // ATTN_HEADS_PER_STEP
    scale = dh ** -0.5
    row = lax.broadcasted_iota(I32, (T, T), 0)
    col = lax.broadcasted_iota(I32, (T, T), 1)
    later = (row > col).astype(BF16)

    def prepare(j, g, diagonal):
        start = pl.multiple_of(j * T, T)
        cols = pl.ds(g * dh, dh)
        kj = k_ref[pl.ds(start, T), cols]
        z = lax.dot_general(q_ref[:, cols], kj, (((1,), (1,)), ((), ())), preferred_element_type=F32) * scale
        sp = jnp.maximum(z, 0.0) + jnp.log(1.0 + jnp.exp(-jnp.abs(z)))
        lom = -sp
        if diagonal:
            strict = col < row
            lom = jnp.where(strict, lom, 0.0)
        hi = lom.astype(BF16)
        lo = (lom - hi.astype(F32)).astype(BF16)
        suffix = (jnp.dot(hi, later, preferred_element_type=F32)
                  + jnp.dot(lo, later, preferred_element_type=F32))
        base = z - sp + suffix
        if diagonal:
            base = jnp.where(strict, base, -jnp.inf)
        return base, jnp.sum(lom, axis=1, keepdims=True), v_ref[pl.ds(start, T), cols]

    def walk(tiles, accs, rems):
        new_accs, new_rems = [], []
        for g in range(ATTN_HEADS_PER_STEP):
            parts = [prepare(j, g, diagonal) for j, diagonal in tiles]
            acc, rem = accs[g], rems[g]
            for base, total, vj in parts:
                a = jnp.exp(base + rem)
                acc = acc + jnp.dot(a.astype(BF16), vj, preferred_element_type=F32)
                rem = rem + total
            new_accs.append(acc)
            new_rems.append(rem)
        return tuple(new_accs), tuple(new_rems)

    def alive(rems):
        top = rems[0]
        for r in rems[1:]:
            top = jnp.maximum(top, r)
        return jnp.max(top) > ATTN_LOG_CUTOFF

    def finish(accs):
        for g in range(ATTN_HEADS_PER_STEP):
            o_ref[:, pl.ds(g * dh, dh)] = accs[g].astype(o_ref.dtype)

    zeros = lambda w: tuple(jnp.zeros((T, w), F32) for _ in range(ATTN_HEADS_PER_STEP))

    @pl.when(qi == 0)
    def _():
        accs, _ = walk([(qi, True)], zeros(dh), zeros(1))
        finish(accs)

    @pl.when(qi > 0)
    def _():
        accs, rems = walk([(qi, True), (qi - 1, False)], zeros(dh), zeros(1))

        def cond(c):
            s, live, _, _ = c
            return jnp.logical_and(s < qi, live)

        def body(c):
            s, _, accs, rems = c
            accs, rems = walk([(qi - 1 - s, False)], accs, rems)
            return s + 1, alive(rems), accs, rems

        _, _, accs, _ = lax.while_loop(cond, body, (jnp.int32(1), alive(rems), accs, rems))
        finish(accs)


def _sb_attention(qkv, nb, seq):
    n, d3 = qkv.shape
    d = d3 // 3
    nh = SB_HEADS
    dh = d // nh
    T = _tile(seq, ATTN_TILE)
    nq = seq // T
    hg = ATTN_HEADS_PER_STEP
    ng = nh // hg
    return pl.pallas_call(
        _sb_attn_kernel,
        grid=(nb, ng, nq),
        in_specs=[
            pl.BlockSpec((T, hg * dh), lambda b, h, i: (b * nq + i, h)),
            pl.BlockSpec((seq, hg * dh), lambda b, h, i: (b, ng + h)),
            pl.BlockSpec((seq, hg * dh), lambda b, h, i: (b, 2 * ng + h)),
        ],
        out_specs=pl.BlockSpec((T, hg * dh), lambda b, h, i: (b * nq + i, h)),
        out_shape=jax.ShapeDtypeStruct((n, d), BF16),
        compiler_params=_params("parallel", "parallel", "parallel"),
        name="stick_breaking_attention",
    )(qkv, qkv, qkv)


def _router_kernel(x_ref, g_ref, sh_ref, sc_ref, wr_ref, h_ref, info_ref, gate_ref, cnt_ref, carry_ref):
    tm = x_ref.shape[0]

    @pl.when(pl.program_id(0) == 0)
    def _():
        carry_ref[...] = jnp.zeros_like(carry_ref)

    h = _norm_mod(x_ref[...], g_ref[...], sh_ref[...], sc_ref[...])
    h_ref[...] = h
    logits = _dot_f32(h, wr_ref[...])
    lane = lax.broadcasted_iota(I32, (tm, LANES), 1)
    lane_f = lane.astype(F32)
    lg = jnp.where(lane < N_EXPERTS, logits, -jnp.inf)
    v0 = jnp.max(lg, axis=1, keepdims=True)
    i0 = jnp.min(jnp.where(lg == v0, lane_f, float(LANES)), axis=1, keepdims=True)
    lg1 = jnp.where(lane_f == i0, -jnp.inf, lg)
    v1 = jnp.max(lg1, axis=1, keepdims=True)
    i1 = jnp.min(jnp.where(lg1 == v1, lane_f, float(LANES)), axis=1, keepdims=True)
    ex = jnp.exp(v1 - v0)
    g0 = 1.0 / (1.0 + ex)
    g1 = ex / (1.0 + ex)
    sel0 = lane_f == i0
    sel1 = lane_f == i1
    onehot = jnp.where(sel0 | sel1, 1.0, 0.0)
    row = lax.broadcasted_iota(I32, (tm, tm), 0)
    col = lax.broadcasted_iota(I32, (tm, tm), 1)
    before = (col < row).astype(BF16)
    earlier = jnp.dot(before, onehot.astype(BF16), preferred_element_type=F32) + carry_ref[...]
    r0 = jnp.sum(jnp.where(sel0, earlier, 0.0), axis=1, keepdims=True)
    r1 = jnp.sum(jnp.where(sel1, earlier, 0.0), axis=1, keepdims=True)
    carry_ref[...] += jnp.sum(onehot, axis=0, keepdims=True)
    info = jnp.where(lane == 0, i0, jnp.where(lane == 1, i1, jnp.where(lane == 2, r0, jnp.where(lane == 3, r1, 0.0))))
    info_ref[...] = info.astype(I32)
    gate_ref[...] = jnp.where(lane == 0, g0, jnp.where(lane == 1, g1, 0.0))
    cnt_ref[...] = carry_ref[...].astype(I32)


def _router(x, g, shift, scale, w_router, seq):
    n, d = x.shape
    tm = _tile(seq, 512)
    per_b = seq // tm
    nb = shift.shape[0]
    vec = lambda a: a.reshape(nb, 1, d)
    bvec = pl.BlockSpec((None, 1, d), lambda i: (i // per_b, 0, 0))
    wr = jnp.pad(w_router, ((0, 0), (0, LANES - w_router.shape[1])))
    return pl.pallas_call(
        _router_kernel,
        grid=(n // tm,),
        in_specs=[
            pl.BlockSpec((tm, d), lambda i: (i, 0)),
            pl.BlockSpec((1, d), lambda i: (0, 0)),
            bvec, bvec,
            pl.BlockSpec((d, LANES), lambda i: (0, 0)),
        ],
        out_specs=[
            pl.BlockSpec((tm, d), lambda i: (i, 0)),
            pl.BlockSpec((tm, LANES), lambda i: (i, 0)),
            pl.BlockSpec((tm, LANES), lambda i: (i, 0)),
            pl.BlockSpec((1, LANES), lambda i: (0, 0)),
        ],
        out_shape=[
            jax.ShapeDtypeStruct((n, d), F32),
            jax.ShapeDtypeStruct((n, LANES), I32),
            jax.ShapeDtypeStruct((n, LANES), F32),
            jax.ShapeDtypeStruct((1, LANES), I32),
        ],
        scratch_shapes=[pltpu.VMEM((1, LANES), F32)],
        compiler_params=_params("arbitrary"),
        name="router",
    )(x, g.reshape(1, d), vec(shift), vec(scale), wr)


def _route_plan(info, counts, tm):
    n = info.shape[0]
    ne = N_EXPERTS
    e0, e1, r0, r1 = info[:, 0], info[:, 1], info[:, 2], info[:, 3]
    cnt = counts[0, :ne]
    padded = (cnt + MOE_SUB - 1) // MOE_SUB * MOE_SUB
    ends = jnp.cumsum(padded)
    starts = ends - padded
    pos0 = (starts[e0] + r0).astype(I32)
    pos1 = (starts[e1] + r1).astype(I32)
    r_pad = TOP_K * n + ne * MOE_SUB
    nv = -(-r_pad // tm) + ne
    ntiles = (padded + tm - 1) // tm
    v_end = jnp.cumsum(ntiles)
    v_first = v_end - ntiles
    total = v_end[-1]
    vid = jnp.arange(nv, dtype=I32)
    vc = jnp.minimum(vid, total - 1)
    grp = jnp.sum((v_end[None, :] <= vc[:, None]).astype(I32), axis=1)
    k = vc - v_first[grp]
    v_start = (starts[grp] + k * tm).astype(I32)
    v_nsub = jnp.where(vid < total, jnp.minimum(tm, padded[grp] - k * tm) // MOE_SUB, 0).astype(I32)
    tail = ends[-1] + MOE_SUB * jnp.arange(ne, dtype=I32)
    z_start = jnp.concatenate([jnp.maximum(ends - MOE_SUB, 0), tail]).astype(I32)
    z_on = jnp.concatenate([padded > 0, tail < r_pad]).astype(I32)
    return pos0, pos1, (grp.astype(I32), v_start, v_nsub), (z_start, z_on), r_pad


def _dispatch_kernel(p0_ref, p1_ref, zs_ref, zn_ref, h_ref, hs_hbm, zero_ref, sem, zsem):
    i = pl.program_id(0)
    td = h_ref.shape[0]
    sub = zero_ref.shape[0]

    def zero_copy(b):
        dst = hs_hbm.at[pl.ds(pl.multiple_of(zs_ref[b], sub), sub), :]
        return pltpu.make_async_copy(zero_ref, dst, zsem)

    @pl.when(i == 0)
    def _():
        zero_ref[...] = jnp.zeros_like(zero_ref)
        for b in range(zs_ref.shape[0]):
            pl.when(zn_ref[b] == 1)(lambda: zero_copy(b).start())
        for b in range(zs_ref.shape[0]):
            pl.when(zn_ref[b] == 1)(lambda: zero_copy(b).wait())

    def issue(r, c):
        t = i * td + r
        src = h_ref.at[pl.ds(r, 1), :]
        pltpu.make_async_copy(src, hs_hbm.at[pl.ds(p0_ref[t], 1), :], sem.at[0]).start()
        pltpu.make_async_copy(src, hs_hbm.at[pl.ds(p1_ref[t], 1), :], sem.at[1]).start(priority=1)
        return c
    lax.fori_loop(0, td, issue, 0, unroll=DMA_UNROLL)
    for stream in range(TOP_K):
        pltpu.make_async_copy(h_ref, hs_hbm.at[pl.ds(0, td), :], sem.at[stream]).wait()


def _dispatch(h, pos0, pos1, zero_blocks, r_pad, seq):
    n, d = h.shape
    td = _tile(seq, 512)
    z_start, z_on = zero_blocks
    return pl.pallas_call(
        _dispatch_kernel,
        grid_spec=pltpu.PrefetchScalarGridSpec(
            num_scalar_prefetch=4,
            grid=(n // td,),
            in_specs=[pl.BlockSpec((td, d), lambda i, p0, p1, zs, zn: (i, 0))],
            out_specs=pl.BlockSpec(memory_space=pl.ANY),
            scratch_shapes=[
                pltpu.VMEM((MOE_SUB, d), F32),
                pltpu.SemaphoreType.DMA((TOP_K,)),
                pltpu.SemaphoreType.DMA(()),
            ],
        ),
        out_shape=jax.ShapeDtypeStruct((r_pad, d), F32),
        compiler_params=_params("arbitrary"),
        name="moe_dispatch",
    )(pos0, pos1, z_start, z_on, h)


def _moe_kernel(vg_ref, vs_ref, vn_ref, hs_hbm, w1_ref, w3_ref, w2_ref, y_hbm,
                stage_ref, hb_ref, acc_ref, wb1_ref, wb3_ref, wb2_ref, sem_in, sem_out):
    v = pl.program_id(0)
    f = pl.program_id(1)
    sub = stage_ref.shape[1]
    start = vs_ref[v]
    nsub = vn_ref[v]

    def rows(sb):
        return pl.ds(pl.multiple_of(sb * sub, sub), sub)

    def hbm_rows(row0, sb):
        return pl.ds(pl.multiple_of(row0 + sb * sub, sub), sub)

    def in_copy(sb, slot):
        return pltpu.make_async_copy(hs_hbm.at[hbm_rows(start, sb), :], stage_ref.at[slot], sem_in.at[slot])

    def out_copy(row0, sb):
        return pltpu.make_async_copy(acc_ref.at[rows(sb), :], y_hbm.at[hbm_rows(row0, sb), :], sem_out)

    def each_block(count, fn):
        def body(sb, c):
            fn(sb)
            return c
        lax.fori_loop(0, count, body, 0)

    @pl.when(f == 0)
    def _():
        @pl.when(nsub > 0)
        def _():
            in_copy(0, 0).start()

        def load(sb):
            slot = sb % 2

            @pl.when(sb + 1 < nsub)
            def _():
                in_copy(sb + 1, 1 - slot).start()
            in_copy(sb, slot).wait()
            hb_ref[rows(sb), :] = stage_ref[slot].astype(BF16)
        each_block(nsub, load)

        prev = jnp.maximum(v - 1, 0)
        n_prev = jnp.where(v > 0, vn_ref[prev], 0)
        each_block(n_prev, lambda sb: out_copy(vs_ref[prev], sb).wait())

        def clear(sb):
            acc_ref[rows(sb), :] = jnp.zeros((sub, acc_ref.shape[1]), F32)
        each_block(nsub, clear)

    @pl.when(nsub > 0)
    def _():
        def cast_weights():
            w1b = w1_ref[...].astype(BF16)
            w3b = w3_ref[...].astype(BF16)
            w2b = w2_ref[...].astype(BF16)
            wb1_ref[...] = w1b
            wb3_ref[...] = w3b
            wb2_ref[...] = w2b
            return w1b, w3b, w2b

        def blocks(sb, count, first=False):
            w1b, w3b, w2b = cast_weights() if first else (wb1_ref[...], wb3_ref[...], wb2_ref[...])
            r = pl.ds(pl.multiple_of(sb * sub, sub), count * sub)
            hs = hb_ref[r, :]
            a = jnp.dot(hs, w1b, preferred_element_type=F32)
            b = jnp.dot(hs, w3b, preferred_element_type=F32)
            act = (a * _sigmoid(a) * b).astype(BF16)
            acc_ref[r, :] += jnp.dot(act, w2b, preferred_element_type=F32)

        n_long = nsub // MOE_LONG
        n_rest = nsub - n_long * MOE_LONG
        pl.when(n_long > 0)(lambda: blocks(0, MOE_LONG, first=True))
        each_block(n_long - 1, lambda i: blocks((i + 1) * MOE_LONG, MOE_LONG))
        for rest in range(1, MOE_LONG):
            pl.when(jnp.logical_and(n_rest == rest, n_long > 0))(lambda: blocks(n_long * MOE_LONG, rest))
            pl.when(jnp.logical_and(n_rest == rest, n_long == 0))(lambda: blocks(0, rest, first=True))

    @pl.when(f == pl.num_programs(1) - 1)
    def _():
        each_block(nsub, lambda sb: out_copy(start, sb).start())

        @pl.when(v == pl.num_programs(0) - 1)
        def _():
            each_block(nsub, lambda sb: out_copy(start, sb).wait())


def _moe(hs, visits, w1, w3, w2, tm):
    r_pad, d = hs.shape
    _, _, dff = w1.shape
    v_group, v_start, v_nsub = visits
    nv = v_group.shape[0]
    tf = _tile(dff, 256)
    nf = dff // tf

    def f_eff(v, f, vn):
        return jnp.where(vn[v] > 0, f, nf - 1)

    return pl.pallas_call(
        _moe_kernel,
        grid_spec=pltpu.PrefetchScalarGridSpec(
            num_scalar_prefetch=3,
            grid=(nv, nf),
            in_specs=[
                pl.BlockSpec(memory_space=pl.ANY),
                pl.BlockSpec((None, d, tf), lambda v, f, vg, vs, vn: (vg[v], 0, f_eff(v, f, vn))),
                pl.BlockSpec((None, d, tf), lambda v, f, vg, vs, vn: (vg[v], 0, f_eff(v, f, vn))),
                pl.BlockSpec((None, tf, d), lambda v, f, vg, vs, vn: (vg[v], f_eff(v, f, vn), 0)),
            ],
            out_specs=pl.BlockSpec(memory_space=pl.ANY),
            scratch_shapes=[
                pltpu.VMEM((2, MOE_SUB, d), F32),
                pltpu.VMEM((tm, d), BF16),
                pltpu.VMEM((tm, d), F32),
                pltpu.VMEM((d, tf), BF16),
                pltpu.VMEM((d, tf), BF16),
                pltpu.VMEM((tf, d), BF16),
                pltpu.SemaphoreType.DMA((2,)),
                pltpu.SemaphoreType.DMA(()),
            ],
        ),
        out_shape=jax.ShapeDtypeStruct((r_pad, d), F32),
        input_output_aliases={3: 0},
        compiler_params=_params("arbitrary", "arbitrary"),
        name="moe_experts",
    )(v_group, v_start, v_nsub, hs, w1, w3, w2)


def _combine_kernel(p0_ref, p1_ref, y_hbm, x_ref, gate_ref, rg_ref, gf_ref, o_ref, buf_ref, sem):
    i = pl.program_id(0)
    tc = x_ref.shape[0]
    slot = i % 2

    def gather(tile, into):
        def issue(r, c):
            t = tile * tc + r
            for k, p_ref in enumerate((p0_ref, p1_ref)):
                pltpu.make_async_copy(y_hbm.at[pl.ds(p_ref[t], 1), :],
                                      buf_ref.at[into, k, pl.ds(r, 1), :], sem.at[into, k]).start(priority=k)
            return c
        lax.fori_loop(0, tc, issue, 0, unroll=DMA_UNROLL)

    @pl.when(i == 0)
    def _():
        gather(0, 0)

    @pl.when(i + 1 < pl.num_programs(0))
    def _():
        gather(i + 1, 1 - slot)

    for k in range(TOP_K):
        pltpu.make_async_copy(y_hbm.at[pl.ds(0, tc), :], buf_ref.at[slot, k], sem.at[slot, k]).wait()

    rg = rg_ref[...]
    moe = rg[:, 0:1] * buf_ref[slot, 0] + rg[:, 1:2] * buf_ref[slot, 1]
    xn = x_ref[...] + gate_ref[...] * moe
    ms = jnp.mean(xn * xn, axis=-1, keepdims=True)
    o_ref[...] = xn * lax.rsqrt(ms + EPS) * gf_ref[...]


def _combine(y, pos0, pos1, route_gates, x, gate, g_final, seq):
    n, d = x.shape
    tc = _tile(seq, 512)
    per_b = seq // tc
    nb = gate.shape[0]
    return pl.pallas_call(
        _combine_kernel,
        grid_spec=pltpu.PrefetchScalarGridSpec(
            num_scalar_prefetch=2,
            grid=(n // tc,),
            in_specs=[
                pl.BlockSpec(memory_space=pl.ANY),
                pl.BlockSpec((tc, d), lambda i, p0, p1: (i, 0)),
                pl.BlockSpec((None, 1, d), lambda i, p0, p1: (i // per_b, 0, 0)),
                pl.BlockSpec((tc, LANES), lambda i, p0, p1: (i, 0)),
                pl.BlockSpec((1, d), lambda i, p0, p1: (0, 0)),
            ],
            out_specs=pl.BlockSpec((tc, d), lambda i, p0, p1: (i, 0)),
            scratch_shapes=[
                pltpu.VMEM((2, TOP_K, tc, d), F32),
                pltpu.SemaphoreType.DMA((2, TOP_K)),
            ],
        ),
        out_shape=jax.ShapeDtypeStruct((n, d), F32),
        compiler_params=_params("arbitrary"),
        name="moe_combine_final_norm",
    )(pos0, pos1, y, x, gate.reshape(nb, 1, d), route_gates, g_final.reshape(1, d))


def _even_layer(x, mod, seq, g_mix, g_ffn, w_in, b_gates, conv_w, conv_b, head_g, pool_w, pool_scale,
                w_out, w1, w3, w2):
    n, d = x.shape
    nb = mod.shape[0]
    sh1, sc1, g1, sh2, sc2, g2 = jnp.split(mod, 6, axis=-1)
    dm = head_g.shape[0]
    dp = pool_scale.shape[0]
    ng = 2 * MLSTM_HEADS
    assert dm == dp and w_in.shape[1] == 4 * dm + ng + dp
    w_pool = w_in[:, 4 * dm + ng:]
    w_gate = jnp.pad(w_in[:, 4 * dm:4 * dm + ng], ((0, 0), (0, LANES - ng)))
    b_gate = jnp.pad(b_gates, (0, LANES - ng)).reshape(1, LANES)
    proj, gates = _nm_matmul(x, g_mix, sh1, sc1, w_in, F32, seq, head_cols=4 * dm,
                             side=(w_pool, w_gate, b_gate))
    gates_row = gates[:, :SUBLANES].T
    hm = _mlstm(proj, gates, gates_row, conv_w, conv_b, head_g, nb, seq)
    hp = _pool(proj, 4 * dm // dp, pool_w, pool_scale, seq)
    x = _mm_res([hm, hp], w_out, x, g1, seq)
    return _ffn(x, g_ffn, sh2, sc2, g2, w1, w3, w2, seq)


def _odd_layer(x, mod, seq, g_mix, g_ffn, g_final, w_qkv, w_o, w_router, w1, w3, w2):
    n, d = x.shape
    nb = mod.shape[0]
    sh1, sc1, g1, sh2, sc2, g2 = jnp.split(mod, 6, axis=-1)
    qkv = _nm_matmul(x, g_mix, sh1, sc1, w_qkv, BF16, seq)
    att = _sb_attention(qkv, nb, seq)
    x = _mm_res([att], w_o, x, g1, seq)
    h, info, gates, counts = _router(x, g_ffn, sh2, sc2, w_router, seq)
    tm = MOE_TILE_SUBS * MOE_SUB
    pos0, pos1, visits, zero_blocks, r_pad = _route_plan(info, counts, tm)
    hs = _dispatch(h, pos0, pos1, zero_blocks, r_pad, seq)
    short = tuple(a[:N_EXPERTS] for a in visits)
    y = lax.cond(visits[2][N_EXPERTS] == 0,
                 lambda rows: _moe(rows, short, w1, w3, w2, tm),
                 lambda rows: _moe(rows, visits, w1, w3, w2, tm), hs)
    return _combine(y, pos0, pos1, gates, x, g2, g_final, seq)


def kernel(x, c, ada_w, ada_b, norm_mix_g, norm_ffn_g, norm_final_g, ev_w_in, ev_b_gates, ev_conv_w,
           ev_conv_b, ev_head_g, ev_pool_w, ev_pool_scale, ev_w_out, ev_ffn_w1, ev_ffn_w3, ev_ffn_w2,
           od_w_qkv, od_w_o, od_router, od_moe_w1, od_moe_w3, od_moe_w2):
    nb, seq, d = x.shape
    assert ada_w.shape[0] == 2, "one even and one odd layer"
    mod = _adaln(c, ada_w, ada_b)
    xf = x.reshape(nb * seq, d)
    xf = _even_layer(xf, mod[0], seq, norm_mix_g[0], norm_ffn_g[0], ev_w_in[0], ev_b_gates[0],
                     ev_conv_w[0], ev_conv_b[0], ev_head_g[0], ev_pool_w[0], ev_pool_scale[0],
                     ev_w_out[0], ev_ffn_w1[0], ev_ffn_w3[0], ev_ffn_w2[0])
    out = _odd_layer(xf, mod[1], seq, norm_mix_g[1], norm_ffn_g[1], norm_final_g, od_w_qkv[0], od_w_o[0],
                     od_router[0], od_moe_w1[0], od_moe_w3[0], od_moe_w2[0])
    return out.reshape(nb, seq, d)
```
